```python
import functools
import jax, jax.numpy as jnp
from jax import lax
import numpy as np

D_MODEL = 4096
BATCH = 2
SEQ = 8192
DEPTH = 1
DEC_BATCH = 16
DEC_SEQ = 64
PAST_LEN = 4096

CHUNK = 64
QBLOCK = 128
EPS = 1e-6
MLA_HEADS = D_MODEL // 256
QK_NOPE = 128
QK_ROPE = 64
V_HEAD = 128
Q_LORA = D_MODEL // 4
KV_LORA = D_MODEL // 8
ROPE_THETA = 10000.0
MLA_SCALE = (QK_NOPE + QK_ROPE) ** -0.5
MLA_WIDTH = MLA_HEADS * V_HEAD
SB_HEADS = D_MODEL // 256
SB_HEAD_DIM = 128
SB_WIDTH = SB_HEADS * SB_HEAD_DIM
SB_SCALE = SB_HEAD_DIM ** -0.5
N_MEM = 256
MEM_HEADS = 4
MEM_HEAD_DIM = 128
MEM_WIDTH = MEM_HEADS * MEM_HEAD_DIM
MEM_SCALE = MEM_HEAD_DIM ** -0.5
D_FF = ((8 * D_MODEL + 768 - 1) // 768) * 256
_O1 = Q_LORA
_O2 = _O1 + KV_LORA
_O3 = _O2 + QK_ROPE
_O4 = _O3 + SB_WIDTH
_O5 = _O4 + SB_WIDTH
_O6 = _O5 + SB_WIDTH
IN_SPLITS = (_O1, _O2, _O3, _O4, _O5, _O6)
IN_WIDTH = _O6 + 2 * D_MODEL

kernel_name = "mla_stickbreak_gated_streaming_encoder"


def _rmsnorm(x, g):
    x32 = x.astype(jnp.float32)
    y = x32 * lax.rsqrt(jnp.mean(x32 * x32, axis=-1, keepdims=True) + EPS)
    return (y * g.astype(jnp.float32)).astype(x.dtype)


def _rope(x, pos):
    half = x.shape[-1] // 2
    inv = ROPE_THETA ** (-jnp.arange(half, dtype=jnp.float32) / half)
    ang = pos.astype(jnp.float32)[:, None] * inv[None, :]
    cos = jnp.cos(ang)[None, :, None, :]
    sin = jnp.sin(ang)[None, :, None, :]
    x32 = x.astype(jnp.float32)
    x1, x2 = x32[..., :half], x32[..., half:]
    return jnp.concatenate([x1 * cos - x2 * sin, x1 * sin + x2 * cos], axis=-1).astype(x.dtype)


def _sweep_queries(block_fn, q_arrays, q_pos):
    t = q_pos.shape[0]
    if t <= QBLOCK:
        return block_fn(*q_arrays, q_pos)
    nb = t // QBLOCK

    def to_blocks(a):
        return jnp.swapaxes(a.reshape((a.shape[0], nb, QBLOCK) + a.shape[2:]), 0, 1)

    blocks = tuple(to_blocks(a) for a in q_arrays) + (q_pos.reshape(nb, QBLOCK),)
    out = lax.map(lambda args: block_fn(*args), blocks)
    out = jnp.swapaxes(out, 0, 1)
    return out.reshape((out.shape[0], t) + out.shape[3:])


def _mla_block(q_lat, q_rope, q_pos, c_kv, k_rope, k_pos, w_uv):
    s = (jnp.einsum('bqhc,bkc->bhqk', q_lat, c_kv)
         + jnp.einsum('bqhr,bkr->bhqk', q_rope, k_rope)).astype(jnp.float32) * MLA_SCALE
    allowed = (k_pos // CHUNK)[None, :] <= (q_pos // CHUNK)[:, None]
    s = jnp.where(allowed[None, None], s, -jnp.inf)
    p = jax.nn.softmax(s, axis=-1).astype(c_kv.dtype)
    o_lat = jnp.einsum('bhqk,bkc->bqhc', p, c_kv)
    return jnp.einsum('bqhc,chv->bqhv', o_lat, w_uv)


def _sb_block(q, q_pos, k, v, k_pos):
    z = jnp.einsum('bqhd,bkhd->bhqk', q, k).astype(jnp.float32) * SB_SCALE
    valid = (k_pos[None, :] < q_pos[:, None])[None, None]
    log_keep = jnp.where(valid, jax.nn.log_sigmoid(-z), 0.0)
    suffix = lax.cumsum(log_keep, axis=3, reverse=True) - log_keep
    w = jnp.where(valid, jnp.exp(jax.nn.log_sigmoid(z) + suffix), 0.0)
    return jnp.einsum('bhqk,bkhd->bqhd', w.astype(v.dtype), v)


def _mem_kv(mem, g_mem, w_mk, w_mv):
    b, m, _ = mem.shape
    mn = _rmsnorm(mem, g_mem)
    k = (mn @ w_mk).reshape(b, m, MEM_HEADS, MEM_HEAD_DIM)
    v = (mn @ w_mv).reshape(b, m, MEM_HEADS, MEM_HEAD_DIM)
    return k, v


def _layer(x, pos, past, mem_k, mem_v, g_mix, w_in, b_gate, g_q_lat, w_uq, g_kv_lat, w_uk, w_uv,
           w_branch_a, w_branch_b, w_out, g_xattn, w_mq, w_mo, g_ffn, w_gate, w_up, w_down):
    b, t, _ = x.shape
    h = _rmsnorm(x, g_mix)
    c_q, c_kv, k_rope_in, sb_q, sb_k, sb_v, gate_logits = jnp.split(h @ w_in, IN_SPLITS, axis=-1)
    q = jnp.einsum('btc,chd->bthd', _rmsnorm(c_q, g_q_lat), w_uq)
    q_lat = jnp.einsum('bthn,chn->bthc', q[..., :QK_NOPE], w_uk)
    q_rope = _rope(q[..., QK_NOPE:], pos)
    ckv = _rmsnorm(c_kv, g_kv_lat)
    krope = _rope(k_rope_in[:, :, None, :], pos)[:, :, 0, :]
    sbq = sb_q.reshape(b, t, SB_HEADS, SB_HEAD_DIM)
    sbk = sb_k.reshape(b, t, SB_HEADS, SB_HEAD_DIM)
    sbv = sb_v.reshape(b, t, SB_HEADS, SB_HEAD_DIM)
    if past is None:
        all_ckv, all_krope, all_k, all_v = ckv, krope, sbk, sbv
    else:
        p_ckv, p_krope, p_k, p_v = past
        all_ckv = jnp.concatenate([p_ckv, ckv], axis=1)
        all_krope = jnp.concatenate([p_krope, krope], axis=1)
        all_k = jnp.concatenate([p_k, sbk], axis=1)
        all_v = jnp.concatenate([p_v, sbv], axis=1)
    k_pos = jnp.arange(all_ckv.shape[1])
    mla_fn = functools.partial(_mla_block, c_kv=all_ckv, k_rope=all_krope, k_pos=k_pos, w_uv=w_uv)
    o_a = _sweep_queries(mla_fn, (q_lat, q_rope), pos).reshape(b, t, MLA_WIDTH)
    sb_fn = functools.partial(_sb_block, k=all_k, v=all_v, k_pos=k_pos)
    o_b = _sweep_queries(sb_fn, (sbq,), pos).reshape(b, t, SB_WIDTH)
    gates = jax.nn.sigmoid(gate_logits + b_gate)
    gate_a, gate_b = gates[..., :D_MODEL], gates[..., D_MODEL:]
    merged = gate_a * (o_a @ w_branch_a) + gate_b * (o_b @ w_branch_b)
    x = x + merged @ w_out
    mq = (_rmsnorm(x, g_xattn) @ w_mq).reshape(b, t, MEM_HEADS, MEM_HEAD_DIM)
    s = jnp.einsum('bthd,bmhd->bhtm', mq, mem_k).astype(jnp.float32) * MEM_SCALE
    p = jax.nn.softmax(s, axis=-1).astype(mem_v.dtype)
    x = x + jnp.einsum('bhtm,bmhd->bthd', p, mem_v).reshape(b, t, MEM_WIDTH) @ w_mo
    h = _rmsnorm(x, g_ffn)
    x = x + (jax.nn.silu(h @ w_gate) * (h @ w_up)) @ w_down
    return x, (ckv, krope, sbk, sbv)


def setup_inputs(seed: int = 0) -> dict:
    key = jax.random.key(seed)
    ks = iter(jax.random.split(key, 40))
    f32 = jnp.float32

    def nrm(shape, scale=1.0):
        return jax.random.normal(next(ks), shape, f32) * scale

    def gain(shape):
        return 1.0 + 0.05 * jax.random.normal(next(ks), shape, f32)

    L = DEPTH
    return {
        'x_prompt': nrm((BATCH, SEQ, D_MODEL)),
        'x_sample': nrm((DEC_BATCH, DEC_SEQ, D_MODEL)),
        'cache_mla_ckv': nrm((L, DEC_BATCH, PAST_LEN, KV_LORA)),
        'cache_mla_krope': nrm((L, DEC_BATCH, PAST_LEN, QK_ROPE)),
        'cache_sb_k': nrm((L, DEC_BATCH, PAST_LEN, SB_HEADS, SB_HEAD_DIM)),
        'cache_sb_v': nrm((L, DEC_BATCH, PAST_LEN, SB_HEADS, SB_HEAD_DIM)),
        'cache_mem_k': nrm((L, DEC_BATCH, N_MEM, MEM_HEADS, MEM_HEAD_DIM)),
        'cache_mem_v': nrm((L, DEC_BATCH, N_MEM, MEM_HEADS, MEM_HEAD_DIM)),
        'mem_prompt': nrm((BATCH, N_MEM, D_MODEL)),
        'g_mix': gain((L, D_MODEL)),
        'w_in': nrm((L, D_MODEL, IN_WIDTH), D_MODEL ** -0.5),
        'b_gate': nrm((L, 2 * D_MODEL), 0.1),
        'g_q_lat': gain((L, Q_LORA)),
        'w_uq': nrm((L, Q_LORA, MLA_HEADS, QK_NOPE + QK_ROPE), Q_LORA ** -0.5),
        'g_kv_lat': gain((L, KV_LORA)),
        'w_uk': nrm((L, KV_LORA, MLA_HEADS, QK_NOPE), KV_LORA ** -0.5),
        'w_uv': nrm((L, KV_LORA, MLA_HEADS, V_HEAD), KV_LORA ** -0.5),
        'w_branch_a': nrm((L, MLA_WIDTH, D_MODEL), MLA_WIDTH ** -0.5),
        'w_branch_b': nrm((L, SB_WIDTH, D_MODEL), SB_WIDTH ** -0.5),
        'w_out': nrm((L, D_MODEL, D_MODEL), D_MODEL ** -0.5),
        'g_xattn': gain((L, D_MODEL)),
        'g_mem': gain((L, D_MODEL)),
        'w_mq': nrm((L, D_MODEL, MEM_WIDTH), D_MODEL ** -0.5),
        'w_mk': nrm((L, D_MODEL, MEM_WIDTH), D_MODEL ** -0.5),
        'w_mv': nrm((L, D_MODEL, MEM_WIDTH), D_MODEL ** -0.5),
        'w_mo': nrm((L, MEM_WIDTH, D_MODEL), MEM_WIDTH ** -0.5),
        'g_ffn': gain((L, D_MODEL)),
        'w_gate': nrm((L, D_MODEL, D_FF), D_MODEL ** -0.5),
        'w_up': nrm((L, D_MODEL, D_FF), D_MODEL ** -0.5),
        'w_down': nrm((L, D_FF, D_MODEL), D_FF ** -0.5),
        'g_final': gain((D_MODEL,)),
    }


def reference(x_prompt, x_sample, cache_mla_ckv, cache_mla_krope, cache_sb_k, cache_sb_v,
              cache_mem_k, cache_mem_v, mem_prompt, g_mix, w_in, b_gate, g_q_lat, w_uq, g_kv_lat,
              w_uk, w_uv, w_branch_a, w_branch_b, w_out, g_xattn, g_mem, w_mq, w_mk, w_mv, w_mo,
              g_ffn, w_gate, w_up, w_down, g_final):
    pos_p = jnp.arange(x_prompt.shape[1])
    past_len = cache_mla_ckv.shape[2]
    pos_s = past_len + jnp.arange(x_sample.shape[1])
    xp, xs = x_prompt, x_sample
    p_ckv, p_krope, p_k, p_v, p_mk, p_mv = [], [], [], [], [], []
    s_ckv, s_krope, s_k, s_v = [], [], [], []
    for l in range(DEPTH):
        w = (g_mix[l], w_in[l], b_gate[l], g_q_lat[l], w_uq[l], g_kv_lat[l], w_uk[l], w_uv[l],
             w_branch_a[l], w_branch_b[l], w_out[l], g_xattn[l], w_mq[l], w_mo[l], g_ffn[l],
             w_gate[l], w_up[l], w_down[l])
        mk, mv = _mem_kv(mem_prompt, g_mem[l], w_mk[l], w_mv[l])
        xp, (ckv, kr, k, v) = _layer(xp, pos_p, None, mk, mv, *w)
        p_ckv.append(ckv); p_krope.append(kr); p_k.append(k); p_v.append(v)
        p_mk.append(mk); p_mv.append(mv)
        past = (cache_mla_ckv[l], cache_mla_krope[l], cache_sb_k[l], cache_sb_v[l])
        xs, (ckv, kr, k, v) = _layer(xs, pos_s, past, cache_mem_k[l], cache_mem_v[l], *w)
        s_ckv.append(ckv); s_krope.append(kr); s_k.append(k); s_v.append(v)
    y_prompt = _rmsnorm(xp, g_final)
    y_sample = _rmsnorm(xs, g_final)
    new_p_ckv = jnp.stack(p_ckv)
    new_p_krope = jnp.stack(p_krope)
    new_p_sb_k = jnp.stack(p_k)
    new_p_sb_v = jnp.stack(p_v)
    new_p_mem_k = jnp.stack(p_mk)
    new_p_mem_v = jnp.stack(p_mv)
    new_s_ckv = jnp.stack(s_ckv)
    new_s_krope = jnp.stack(s_krope)
    new_s_sb_k = jnp.stack(s_k)
    new_s_sb_v = jnp.stack(s_v)
    return (y_prompt, y_sample, new_p_ckv, new_p_krope, new_p_sb_k, new_p_sb_v, new_p_mem_k,
            new_p_mem_v, new_s_ckv, new_s_krope, new_s_sb_k, new_s_sb_v)
```

```python
import functools
import math

import jax
import jax.numpy as jnp
from jax import lax
from jax.experimental import pallas as pl
from jax.experimental.pallas import tpu as pltpu

F32 = jnp.float32
BF16 = jnp.bfloat16

CHUNK = 64
EPS = 1e-6
ROPE_THETA = 10000.0
NEG_BIG = -1e30
MIB = 1024 * 1024
LANE = 128


def _cparams(sem, vmem_mib):
    return pltpu.CompilerParams(dimension_semantics=sem, vmem_limit_bytes=vmem_mib * MIB)


def _dot(a, b):
    return jnp.dot(a, b, preferred_element_type=F32)


def _dot_nt(a, b):
    return lax.dot_general(a, b, (((1,), (1,)), ((), ())), preferred_element_type=F32)


def _sigmoid(x):
    return 1.0 / (1.0 + jnp.exp(-x))


def _rms(x, g):
    return x * lax.rsqrt(jnp.mean(x * x, axis=-1, keepdims=True) + EPS) * g


def _tile(n, pref):
    if n <= pref:
        return n
    t = pref
    while n % t:
        t //= 2
    return t


def _norm_kernel(x_ref, g_ref, o_ref):
    o_ref[...] = _rms(x_ref[...], g_ref[...]).astype(o_ref.dtype)


def _rmsnorm(x, g, out_dtype, name):
    m, d = x.shape
    tm = _tile(m, 256)
    return pl.pallas_call(
        _norm_kernel,
        out_shape=jax.ShapeDtypeStruct((m, d), out_dtype),
        grid=(m // tm,),
        in_specs=[pl.BlockSpec((tm, d), lambda i: (i, 0)),
                  pl.BlockSpec((1, d), lambda i: (0, 0))],
        out_specs=pl.BlockSpec((tm, d), lambda i: (i, 0)),
        compiler_params=_cparams(("parallel",), 40),
        name=name,
    )(x, g.reshape(1, d))


def _fused_matmul(lhs, dots, extras, outs, epilogue, *, n, tm, tn, name, vmem_mib=48):
    m = lhs[0].shape[0]
    na, nd, ne = len(lhs), len(dots), len(extras)
    hpt = tn // LANE

    def kernel(*refs):
        a_refs, w_refs = refs[:na], refs[na:na + nd]
        e_refs = refs[na + nd:na + nd + ne]
        o_refs = refs[na + nd + ne:]
        accs = [_dot(a_refs[k][...], w[...]) for (k, _, _), w in zip(dots, w_refs)]
        vals = epilogue(accs, [e[...] for e in e_refs])
        for o_ref, v, (_, kind) in zip(o_refs, vals, outs):
            if kind == "tile":
                o_ref[...] = v.astype(o_ref.dtype)
            else:
                for hh in range(hpt):
                    o_ref[hh] = v[:, hh * LANE:(hh + 1) * LANE].astype(o_ref.dtype)

    in_specs, args = [], []
    for a in lhs:
        in_specs.append(pl.BlockSpec((tm, a.shape[1]), lambda i, j: (i, 0)))
        args.append(a)
    for _, w, off in dots:
        in_specs.append(pl.BlockSpec((w.shape[0], tn), lambda i, j, off=off: (0, j + off)))
        args.append(w)
    for e, kind, off in extras:
        if kind == "row":
            in_specs.append(pl.BlockSpec((1, tn), lambda i, j, off=off: (0, j + off)))
        else:
            in_specs.append(pl.BlockSpec((tm, tn), lambda i, j, off=off: (i, j + off)))
        args.append(e)
    out_shape, out_specs = [], []
    for dt, kind in outs:
        if kind == "tile":
            out_shape.append(jax.ShapeDtypeStruct((m, n), dt))
            out_specs.append(pl.BlockSpec((tm, tn), lambda i, j: (i, j)))
        else:
            out_shape.append(jax.ShapeDtypeStruct((n // LANE, m, LANE), dt))
            out_specs.append(pl.BlockSpec((hpt, tm, LANE), lambda i, j: (j, i, 0)))
    return pl.pallas_call(
        kernel,
        out_shape=out_shape,
        grid=(m // tm, n // tn),
        in_specs=in_specs,
        out_specs=out_specs,
        compiler_params=_cparams(("parallel", "arbitrary"), vmem_mib),
        name=name,
    )(*args)


def _lat_post_kernel(p_ref, gq_ref, gkv_ref, cs_ref, cqn_ref, ckv_ref, ckvb_ref, kr_ref, krb_ref,
                     *, q_lora, kv_lora, rope):
    p = p_ref[...]
    cqn_ref[...] = _rms(p[:, :q_lora], gq_ref[...]).astype(cqn_ref.dtype)
    ckv = _rms(p[:, q_lora:q_lora + kv_lora], gkv_ref[...])
    ckv_ref[...] = ckv
    ckvb_ref[...] = ckv.astype(ckvb_ref.dtype)
    t = p[:, q_lora + kv_lora:] * cs_ref[...]
    kr = (t + pltpu.roll(t, rope, axis=1))[:, :rope]
    kr_ref[...] = kr
    krb_ref[...] = kr.astype(krb_ref.dtype)


def _lat_post(p, g_q, g_kv, cs, *, q_lora, kv_lora, rope, name):
    m, w = p.shape
    tm = _tile(m, 512)
    row = lambda i: (i, 0)
    fix = lambda i: (0, 0)
    return pl.pallas_call(
        functools.partial(_lat_post_kernel, q_lora=q_lora, kv_lora=kv_lora, rope=rope),
        out_shape=[jax.ShapeDtypeStruct((m, q_lora), BF16),
                   jax.ShapeDtypeStruct((m, kv_lora), F32),
                   jax.ShapeDtypeStruct((m, kv_lora), BF16),
                   jax.ShapeDtypeStruct((m, rope), F32),
                   jax.ShapeDtypeStruct((m, rope), BF16)],
        grid=(m // tm,),
        in_specs=[pl.BlockSpec((tm, w), row), pl.BlockSpec((1, q_lora), fix),
                  pl.BlockSpec((1, kv_lora), fix), pl.BlockSpec((tm, 2 * rope), row)],
        out_specs=[pl.BlockSpec((tm, q_lora), row), pl.BlockSpec((tm, kv_lora), row),
                   pl.BlockSpec((tm, kv_lora), row), pl.BlockSpec((tm, rope), row),
                   pl.BlockSpec((tm, rope), row)],
        compiler_params=_cparams(("parallel",), 32),
        name=name,
    )(p, g_q.reshape(1, -1), g_kv.reshape(1, -1), cs)


def _mla_q_kernel(cqn_ref, wq_ref, wuk_ref, cs_ref, qlat_ref, qrope_ref, *, nope, rope, scale):
    qh = _dot(cqn_ref[...], wq_ref[...])
    qn = qh[:, :nope].astype(BF16)
    qlat_ref[0] = (_dot(qn, wuk_ref[...]) * scale).astype(qlat_ref.dtype)
    t = qh[:, nope:] * cs_ref[...]
    qr = (t + pltpu.roll(t, rope, axis=1))[:, :rope]
    qrope_ref[0] = (qr * scale).astype(qrope_ref.dtype)


def _mla_q(cqn, wq_cat, wuk_t, cs, *, heads, nope, rope, kv_lora, scale, name):
    m, q_lora = cqn.shape
    tm = _tile(m, 1024)
    hw = nope + 2 * rope
    return pl.pallas_call(
        functools.partial(_mla_q_kernel, nope=nope, rope=rope, scale=scale),
        out_shape=[jax.ShapeDtypeStruct((heads, m, kv_lora), BF16),
                   jax.ShapeDtypeStruct((heads, m, rope), BF16)],
        grid=(m // tm, heads),
        in_specs=[pl.BlockSpec((tm, q_lora), lambda i, h: (i, 0)),
                  pl.BlockSpec((q_lora, hw), lambda i, h: (0, h)),
                  pl.BlockSpec((nope, kv_lora), lambda i, h: (h, 0)),
                  pl.BlockSpec((tm, 2 * rope), lambda i, h: (i, 0))],
        out_specs=[pl.BlockSpec((1, tm, kv_lora), lambda i, h: (h, i, 0)),
                   pl.BlockSpec((1, tm, rope), lambda i, h: (h, i, 0))],
        compiler_params=_cparams(("parallel", "arbitrary"), 32),
        name=name,
    )(cqn, wq_cat, wuk_t, cs)


def _flash_step(q, qr, k, kr, mask, m_ref, l_ref, acc_ref, heads):
    s = _dot_nt(q, k) + _dot_nt(qr, kr)
    if mask is not None:
        r, tk = s.shape
        s = jnp.where(mask[None], s.reshape(heads, r // heads, tk), NEG_BIG).reshape(r, tk)
    m_prev = m_ref[...]
    m_new = jnp.maximum(m_prev, jnp.max(s, axis=-1, keepdims=True))
    p = jnp.exp(s - m_new)
    alpha = jnp.exp(m_prev - m_new)
    l_ref[...] = alpha * l_ref[...] + jnp.sum(p, axis=-1, keepdims=True)
    acc_ref[...] = alpha * acc_ref[...] + _dot(p.astype(BF16), k)
    m_ref[...] = m_new


def _flash_init(m_ref, l_ref, acc_ref):
    m_ref[...] = jnp.full(m_ref.shape, NEG_BIG, F32)
    l_ref[...] = jnp.zeros(l_ref.shape, F32)
    acc_ref[...] = jnp.zeros(acc_ref.shape, F32)


def _flash_finish(wuv_ref, o_ref, l_ref, acc_ref, heads, vh):
    tq = acc_ref.shape[0] // heads
    o = (acc_ref[...] * (1.0 / l_ref[...])).astype(BF16)
    for h in range(heads):
        o_ref[:, h * vh:(h + 1) * vh] = _dot(o[h * tq:(h + 1) * tq], wuv_ref[h]).astype(o_ref.dtype)


def _chunk_mask(q0, k0, tq, tk):
    qc = (q0 + lax.broadcasted_iota(jnp.int32, (tq, tk), 0)) // CHUNK
    kc = (k0 + lax.broadcasted_iota(jnp.int32, (tq, tk), 1)) // CHUNK
    return kc <= qc


def _mla_prompt_kernel(qlat_ref, qrope_ref, ckv_ref, kr_ref, wuv_ref, o_ref, m_ref, l_ref, acc_ref,
                       *, heads, tq, tk, vh):
    qi, ki = pl.program_id(1), pl.program_id(2)
    k_last = ((qi + 1) * tq - 1) // tk

    @pl.when(ki == 0)
    def _():
        _flash_init(m_ref, l_ref, acc_ref)

    @pl.when(ki <= k_last)
    def _():
        r = heads * tq
        q = qlat_ref[...].reshape(r, qlat_ref.shape[-1])
        qr = qrope_ref[...].reshape(r, qrope_ref.shape[-1])
        mask = _chunk_mask(qi * tq, ki * tk, tq, tk)
        _flash_step(q, qr, ckv_ref[...], kr_ref[...], mask, m_ref, l_ref, acc_ref, heads)

    @pl.when(ki == k_last)
    def _():
        _flash_finish(wuv_ref, o_ref, l_ref, acc_ref, heads, vh)


def _mla_prompt(qlat, qrope, ckv, krope, wuv, *, batch, seq, name):
    heads, _, c = qlat.shape
    rope = qrope.shape[-1]
    vh = wuv.shape[-1]
    tq = _tile(seq, 128)
    tk = _tile(seq, 512)
    nq, nk = seq // tq, seq // tk

    def kmap(b, qi, ki):
        return (b * nk + jnp.minimum(ki, ((qi + 1) * tq - 1) // tk), 0)

    return pl.pallas_call(
        functools.partial(_mla_prompt_kernel, heads=heads, tq=tq, tk=tk, vh=vh),
        out_shape=jax.ShapeDtypeStruct((batch * seq, heads * vh), BF16),
        grid=(batch, nq, nk),
        in_specs=[pl.BlockSpec((heads, tq, c), lambda b, qi, ki: (0, b * nq + qi, 0)),
                  pl.BlockSpec((heads, tq, rope), lambda b, qi, ki: (0, b * nq + qi, 0)),
                  pl.BlockSpec((tk, c), kmap),
                  pl.BlockSpec((tk, rope), kmap),
                  pl.BlockSpec((heads, c, vh), lambda b, qi, ki: (0, 0, 0))],
        out_specs=pl.BlockSpec((tq, heads * vh), lambda b, qi, ki: (b * nq + qi, 0)),
        scratch_shapes=[pltpu.VMEM((heads * tq, 1), F32), pltpu.VMEM((heads * tq, 1), F32),
                        pltpu.VMEM((heads * tq, c), F32)],
        compiler_params=_cparams(("parallel", "parallel", "arbitrary"), 48),
        name=name,
    )(qlat, qrope, ckv, krope, wuv)


def _mla_sample_kernel(qlat_ref, qrope_ref, cckv_ref, ckr_ref, nckv_ref, nkr_ref, wuv_ref, o_ref,
                       m_ref, l_ref, acc_ref, *, heads, tq, nkc, past, vh):
    ki = pl.program_id(1)
    r = heads * tq

    @pl.when(ki == 0)
    def _():
        _flash_init(m_ref, l_ref, acc_ref)

    def step(k, kr, mask):
        q = qlat_ref[...].reshape(r, qlat_ref.shape[-1])
        qr = qrope_ref[...].reshape(r, qrope_ref.shape[-1])
        _flash_step(q, qr, k, kr, mask, m_ref, l_ref, acc_ref, heads)

    @pl.when(ki < nkc)
    def _():
        step(cckv_ref[0].astype(BF16), ckr_ref[0].astype(BF16), None)

    @pl.when(ki == nkc)
    def _():
        step(nckv_ref[...], nkr_ref[...], _chunk_mask(past, past, tq, tq))
        _flash_finish(wuv_ref, o_ref, l_ref, acc_ref, heads, vh)


def _mla_sample(qlat, qrope, cache_ckv, cache_kr, ckv, krope, wuv, *, name):
    heads, _, c = qlat.shape
    rope = qrope.shape[-1]
    vh = wuv.shape[-1]
    batch, past, _ = cache_ckv.shape
    tq = ckv.shape[0] // batch
    tk = _tile(past, 512)
    nkc = past // tk
    cmap = lambda b, ki: (b, jnp.minimum(ki, nkc - 1), 0)
    return pl.pallas_call(
        functools.partial(_mla_sample_kernel, heads=heads, tq=tq, nkc=nkc, past=past, vh=vh),
        out_shape=jax.ShapeDtypeStruct((batch * tq, heads * vh), BF16),
        grid=(batch, nkc + 1),
        in_specs=[pl.BlockSpec((heads, tq, c), lambda b, ki: (0, b, 0)),
                  pl.BlockSpec((heads, tq, rope), lambda b, ki: (0, b, 0)),
                  pl.BlockSpec((1, tk, c), cmap),
                  pl.BlockSpec((1, tk, rope), cmap),
                  pl.BlockSpec((tq, c), lambda b, ki: (b, 0)),
                  pl.BlockSpec((tq, rope), lambda b, ki: (b, 0)),
                  pl.BlockSpec((heads, c, vh), lambda b, ki: (0, 0, 0))],
        out_specs=pl.BlockSpec((tq, heads * vh), lambda b, ki: (b, 0)),
        scratch_shapes=[pltpu.VMEM((heads * tq, 1), F32), pltpu.VMEM((heads * tq, 1), F32),
                        pltpu.VMEM((heads * tq, c), F32)],
        compiler_params=_cparams(("parallel", "arbitrary"), 48),
        name=name,
    )(qlat, qrope, cache_ckv, cache_kr, ckv, krope, wuv)


def _sb_head_step(q, k, v, tri, valid, run, acc):
    z = _dot_nt(q, k)
    sp = jnp.log(1.0 + jnp.exp(-jnp.abs(z)))
    lk = -(jnp.maximum(z, 0.0) + sp)
    if valid is not None:
        lk = jnp.where(valid, lk, 0.0)
    hi = lk.astype(BF16)
    lo = (lk - hi.astype(F32)).astype(BF16)
    suffix = _dot(hi, tri) + _dot(lo, tri)
    logw = (jnp.minimum(z, 0.0) - sp) + suffix + run
    w = jnp.exp(logw)
    if valid is not None:
        w = jnp.where(valid, w, 0.0)
    acc = acc + _dot(w.astype(BF16), v)
    run = run + suffix[:, :1] + lk[:, :1]
    return run, acc


def _sb_prompt_kernel(q_ref, k_ref, v_ref, tri_ref, o_ref, run_ref, acc_ref, *, heads, tq, hd):
    qi, kk = pl.program_id(1), pl.program_id(2)

    @pl.when(kk == 0)
    def _():
        run_ref[...] = jnp.zeros(run_ref.shape, F32)
        acc_ref[...] = jnp.zeros(acc_ref.shape, F32)

    def block(valid):
        tri = tri_ref[...]
        for h in range(heads):
            run, acc = _sb_head_step(q_ref[h], k_ref[h], v_ref[h], tri, valid,
                                     run_ref[h], acc_ref[h])
            run_ref[h] = run
            acc_ref[h] = acc

    @pl.when(kk == 0)
    def _():
        row = lax.broadcasted_iota(jnp.int32, (tq, tq), 0)
        col = lax.broadcasted_iota(jnp.int32, (tq, tq), 1)
        block(col < row)

    @pl.when((kk > 0) & (kk <= qi))
    def _():
        block(None)

    @pl.when(kk == qi)
    def _():
        for h in range(heads):
            o_ref[:, h * hd:(h + 1) * hd] = acc_ref[h].astype(o_ref.dtype)


def _sb_prompt(q, k, v, tri, *, batch, seq, name):
    heads, _, hd = q.shape
    tq = tri.shape[0]
    nq = seq // tq
    kmap = lambda b, qi, kk: (0, b * nq + jnp.maximum(qi - kk, 0), 0)
    return pl.pallas_call(
        functools.partial(_sb_prompt_kernel, heads=heads, tq=tq, hd=hd),
        out_shape=jax.ShapeDtypeStruct((batch * seq, heads * hd), BF16),
        grid=(batch, nq, nq),
        in_specs=[pl.BlockSpec((heads, tq, hd), lambda b, qi, kk: (0, b * nq + qi, 0)),
                  pl.BlockSpec((heads, tq, hd), kmap),
                  pl.BlockSpec((heads, tq, hd), kmap),
                  pl.BlockSpec((tq, tq), lambda b, qi, kk: (0, 0))],
        out_specs=pl.BlockSpec((tq, heads * hd), lambda b, qi, kk: (b * nq + qi, 0)),
        scratch_shapes=[pltpu.VMEM((heads, tq, 1), F32), pltpu.VMEM((heads, tq, hd), F32)],
        compiler_params=_cparams(("parallel", "parallel", "arbitrary"), 40),
        name=name,
    )(q, k, v, tri)


def _sb_sample_kernel(q_ref, nk_ref, nv_ref, ck_ref, cv_ref, tri_ref, o_ref, run_ref, acc_ref,
                      *, heads, tq, tk, hd):
    kk = pl.program_id(1)

    @pl.when(kk == 0)
    def _():
        row = lax.broadcasted_iota(jnp.int32, (tq, tq), 0)
        col = lax.broadcasted_iota(jnp.int32, (tq, tq), 1)
        valid = col < row
        tri = tri_ref[:tq, :tq]
        for h in range(heads):
            run, acc = _sb_head_step(q_ref[h], nk_ref[h], nv_ref[h], tri, valid,
                                     jnp.zeros((tq, 1), F32), jnp.zeros((tq, hd), F32))
            run_ref[h] = run
            acc_ref[h] = acc

    @pl.when(kk > 0)
    def _():
        tri = tri_ref[...]
        for h in range(heads):
            k = ck_ref[0, :, h * hd:(h + 1) * hd].astype(BF16)
            v = cv_ref[0, :, h * hd:(h + 1) * hd].astype(BF16)
            run, acc = _sb_head_step(q_ref[h], k, v, tri, None, run_ref[h], acc_ref[h])
            run_ref[h] = run
            acc_ref[h] = acc

    @pl.when(kk == pl.num_programs(1) - 1)
    def _():
        for h in range(heads):
            o_ref[:, h * hd:(h + 1) * hd] = acc_ref[h].astype(o_ref.dtype)


def _sb_sample(q, k_new, v_new, cache_k, cache_v, tri, *, name):
    heads, rows, hd = q.shape
    batch, past, width = cache_k.shape
    tq = rows // batch
    tk = tri.shape[0]
    nkc = past // tk
    cmap = lambda b, kk: (b, jnp.clip(nkc - kk, 0, nkc - 1), 0)
    new = lambda b, kk: (0, b, 0)
    return pl.pallas_call(
        functools.partial(_sb_sample_kernel, heads=heads, tq=tq, tk=tk, hd=hd),
        out_shape=jax.ShapeDtypeStruct((rows, heads * hd), BF16),
        grid=(batch, nkc + 1),
        in_specs=[pl.BlockSpec((heads, tq, hd), new), pl.BlockSpec((heads, tq, hd), new),
                  pl.BlockSpec((heads, tq, hd), new),
                  pl.BlockSpec((1, tk, width), cmap), pl.BlockSpec((1, tk, width), cmap),
                  pl.BlockSpec((tk, tk), lambda b, kk: (0, 0))],
        out_specs=pl.BlockSpec((tq, heads * hd), lambda b, kk: (b, 0)),
        scratch_shapes=[pltpu.VMEM((heads, tq, 1), F32), pltpu.VMEM((heads, tq, hd), F32)],
        compiler_params=_cparams(("parallel", "arbitrary"), 40),
        name=name,
    )(q, k_new, v_new, cache_k, cache_v, tri)


def _mem_attn_kernel(x_ref, g_ref, wq_ref, mk_ref, mv_ref, wo_ref, gn_ref, xo_ref, hn_ref, o_scr,
                     *, nsub, sub, heads, hd, scale):
    x = x_ref[...]
    mq = (_dot(_rms(x, g_ref[...]).astype(BF16), wq_ref[...]) * scale).astype(BF16)
    for s in range(nsub):
        for h in range(heads):
            q = mq[s * sub:(s + 1) * sub, h * hd:(h + 1) * hd]
            k = mk_ref[s, :, h * hd:(h + 1) * hd]
            v = mv_ref[s, :, h * hd:(h + 1) * hd]
            sc = _dot_nt(q, k)
            p = jnp.exp(sc - jnp.max(sc, axis=-1, keepdims=True))
            p = p * (1.0 / jnp.sum(p, axis=-1, keepdims=True))
            o_scr[s * sub:(s + 1) * sub, h * hd:(h + 1) * hd] = _dot(p.astype(BF16), v).astype(BF16)
    xn = x + _dot(o_scr[...], wo_ref[...])
    xo_ref[...] = xn
    hn_ref[...] = _rms(xn, gn_ref[...]).astype(hn_ref.dtype)


def _mem_attn(x, g, w_mq, mem_k, mem_v, w_mo, g_next, *, sub, heads, name):
    m, d = x.shape
    nb, n_mem, width = mem_k.shape
    hd = width // heads
    tm = _tile(m, 256)
    if sub >= tm:
        nsub, rows = 1, tm
        per = sub // tm
        mmap = lambda i: (i // per, 0, 0)
    else:
        nsub, rows = tm // sub, sub
        mmap = lambda i: (i, 0, 0)
    row = lambda i: (i, 0)
    fix = lambda i: (0, 0)
    return pl.pallas_call(
        functools.partial(_mem_attn_kernel, nsub=nsub, sub=rows, heads=heads, hd=hd,
                          scale=hd ** -0.5),
        out_shape=[jax.ShapeDtypeStruct((m, d), F32), jax.ShapeDtypeStruct((m, d), BF16)],
        grid=(m // tm,),
        in_specs=[pl.BlockSpec((tm, d), row), pl.BlockSpec((1, d), fix),
                  pl.BlockSpec((d, width), fix),
                  pl.BlockSpec((nsub, n_mem, width), mmap),
                  pl.BlockSpec((nsub, n_mem, width), mmap),
                  pl.BlockSpec((width, d), fix), pl.BlockSpec((1, d), fix)],
        out_specs=[pl.BlockSpec((tm, d), row), pl.BlockSpec((tm, d), row)],
        scratch_shapes=[pltpu.VMEM((tm, width), BF16)],
        compiler_params=_cparams(("parallel",), 48),
        name=name,
    )(x, g.reshape(1, d), w_mq, mem_k, mem_v, w_mo, g_next.reshape(1, d))


def _rotate_half_cols(w):
    half = w.shape[-1] // 2
    return jnp.concatenate([-w[..., half:], w[..., :half]], axis=-1)


def _rope_table(pos, half):
    inv = ROPE_THETA ** (-jnp.arange(half, dtype=F32) / half)
    ang = pos.astype(F32)[:, None] * inv[None, :]
    c, s = jnp.cos(ang), jnp.sin(ang)
    return jnp.concatenate([c, c, s, s], axis=-1)


def _prepare_weights(w_in, w_uq, w_uk, w_uv, w_branch_a, w_branch_b, w_out, w_mq, w_mk, w_mv, w_mo,
                     w_gate, w_up, w_down, dims):
    q_lora, kv_lora, rope, heads, nope = dims
    o_kr = q_lora + kv_lora
    o_sb = o_kr + rope
    w_lat = jnp.concatenate([w_in[:, :o_sb], _rotate_half_cols(w_in[:, o_kr:o_sb])], axis=1)
    wq_cat = jnp.concatenate([w_uq, _rotate_half_cols(w_uq[..., nope:])], axis=-1)
    return dict(
        w_lat=w_lat.astype(BF16),
        w_rest=w_in[:, o_sb:].astype(BF16),
        wq_cat=wq_cat.reshape(q_lora, -1).astype(BF16),
        wuk_t=jnp.transpose(w_uk, (1, 2, 0)).reshape(heads * nope, kv_lora).astype(BF16),
        wuv=jnp.transpose(w_uv, (1, 0, 2)).astype(BF16),
        w_ba=w_branch_a.astype(BF16), w_bb=w_branch_b.astype(BF16), w_out=w_out.astype(BF16),
        w_mq=w_mq.astype(BF16), w_mk=w_mk.astype(BF16), w_mv=w_mv.astype(BF16),
        w_mo=w_mo.astype(BF16),
        w_gate=w_gate.astype(BF16), w_up=w_up.astype(BF16), w_down=w_down.astype(BF16),
    )


def _layer(x, pos, past, mem_k, mem_v, w, gains, b_gate, dims, *, batch, tag):
    g_mix, g_q_lat, g_kv_lat, g_xattn, g_ffn = gains
    q_lora, kv_lora, rope, heads, nope = dims
    m, d = x.shape
    t = m // batch
    sb_width = (w["w_rest"].shape[1] - 2 * d) // 3
    sb_heads = sb_width // LANE
    mla_scale = (nope + rope) ** -0.5
    sb_scale = LANE ** -0.5
    tm = _tile(m, 1024)
    tn = 512

    h = _rmsnorm(x, g_mix, BF16, f"{tag}_norm_mix")
    lat_w = w["w_lat"].shape[1]
    (p_lat,) = _fused_matmul(
        [h], [(0, w["w_lat"], 0)], [], [(F32, "tile")], lambda accs, ex: (accs[0],),
        n=lat_w, tm=_tile(m, 256), tn=lat_w, name=f"{tag}_proj_lat")
    cs = jnp.tile(_rope_table(pos, rope // 2), (batch, 1))
    cqn, ckv, ckv_b, krope, krope_b = _lat_post(p_lat, g_q_lat, g_kv_lat, cs, q_lora=q_lora,
                                                kv_lora=kv_lora, rope=rope, name=f"{tag}_lat_post")
    nsb = sb_width // tn
    (sbq,) = _fused_matmul(
        [h], [(0, w["w_rest"], 0)], [], [(BF16, "heads")],
        lambda accs, ex: (accs[0] * sb_scale,), n=sb_width, tm=tm, tn=tn, name=f"{tag}_proj_sbq")
    sbk, sbk_b = _fused_matmul(
        [h], [(0, w["w_rest"], nsb)], [], [(F32, "tile"), (BF16, "heads")],
        lambda accs, ex: (accs[0], accs[0]), n=sb_width, tm=tm, tn=tn, name=f"{tag}_proj_sbk")
    sbv, sbv_b = _fused_matmul(
        [h], [(0, w["w_rest"], 2 * nsb)], [], [(F32, "tile"), (BF16, "heads")],
        lambda accs, ex: (accs[0], accs[0]), n=sb_width, tm=tm, tn=tn, name=f"{tag}_proj_sbv")
    (gates,) = _fused_matmul(
        [h], [(0, w["w_rest"], 3 * nsb)], [(b_gate.reshape(1, -1), "row", 0)], [(BF16, "tile")],
        lambda accs, ex: (_sigmoid(accs[0] + ex[0]),), n=2 * d, tm=tm, tn=tn,
        name=f"{tag}_proj_gates")

    qlat, qrope = _mla_q(cqn, w["wq_cat"], w["wuk_t"], cs, heads=heads, nope=nope, rope=rope,
                         kv_lora=kv_lora, scale=mla_scale, name=f"{tag}_mla_q")
    tri_n = 256 if t % 256 == 0 else t
    tri = (jnp.arange(tri_n)[:, None] > jnp.arange(tri_n)[None, :]).astype(BF16)
    if past is None:
        o_a = _mla_prompt(qlat, qrope, ckv_b, krope_b, w["wuv"], batch=batch, seq=t,
                          name=f"{tag}_mla_attn")
        o_b = _sb_prompt(sbq, sbk_b, sbv_b, tri, batch=batch, seq=t, name=f"{tag}_sb_attn")
    else:
        c_ckv, c_kr, c_k, c_v = past
        o_a = _mla_sample(qlat, qrope, c_ckv, c_kr, ckv_b, krope_b, w["wuv"], name=f"{tag}_mla_attn")
        tri = (jnp.arange(256)[:, None] > jnp.arange(256)[None, :]).astype(BF16)
        o_b = _sb_sample(sbq, sbk_b, sbv_b, c_k.reshape(c_k.shape[0], c_k.shape[1], -1),
                         c_v.reshape(c_v.shape[0], c_v.shape[1], -1), tri, name=f"{tag}_sb_attn")

    ng = d // tn
    (merged,) = _fused_matmul(
        [o_a, o_b], [(0, w["w_ba"], 0), (1, w["w_bb"], 0)],
        [(gates, "tile", 0), (gates, "tile", ng)], [(BF16, "tile")],
        lambda accs, ex: (ex[0].astype(F32) * accs[0] + ex[1].astype(F32) * accs[1],),
        n=d, tm=tm, tn=tn, name=f"{tag}_merge")
    (x,) = _fused_matmul(
        [merged], [(0, w["w_out"], 0)], [(x, "tile", 0)], [(F32, "tile")],
        lambda accs, ex: (ex[0] + accs[0],), n=d, tm=tm, tn=tn, name=f"{tag}_out_proj")

    mem_heads = mem_k.shape[2]
    mk = mem_k.reshape(mem_k.shape[0], mem_k.shape[1], -1).astype(BF16)
    mv = mem_v.reshape(mem_v.shape[0], mem_v.shape[1], -1).astype(BF16)
    x, hf = _mem_attn(x, g_xattn, w["w_mq"], mk, mv, w["w_mo"], g_ffn, sub=t, heads=mem_heads,
                      name=f"{tag}_mem_attn")

    d_ff = w["w_gate"].shape[1]
    tn_ff = _tile(d_ff, 256) if d_ff % 512 else 512
    (act,) = _fused_matmul(
        [hf], [(0, w["w_gate"], 0), (0, w["w_up"], 0)], [], [(BF16, "tile")],
        lambda accs, ex: (accs[0] * _sigmoid(accs[0]) * accs[1],), n=d_ff, tm=tm, tn=tn_ff,
        name=f"{tag}_ffn_up")
    (x,) = _fused_matmul(
        [act], [(0, w["w_down"], 0)], [(x, "tile", 0)], [(F32, "tile")],
        lambda accs, ex: (ex[0] + accs[0],), n=d, tm=_tile(m, 512), tn=256, name=f"{tag}_ffn_down")
    return x, (ckv, krope, sbk, sbv)


def kernel(x_prompt, x_sample, cache_mla_ckv, cache_mla_krope, cache_sb_k, cache_sb_v, cache_mem_k, cache_mem_v, mem_prompt, g_mix, w_in, b_gate, g_q_lat, w_uq, g_kv_lat, w_uk, w_uv, w_branch_a, w_branch_b, w_out, g_xattn, g_mem, w_mq, w_mk, w_mv, w_mo, g_ffn, w_gate, w_up, w_down, g_final):
    depth = w_in.shape[0]
    bp, seq, d = x_prompt.shape
    bs, dec, _ = x_sample.shape
    past_len = cache_mla_ckv.shape[2]
    q_lora, heads, qk = w_uq.shape[1:]
    kv_lora, _, nope = w_uk.shape[1:]
    rope = qk - nope
    dims = (q_lora, kv_lora, rope, heads, nope)
    sb_heads, sb_hd = cache_sb_k.shape[3:]
    n_mem, mem_heads, mem_hd = cache_mem_k.shape[2:]
    pos_p = jnp.arange(seq)
    pos_s = past_len + jnp.arange(dec)

    xp = x_prompt.reshape(bp * seq, d)
    xs = x_sample.reshape(bs * dec, d)
    outs = [[] for _ in range(10)]
    for l in range(depth):
        w = _prepare_weights(w_in[l], w_uq[l], w_uk[l], w_uv[l], w_branch_a[l], w_branch_b[l],
                             w_out[l], w_mq[l], w_mk[l], w_mv[l], w_mo[l], w_gate[l], w_up[l],
                             w_down[l], dims)
        gains = (g_mix[l], g_q_lat[l], g_kv_lat[l], g_xattn[l], g_ffn[l])
        mn = _rmsnorm(mem_prompt.reshape(bp * n_mem, d), g_mem[l], BF16, f"l{l}_norm_mem")
        mem_w = w["w_mk"].shape[1]
        mk, mv = _fused_matmul(
            [mn], [(0, w["w_mk"], 0), (0, w["w_mv"], 0)], [], [(F32, "tile"), (F32, "tile")],
            lambda accs, ex: (accs[0], accs[1]), n=mem_w, tm=_tile(bp * n_mem, 512),
            tn=_tile(mem_w, 512), name=f"l{l}_mem_kv")
        mk = mk.reshape(bp, n_mem, mem_heads, mem_hd)
        mv = mv.reshape(bp, n_mem, mem_heads, mem_hd)
        xp, (ckv, kr, k, v) = _layer(xp, pos_p, None, mk, mv, w, gains, b_gate[l], dims,
                                     batch=bp, tag=f"l{l}p")
        for lst, val in zip(outs[:6], (ckv.reshape(bp, seq, -1), kr.reshape(bp, seq, -1),
                                       k.reshape(bp, seq, sb_heads, sb_hd),
                                       v.reshape(bp, seq, sb_heads, sb_hd), mk, mv)):
            lst.append(val)
        past = (cache_mla_ckv[l], cache_mla_krope[l], cache_sb_k[l], cache_sb_v[l])
        xs, (ckv, kr, k, v) = _layer(xs, pos_s, past, cache_mem_k[l], cache_mem_v[l], w, gains,
                                     b_gate[l], dims, batch=bs, tag=f"l{l}s")
        for lst, val in zip(outs[6:], (ckv.reshape(bs, dec, -1), kr.reshape(bs, dec, -1),
                                       k.reshape(bs, dec, sb_heads, sb_hd),
                                       v.reshape(bs, dec, sb_heads, sb_hd))):
            lst.append(val)
    y_prompt = _rmsnorm(xp, g_final, F32, "final_norm_p").reshape(bp, seq, d)
    y_sample = _rmsnorm(xs, g_final, F32, "final_norm_s").reshape(bs, dec, d)
    return (y_prompt, y_sample) + tuple(jnp.stack(o) for o in outs)
```

```python
import functools
import math

import jax
import jax.numpy as jnp
from jax import lax
from jax.experimental import pallas as pl
from jax.experimental.pallas import tpu as pltpu

F32 = jnp.float32
BF16 = jnp.bfloat16

CHUNK = 64
EPS = 1e-6
ROPE_THETA = 10000.0
NEG_BIG = -1e30
LOG2E = math.log2(math.e)
MIB = 1024 * 1024
LANE = 128
_SB_GROUP = 4
_MLA_GROUP = 4
_MLA_TQ = 256
_MLA_TK = 1024


def _cparams(sem, vmem_mib):
    return pltpu.CompilerParams(dimension_semantics=sem, vmem_limit_bytes=vmem_mib * MIB)


def _dot(a, b):
    return jnp.dot(a, b, preferred_element_type=F32)


def _dot_nt(a, b):
    return lax.dot_general(a, b, (((1,), (1,)), ((), ())), preferred_element_type=F32)


def _sigmoid(x):
    return 1.0 / (1.0 + jnp.exp(-x))


def _rms(x, g):
    return x * lax.rsqrt(jnp.mean(x * x, axis=-1, keepdims=True) + EPS) * g


def _tile(n, pref):
    if n <= pref:
        return n
    t = pref
    while n % t:
        t //= 2
    return t


def _norm_kernel(x_ref, g_ref, o_ref):
    o_ref[...] = _rms(x_ref[...], g_ref[...]).astype(o_ref.dtype)


def _rmsnorm(x, g, out_dtype, name):
    m, d = x.shape
    tm = _tile(m, 256)
    return pl.pallas_call(
        _norm_kernel,
        out_shape=jax.ShapeDtypeStruct((m, d), out_dtype),
        grid=(m // tm,),
        in_specs=[pl.BlockSpec((tm, d), lambda i: (i, 0)),
                  pl.BlockSpec((1, d), lambda i: (0, 0))],
        out_specs=pl.BlockSpec((tm, d), lambda i: (i, 0)),
        compiler_params=_cparams(("parallel",), 40),
        name=name,
    )(x, g.reshape(1, d))


def _fused_matmul(lhs, dots, extras, outs, epilogue, *, n, tm, tn, name, vmem_mib=48):
    m = lhs[0].shape[0]
    na, nd, ne = len(lhs), len(dots), len(extras)
    hpt = tn // LANE

    def kernel(*refs):
        a_refs, w_refs = refs[:na], refs[na:na + nd]
        e_refs = refs[na + nd:na + nd + ne]
        o_refs = refs[na + nd + ne:]
        accs = [_dot(a_refs[k][...], w[...]) for (k, _, _), w in zip(dots, w_refs)]
        vals = epilogue(accs, [e[...] for e in e_refs])
        for o_ref, v, (_, kind) in zip(o_refs, vals, outs):
            if kind == "tile":
                o_ref[...] = v.astype(o_ref.dtype)
            else:
                for hh in range(hpt):
                    o_ref[hh] = v[:, hh * LANE:(hh + 1) * LANE].astype(o_ref.dtype)

    in_specs, args = [], []
    for a in lhs:
        in_specs.append(pl.BlockSpec((tm, a.shape[1]), lambda i, j: (i, 0)))
        args.append(a)
    for _, w, off in dots:
        in_specs.append(pl.BlockSpec((w.shape[0], tn), lambda i, j, off=off: (0, j + off)))
        args.append(w)
    for e, kind, off in extras:
        if kind == "row":
            in_specs.append(pl.BlockSpec((1, tn), lambda i, j, off=off: (0, j + off)))
        else:
            in_specs.append(pl.BlockSpec((tm, tn), lambda i, j, off=off: (i, j + off)))
        args.append(e)
    out_shape, out_specs = [], []
    for dt, kind in outs:
        if kind == "tile":
            out_shape.append(jax.ShapeDtypeStruct((m, n), dt))
            out_specs.append(pl.BlockSpec((tm, tn), lambda i, j: (i, j)))
        else:
            out_shape.append(jax.ShapeDtypeStruct((n // LANE, m, LANE), dt))
            out_specs.append(pl.BlockSpec((hpt, tm, LANE), lambda i, j: (j, i, 0)))
    return pl.pallas_call(
        kernel,
        out_shape=out_shape,
        grid=(m // tm, n // tn),
        in_specs=in_specs,
        out_specs=out_specs,
        compiler_params=_cparams(("parallel", "arbitrary"), vmem_mib),
        name=name,
    )(*args)


def _lat_post_kernel(p_ref, gq_ref, gkv_ref, cs_ref, cqn_ref, ckv_ref, ckvb_ref, kr_ref, krb_ref,
                     *, q_lora, kv_lora, rope):
    p = p_ref[...]
    cqn_ref[...] = _rms(p[:, :q_lora], gq_ref[...]).astype(cqn_ref.dtype)
    ckv = _rms(p[:, q_lora:q_lora + kv_lora], gkv_ref[...])
    ckv_ref[...] = ckv
    ckvb_ref[...] = ckv.astype(ckvb_ref.dtype)
    t = p[:, q_lora + kv_lora:] * cs_ref[...]
    kr = (t + pltpu.roll(t, rope, axis=1))[:, :rope]
    kr_ref[...] = kr
    krb_ref[...] = kr.astype(krb_ref.dtype)


def _lat_post(p, g_q, g_kv, cs, *, q_lora, kv_lora, rope, name):
    m, w = p.shape
    tm = _tile(m, 512)
    row = lambda i: (i, 0)
    fix = lambda i: (0, 0)
    return pl.pallas_call(
        functools.partial(_lat_post_kernel, q_lora=q_lora, kv_lora=kv_lora, rope=rope),
        out_shape=[jax.ShapeDtypeStruct((m, q_lora), BF16),
                   jax.ShapeDtypeStruct((m, kv_lora), F32),
                   jax.ShapeDtypeStruct((m, kv_lora), BF16),
                   jax.ShapeDtypeStruct((m, rope), F32),
                   jax.ShapeDtypeStruct((m, rope), BF16)],
        grid=(m // tm,),
        in_specs=[pl.BlockSpec((tm, w), row), pl.BlockSpec((1, q_lora), fix),
                  pl.BlockSpec((1, kv_lora), fix), pl.BlockSpec((tm, 2 * rope), row)],
        out_specs=[pl.BlockSpec((tm, q_lora), row), pl.BlockSpec((tm, kv_lora), row),
                   pl.BlockSpec((tm, kv_lora), row), pl.BlockSpec((tm, rope), row),
                   pl.BlockSpec((tm, rope), row)],
        compiler_params=_cparams(("parallel",), 32),
        name=name,
    )(p, g_q.reshape(1, -1), g_kv.reshape(1, -1), cs)


def _mla_q_kernel(cqn_ref, wq_ref, wuk_ref, cs_ref, qlat_ref, qrope_ref, *, nope, rope, scale):
    qh = _dot(cqn_ref[...], wq_ref[...])
    qn = qh[:, :nope].astype(BF16)
    qlat_ref[0] = (_dot(qn, wuk_ref[...]) * scale).astype(qlat_ref.dtype)
    t = qh[:, nope:] * cs_ref[...]
    qr = (t + pltpu.roll(t, rope, axis=1))[:, :rope]
    qrope_ref[0] = (qr * scale).astype(qrope_ref.dtype)


def _mla_q(cqn, wq_cat, wuk_t, cs, *, heads, nope, rope, kv_lora, scale, name):
    m, q_lora = cqn.shape
    tm = _tile(m, 1024)
    hw = nope + 2 * rope
    return pl.pallas_call(
        functools.partial(_mla_q_kernel, nope=nope, rope=rope, scale=scale),
        out_shape=[jax.ShapeDtypeStruct((heads, m, kv_lora), BF16),
                   jax.ShapeDtypeStruct((heads, m, rope), BF16)],
        grid=(m // tm, heads),
        in_specs=[pl.BlockSpec((tm, q_lora), lambda i, h: (i, 0)),
                  pl.BlockSpec((q_lora, hw), lambda i, h: (0, h)),
                  pl.BlockSpec((nope, kv_lora), lambda i, h: (h, 0)),
                  pl.BlockSpec((tm, 2 * rope), lambda i, h: (i, 0))],
        out_specs=[pl.BlockSpec((1, tm, kv_lora), lambda i, h: (h, i, 0)),
                   pl.BlockSpec((1, tm, rope), lambda i, h: (h, i, 0))],
        compiler_params=_cparams(("parallel", "arbitrary"), 32),
        name=name,
    )(cqn, wq_cat, wuk_t, cs)


def _flash_step(qlat_ref, qrope_ref, k, kr, mask, m_ref, l_ref, acc_ref, *, heads, group, tq):
    rows = group * tq
    tk = k.shape[0]
    for g0 in range(0, heads, group):
        q = qlat_ref[g0:g0 + group].reshape(rows, qlat_ref.shape[-1])
        qr = qrope_ref[g0:g0 + group].reshape(rows, qrope_ref.shape[-1])
        s = _dot_nt(q, k) + _dot_nt(qr, kr)
        if mask is not None:
            s = jnp.where(mask[None], s.reshape(group, tq, tk), NEG_BIG).reshape(rows, tk)
        sl = slice(g0 * tq, g0 * tq + rows)
        m_prev = m_ref[sl]
        m_new = jnp.maximum(m_prev, jnp.max(s, axis=-1, keepdims=True))
        p = jnp.exp2(s - m_new)
        alpha = jnp.exp2(m_prev - m_new)
        l_ref[sl] = alpha * l_ref[sl] + jnp.sum(p, axis=-1, keepdims=True)
        acc_ref[sl] = alpha * acc_ref[sl] + _dot(p.astype(BF16), k)
        m_ref[sl] = m_new


def _flash_init(m_ref, l_ref, acc_ref):
    m_ref[...] = jnp.full(m_ref.shape, NEG_BIG, F32)
    l_ref[...] = jnp.zeros(l_ref.shape, F32)
    acc_ref[...] = jnp.zeros(acc_ref.shape, F32)


def _flash_finish(wuv_ref, o_ref, l_ref, acc_ref, heads, vh):
    tq = acc_ref.shape[0] // heads
    o = (acc_ref[...] * (1.0 / l_ref[...])).astype(BF16)
    for h in range(heads):
        o_ref[:, h * vh:(h + 1) * vh] = _dot(o[h * tq:(h + 1) * tq], wuv_ref[h]).astype(o_ref.dtype)


def _chunk_mask(q0, k0, tq, tk):
    qc = (q0 + lax.broadcasted_iota(jnp.int32, (tq, tk), 0)) // CHUNK
    kc = (k0 + lax.broadcasted_iota(jnp.int32, (tq, tk), 1)) // CHUNK
    return kc <= qc


def _mla_prompt_kernel(qlat_ref, qrope_ref, ckv_ref, kr_ref, wuv_ref, o_ref, m_ref, l_ref, acc_ref,
                       *, heads, group, tq, tk, vh):
    qi, ki = pl.program_id(1), pl.program_id(2)
    k_last = ((qi + 1) * tq - 1) // tk
    partial = (ki + 1) * tk > qi * tq + CHUNK
    step = functools.partial(_flash_step, qlat_ref, qrope_ref, m_ref=m_ref, l_ref=l_ref,
                             acc_ref=acc_ref, heads=heads, group=group, tq=tq)

    @pl.when(ki == 0)
    def _():
        _flash_init(m_ref, l_ref, acc_ref)

    @pl.when((ki <= k_last) & jnp.logical_not(partial))
    def _():
        step(ckv_ref[...], kr_ref[...], None)

    @pl.when((ki <= k_last) & partial)
    def _():
        step(ckv_ref[...], kr_ref[...], _chunk_mask(qi * tq, ki * tk, tq, tk))

    @pl.when(ki == k_last)
    def _():
        _flash_finish(wuv_ref, o_ref, l_ref, acc_ref, heads, vh)


def _mla_prompt(qlat, qrope, ckv, krope, wuv, *, batch, seq, name):
    heads, _, c = qlat.shape
    rope = qrope.shape[-1]
    vh = wuv.shape[-1]
    tq = _tile(seq, _MLA_TQ)
    tk = _tile(seq, _MLA_TK)
    nq, nk = seq // tq, seq // tk

    def kmap(b, qi, ki):
        return (b * nk + jnp.minimum(ki, ((qi + 1) * tq - 1) // tk), 0)

    return pl.pallas_call(
        functools.partial(_mla_prompt_kernel, heads=heads, group=_MLA_GROUP, tq=tq, tk=tk, vh=vh),
        out_shape=jax.ShapeDtypeStruct((batch * seq, heads * vh), BF16),
        grid=(batch, nq, nk),
        in_specs=[pl.BlockSpec((heads, tq, c), lambda b, qi, ki: (0, b * nq + qi, 0)),
                  pl.BlockSpec((heads, tq, rope), lambda b, qi, ki: (0, b * nq + qi, 0)),
                  pl.BlockSpec((tk, c), kmap),
                  pl.BlockSpec((tk, rope), kmap),
                  pl.BlockSpec((heads, c, vh), lambda b, qi, ki: (0, 0, 0))],
        out_specs=pl.BlockSpec((tq, heads * vh), lambda b, qi, ki: (b * nq + qi, 0)),
        scratch_shapes=[pltpu.VMEM((heads * tq, 1), F32), pltpu.VMEM((heads * tq, 1), F32),
                        pltpu.VMEM((heads * tq, c), F32)],
        compiler_params=_cparams(("parallel", "parallel", "arbitrary"), 48),
        name=name,
    )(qlat, qrope, ckv, krope, wuv)


def _mla_sample_kernel(qlat_ref, qrope_ref, cckv_ref, ckr_ref, nckv_ref, nkr_ref, wuv_ref, o_ref,
                       m_ref, l_ref, acc_ref, *, heads, group, tq, nkc, past, vh):
    ki = pl.program_id(1)

    @pl.when(ki == 0)
    def _():
        _flash_init(m_ref, l_ref, acc_ref)

    step = functools.partial(_flash_step, qlat_ref, qrope_ref, m_ref=m_ref, l_ref=l_ref,
                             acc_ref=acc_ref, heads=heads, group=group, tq=tq)

    @pl.when(ki < nkc)
    def _():
        step(cckv_ref[0].astype(BF16), ckr_ref[0].astype(BF16), None)

    @pl.when(ki == nkc)
    def _():
        step(nckv_ref[...], nkr_ref[...], _chunk_mask(past, past, tq, tq))
        _flash_finish(wuv_ref, o_ref, l_ref, acc_ref, heads, vh)


def _mla_sample(qlat, qrope, cache_ckv, cache_kr, ckv, krope, wuv, *, name):
    heads, _, c = qlat.shape
    rope = qrope.shape[-1]
    vh = wuv.shape[-1]
    batch, past, _ = cache_ckv.shape
    tq = ckv.shape[0] // batch
    tk = _tile(past, 512)
    nkc = past // tk
    cmap = lambda b, ki: (b, jnp.minimum(ki, nkc - 1), 0)
    return pl.pallas_call(
        functools.partial(_mla_sample_kernel, heads=heads, group=_MLA_GROUP, tq=tq, nkc=nkc,
                          past=past, vh=vh),
        out_shape=jax.ShapeDtypeStruct((batch * tq, heads * vh), BF16),
        grid=(batch, nkc + 1),
        in_specs=[pl.BlockSpec((heads, tq, c), lambda b, ki: (0, b, 0)),
                  pl.BlockSpec((heads, tq, rope), lambda b, ki: (0, b, 0)),
                  pl.BlockSpec((1, tk, c), cmap),
                  pl.BlockSpec((1, tk, rope), cmap),
                  pl.BlockSpec((tq, c), lambda b, ki: (b, 0)),
                  pl.BlockSpec((tq, rope), lambda b, ki: (b, 0)),
                  pl.BlockSpec((heads, c, vh), lambda b, ki: (0, 0, 0))],
        out_specs=pl.BlockSpec((tq, heads * vh), lambda b, ki: (b, 0)),
        scratch_shapes=[pltpu.VMEM((heads * tq, 1), F32), pltpu.VMEM((heads * tq, 1), F32),
                        pltpu.VMEM((heads * tq, c), F32)],
        compiler_params=_cparams(("parallel", "arbitrary"), 48),
        name=name,
    )(qlat, qrope, cache_ckv, cache_kr, ckv, krope, wuv)


SB_SKIP = -160.0


def _sb_block(q_ref, get_kv, tri, valid, run_ref, acc_ref, *, heads, group, tq):
    rows = group * tq
    for g0 in range(0, heads, group):
        kv = [get_kv(h) for h in range(g0, g0 + group)]
        z = jnp.concatenate([_dot_nt(q_ref[g0 + i], kv[i][0]) for i in range(group)], axis=0)
        tk = z.shape[-1]
        sp = jnp.log2(1.0 + jnp.exp2(-jnp.abs(z)))
        lk = -(jnp.maximum(z, 0.0) + sp)
        if valid is not None:
            lk = jnp.where(valid[None], lk.reshape(group, tq, tk), 0.0).reshape(rows, tk)
        hi = lk.astype(BF16)
        lo = (lk - hi.astype(F32)).astype(BF16)
        suffix = _dot(hi, tri) + _dot(lo, tri)
        run = run_ref[g0 * tq:g0 * tq + rows]
        w = jnp.exp2((jnp.minimum(z, 0.0) - sp) + suffix + run)
        if valid is not None:
            w = jnp.where(valid[None], w.reshape(group, tq, tk), 0.0).reshape(rows, tk)
        w = w.astype(BF16)
        run_ref[g0 * tq:g0 * tq + rows] = run + suffix[:, :1] + lk[:, :1]
        for i in range(group):
            r0 = (g0 + i) * tq
            acc_ref[r0:r0 + tq] = acc_ref[r0:r0 + tq] + _dot(w[i * tq:(i + 1) * tq], kv[i][1])


def _strict_lower(n):
    row = lax.broadcasted_iota(jnp.int32, (n, n), 0)
    col = lax.broadcasted_iota(jnp.int32, (n, n), 1)
    return col < row


def _all_below(run_ref, bound):
    m = jnp.max(run_ref[...], axis=0, keepdims=True)
    return m[0, 0] <= bound


def _sb_write(o_ref, acc_ref, heads, tq, hd):
    for h in range(heads):
        o_ref[:, h * hd:(h + 1) * hd] = acc_ref[h * tq:(h + 1) * tq].astype(o_ref.dtype)


def _sb_prompt_kernel(q_ref, kd_ref, vd_ref, tri_ref, k_hbm, v_hbm, o_ref, kbuf, vbuf, sem,
                      run_ref, acc_ref, *, heads, group, tq, hd, nq):
    b, qi = pl.program_id(0), pl.program_id(1)

    def copies(j, slot):
        row0 = pl.multiple_of((b * nq + j) * tq, tq)
        return (pltpu.make_async_copy(k_hbm.at[:, pl.ds(row0, tq), :], kbuf.at[slot], sem.at[0, slot]),
                pltpu.make_async_copy(v_hbm.at[:, pl.ds(row0, tq), :], vbuf.at[slot], sem.at[1, slot]))

    @pl.when(qi > 0)
    def _():
        for cp in copies(qi - 1, 0):
            cp.start()

    run_ref[...] = jnp.zeros(run_ref.shape, F32)
    acc_ref[...] = jnp.zeros(acc_ref.shape, F32)
    tri = tri_ref[...]
    blk = functools.partial(_sb_block, run_ref=run_ref, acc_ref=acc_ref, heads=heads, group=group,
                            tq=tq)
    blk(q_ref, lambda h: (kd_ref[h], vd_ref[h]), tri, _strict_lower(tq))

    def cond(c):
        return (c[0] >= 0) & (c[1] > 0)

    def body(c):
        j = c[0]
        slot = lax.rem(qi - 1 - j, 2)
        for cp in copies(j, slot):
            cp.wait()

        @pl.when(j > 0)
        def _():
            for cp in copies(j - 1, 1 - slot):
                cp.start()

        blk(q_ref, lambda h: (kbuf[slot, h], vbuf[slot, h]), tri, None)
        go = jnp.where(_all_below(run_ref, SB_SKIP), 0, 1).astype(jnp.int32)
        return (j - 1, go)

    j_end, _ = lax.while_loop(cond, body, (qi - 1, jnp.int32(1)))

    @pl.when(j_end >= 0)
    def _():
        for cp in copies(j_end, lax.rem(qi - 1 - j_end, 2)):
            cp.wait()

    _sb_write(o_ref, acc_ref, heads, tq, hd)


def _sb_prompt(q, k, v, tri, *, batch, seq, name):
    heads, _, hd = q.shape
    tq = tri.shape[0]
    nq = seq // tq
    blk = lambda b, qi: (0, b * nq + qi, 0)
    return pl.pallas_call(
        functools.partial(_sb_prompt_kernel, heads=heads, group=_SB_GROUP, tq=tq, hd=hd, nq=nq),
        out_shape=jax.ShapeDtypeStruct((batch * seq, heads * hd), BF16),
        grid=(batch, nq),
        in_specs=[pl.BlockSpec((heads, tq, hd), blk), pl.BlockSpec((heads, tq, hd), blk),
                  pl.BlockSpec((heads, tq, hd), blk),
                  pl.BlockSpec((tq, tq), lambda b, qi: (0, 0)),
                  pl.BlockSpec(memory_space=pl.ANY), pl.BlockSpec(memory_space=pl.ANY)],
        out_specs=pl.BlockSpec((tq, heads * hd), lambda b, qi: (b * nq + qi, 0)),
        scratch_shapes=[pltpu.VMEM((2, heads, tq, hd), BF16), pltpu.VMEM((2, heads, tq, hd), BF16),
                        pltpu.SemaphoreType.DMA((2, 2)),
                        pltpu.VMEM((heads * tq, 1), F32), pltpu.VMEM((heads * tq, hd), F32)],
        compiler_params=_cparams(("arbitrary", "arbitrary"), 40),
        name=name,
    )(q, k, v, tri, k, v)


def _sb_sample_kernel(q_ref, nk_ref, nv_ref, tri_ref, ck_hbm, cv_hbm, o_ref, kbuf, vbuf, sem,
                      run_ref, acc_ref, *, heads, group, tq, tk, hd, nkc):
    b = pl.program_id(0)

    def copies(j, slot):
        p0 = pl.multiple_of(j * tk, tk)
        return (pltpu.make_async_copy(ck_hbm.at[b, pl.ds(p0, tk)], kbuf.at[slot], sem.at[0, slot]),
                pltpu.make_async_copy(cv_hbm.at[b, pl.ds(p0, tk)], vbuf.at[slot], sem.at[1, slot]))

    for cp in copies(nkc - 1, 0):
        cp.start()

    run_ref[...] = jnp.zeros(run_ref.shape, F32)
    acc_ref[...] = jnp.zeros(acc_ref.shape, F32)
    blk = functools.partial(_sb_block, run_ref=run_ref, acc_ref=acc_ref, heads=heads, group=group,
                            tq=tq)
    blk(q_ref, lambda h: (nk_ref[h], nv_ref[h]), tri_ref[:tq, :tq], _strict_lower(tq))
    tri = tri_ref[...]

    def cond(c):
        return (c[0] >= 0) & (c[1] > 0)

    def body(c):
        j = c[0]
        slot = lax.rem(nkc - 1 - j, 2)
        for cp in copies(j, slot):
            cp.wait()

        @pl.when(j > 0)
        def _():
            for cp in copies(j - 1, 1 - slot):
                cp.start()

        blk(q_ref, lambda h: (kbuf[slot, :, h, :].astype(BF16), vbuf[slot, :, h, :].astype(BF16)),
            tri, None)
        go = jnp.where(_all_below(run_ref, SB_SKIP), 0, 1).astype(jnp.int32)
        return (j - 1, go)

    j_end, _ = lax.while_loop(cond, body, (jnp.int32(nkc - 1), jnp.int32(1)))

    @pl.when(j_end >= 0)
    def _():
        for cp in copies(j_end, lax.rem(nkc - 1 - j_end, 2)):
            cp.wait()

    _sb_write(o_ref, acc_ref, heads, tq, hd)


def _sb_sample(q, k_new, v_new, cache_k, cache_v, tri, *, name):
    heads, rows, hd = q.shape
    batch, past = cache_k.shape[:2]
    tq = rows // batch
    tk = tri.shape[0]
    nkc = past // tk
    new = lambda b: (0, b, 0)
    return pl.pallas_call(
        functools.partial(_sb_sample_kernel, heads=heads, group=_SB_GROUP, tq=tq, tk=tk, hd=hd,
                          nkc=nkc),
        out_shape=jax.ShapeDtypeStruct((rows, heads * hd), BF16),
        grid=(batch,),
        in_specs=[pl.BlockSpec((heads, tq, hd), new), pl.BlockSpec((heads, tq, hd), new),
                  pl.BlockSpec((heads, tq, hd), new),
                  pl.BlockSpec((tk, tk), lambda b: (0, 0)),
                  pl.BlockSpec(memory_space=pl.ANY), pl.BlockSpec(memory_space=pl.ANY)],
        out_specs=pl.BlockSpec((tq, heads * hd), lambda b: (b, 0)),
        scratch_shapes=[pltpu.VMEM((2, tk, heads, hd), F32), pltpu.VMEM((2, tk, heads, hd), F32),
                        pltpu.SemaphoreType.DMA((2, 2)),
                        pltpu.VMEM((heads * tq, 1), F32), pltpu.VMEM((heads * tq, hd), F32)],
        compiler_params=_cparams(("arbitrary",), 40),
        name=name,
    )(q, k_new, v_new, tri, cache_k, cache_v)


def _mem_attn_kernel(x_ref, g_ref, wq_ref, mk_ref, mv_ref, wo_ref, gn_ref, xo_ref, hn_ref, o_scr,
                     *, nsub, sub, heads, hd, scale):
    x = x_ref[...]
    mq = (_dot(_rms(x, g_ref[...]).astype(BF16), wq_ref[...]) * scale).astype(BF16)
    for s in range(nsub):
        for h in range(heads):
            q = mq[s * sub:(s + 1) * sub, h * hd:(h + 1) * hd]
            k = mk_ref[s, :, h * hd:(h + 1) * hd]
            v = mv_ref[s, :, h * hd:(h + 1) * hd]
            sc = _dot_nt(q, k)
            p = jnp.exp(sc - jnp.max(sc, axis=-1, keepdims=True))
            p = p * (1.0 / jnp.sum(p, axis=-1, keepdims=True))
            o_scr[s * sub:(s + 1) * sub, h * hd:(h + 1) * hd] = _dot(p.astype(BF16), v).astype(BF16)
    xn = x + _dot(o_scr[...], wo_ref[...])
    xo_ref[...] = xn
    hn_ref[...] = _rms(xn, gn_ref[...]).astype(hn_ref.dtype)


def _mem_attn(x, g, w_mq, mem_k, mem_v, w_mo, g_next, *, sub, heads, name):
    m, d = x.shape
    nb, n_mem, width = mem_k.shape
    hd = width // heads
    tm = _tile(m, 256)
    if sub >= tm:
        nsub, rows = 1, tm
        per = sub // tm
        mmap = lambda i: (i // per, 0, 0)
    else:
        nsub, rows = tm // sub, sub
        mmap = lambda i: (i, 0, 0)
    row = lambda i: (i, 0)
    fix = lambda i: (0, 0)
    return pl.pallas_call(
        functools.partial(_mem_attn_kernel, nsub=nsub, sub=rows, heads=heads, hd=hd,
                          scale=hd ** -0.5),
        out_shape=[jax.ShapeDtypeStruct((m, d), F32), jax.ShapeDtypeStruct((m, d), BF16)],
        grid=(m // tm,),
        in_specs=[pl.BlockSpec((tm, d), row), pl.BlockSpec((1, d), fix),
                  pl.BlockSpec((d, width), fix),
                  pl.BlockSpec((nsub, n_mem, width), mmap),
                  pl.BlockSpec((nsub, n_mem, width), mmap),
                  pl.BlockSpec((width, d), fix), pl.BlockSpec((1, d), fix)],
        out_specs=[pl.BlockSpec((tm, d), row), pl.BlockSpec((tm, d), row)],
        scratch_shapes=[pltpu.VMEM((tm, width), BF16)],
        compiler_params=_cparams(("parallel",), 48),
        name=name,
    )(x, g.reshape(1, d), w_mq, mem_k, mem_v, w_mo, g_next.reshape(1, d))


def _rotate_half_cols(w):
    half = w.shape[-1] // 2
    return jnp.concatenate([-w[..., half:], w[..., :half]], axis=-1)


def _rope_table(pos, half):
    inv = ROPE_THETA ** (-jnp.arange(half, dtype=F32) / half)
    ang = pos.astype(F32)[:, None] * inv[None, :]
    c, s = jnp.cos(ang), jnp.sin(ang)
    return jnp.concatenate([c, c, s, s], axis=-1)


def _prepare_weights(w_in, w_uq, w_uk, w_uv, w_branch_a, w_branch_b, w_out, w_mq, w_mk, w_mv, w_mo,
                     w_gate, w_up, w_down, dims):
    q_lora, kv_lora, rope, heads, nope = dims
    o_kr = q_lora + kv_lora
    o_sb = o_kr + rope
    w_lat = jnp.concatenate([w_in[:, :o_sb], _rotate_half_cols(w_in[:, o_kr:o_sb])], axis=1)
    wq_cat = jnp.concatenate([w_uq, _rotate_half_cols(w_uq[..., nope:])], axis=-1)
    return dict(
        w_lat=w_lat.astype(BF16),
        w_rest=w_in[:, o_sb:].astype(BF16),
        wq_cat=wq_cat.reshape(q_lora, -1).astype(BF16),
        wuk_t=jnp.transpose(w_uk, (1, 2, 0)).reshape(heads * nope, kv_lora).astype(BF16),
        wuv=jnp.transpose(w_uv, (1, 0, 2)).astype(BF16),
        w_ba=w_branch_a.astype(BF16), w_bb=w_branch_b.astype(BF16), w_out=w_out.astype(BF16),
        w_mq=w_mq.astype(BF16), w_mk=w_mk.astype(BF16), w_mv=w_mv.astype(BF16),
        w_mo=w_mo.astype(BF16),
        w_gate=w_gate.astype(BF16), w_up=w_up.astype(BF16), w_down=w_down.astype(BF16),
    )


def _layer(x, pos, past, mem_k, mem_v, w, gains, b_gate, dims, *, batch, tag):
    g_mix, g_q_lat, g_kv_lat, g_xattn, g_ffn = gains
    q_lora, kv_lora, rope, heads, nope = dims
    m, d = x.shape
    t = m // batch
    sb_width = (w["w_rest"].shape[1] - 2 * d) // 3
    sb_heads = sb_width // LANE
    mla_scale = (nope + rope) ** -0.5 * LOG2E
    sb_scale = LANE ** -0.5 * LOG2E
    tm = _tile(m, 1024)
    tn = 512

    h = _rmsnorm(x, g_mix, BF16, f"{tag}_norm_mix")
    lat_w = w["w_lat"].shape[1]
    (p_lat,) = _fused_matmul(
        [h], [(0, w["w_lat"], 0)], [], [(F32, "tile")], lambda accs, ex: (accs[0],),
        n=lat_w, tm=_tile(m, 256), tn=lat_w, name=f"{tag}_proj_lat")
    cs = jnp.tile(_rope_table(pos, rope // 2), (batch, 1))
    cqn, ckv, ckv_b, krope, krope_b = _lat_post(p_lat, g_q_lat, g_kv_lat, cs, q_lora=q_lora,
                                                kv_lora=kv_lora, rope=rope, name=f"{tag}_lat_post")
    nsb = sb_width // tn
    (sbq,) = _fused_matmul(
        [h], [(0, w["w_rest"], 0)], [], [(BF16, "heads")],
        lambda accs, ex: (accs[0] * sb_scale,), n=sb_width, tm=tm, tn=tn, name=f"{tag}_proj_sbq")
    sbk, sbk_b = _fused_matmul(
        [h], [(0, w["w_rest"], nsb)], [], [(F32, "tile"), (BF16, "heads")],
        lambda accs, ex: (accs[0], accs[0]), n=sb_width, tm=tm, tn=tn, name=f"{tag}_proj_sbk")
    sbv, sbv_b = _fused_matmul(
        [h], [(0, w["w_rest"], 2 * nsb)], [], [(F32, "tile"), (BF16, "heads")],
        lambda accs, ex: (accs[0], accs[0]), n=sb_width, tm=tm, tn=tn, name=f"{tag}_proj_sbv")
    (gates,) = _fused_matmul(
        [h], [(0, w["w_rest"], 3 * nsb)], [(b_gate.reshape(1, -1), "row", 0)], [(BF16, "tile")],
        lambda accs, ex: (_sigmoid(accs[0] + ex[0]),), n=2 * d, tm=tm, tn=tn,
        name=f"{tag}_proj_gates")

    qlat, qrope = _mla_q(cqn, w["wq_cat"], w["wuk_t"], cs, heads=heads, nope=nope, rope=rope,
                         kv_lora=kv_lora, scale=mla_scale, name=f"{tag}_mla_q")
    tri_n = 256 if t % 256 == 0 else t
    tri = (jnp.arange(tri_n)[:, None] > jnp.arange(tri_n)[None, :]).astype(BF16)
    if past is None:
        o_a = _mla_prompt(qlat, qrope, ckv_b, krope_b, w["wuv"], batch=batch, seq=t,
                          name=f"{tag}_mla_attn")
        o_b = _sb_prompt(sbq, sbk_b, sbv_b, tri, batch=batch, seq=t, name=f"{tag}_sb_attn")
    else:
        c_ckv, c_kr, c_k, c_v = past
        o_a = _mla_sample(qlat, qrope, c_ckv, c_kr, ckv_b, krope_b, w["wuv"], name=f"{tag}_mla_attn")
        tri = (jnp.arange(256)[:, None] > jnp.arange(256)[None, :]).astype(BF16)
        o_b = _sb_sample(sbq, sbk_b, sbv_b, c_k, c_v, tri, name=f"{tag}_sb_attn")

    ng = d // tn
    (merged,) = _fused_matmul(
        [o_a, o_b], [(0, w["w_ba"], 0), (1, w["w_bb"], 0)],
        [(gates, "tile", 0), (gates, "tile", ng)], [(BF16, "tile")],
        lambda accs, ex: (ex[0].astype(F32) * accs[0] + ex[1].astype(F32) * accs[1],),
        n=d, tm=tm, tn=tn, name=f"{tag}_merge")
    (x,) = _fused_matmul(
        [merged], [(0, w["w_out"], 0)], [(x, "tile", 0)], [(F32, "tile")],
        lambda accs, ex: (ex[0] + accs[0],), n=d, tm=tm, tn=tn, name=f"{tag}_out_proj")

    mem_heads = mem_k.shape[2]
    mk = mem_k.reshape(mem_k.shape[0], mem_k.shape[1], -1).astype(BF16)
    mv = mem_v.reshape(mem_v.shape[0], mem_v.shape[1], -1).astype(BF16)
    x, hf = _mem_attn(x, g_xattn, w["w_mq"], mk, mv, w["w_mo"], g_ffn, sub=t, heads=mem_heads,
                      name=f"{tag}_mem_attn")

    d_ff = w["w_gate"].shape[1]
    tn_ff = _tile(d_ff, 256) if d_ff % 512 else 512
    (act,) = _fused_matmul(
        [hf], [(0, w["w_gate"], 0), (0, w["w_up"], 0)], [], [(BF16, "tile")],
        lambda accs, ex: (accs[0] * _sigmoid(accs[0]) * accs[1],), n=d_ff, tm=tm, tn=tn_ff,
        name=f"{tag}_ffn_up")
    (x,) = _fused_matmul(
        [act], [(0, w["w_down"], 0)], [(x, "tile", 0)], [(F32, "tile")],
        lambda accs, ex: (ex[0] + accs[0],), n=d, tm=_tile(m, 512), tn=256, name=f"{tag}_ffn_down")
    return x, (ckv, krope, sbk, sbv)


def kernel(x_prompt, x_sample, cache_mla_ckv, cache_mla_krope, cache_sb_k, cache_sb_v, cache_mem_k, cache_mem_v, mem_prompt, g_mix, w_in, b_gate, g_q_lat, w_uq, g_kv_lat, w_uk, w_uv, w_branch_a, w_branch_b, w_out, g_xattn, g_mem, w_mq, w_mk, w_mv, w_mo, g_ffn, w_gate, w_up, w_down, g_final):
    depth = w_in.shape[0]
    bp, seq, d = x_prompt.shape
    bs, dec, _ = x_sample.shape
    past_len = cache_mla_ckv.shape[2]
    q_lora, heads, qk = w_uq.shape[1:]
    kv_lora, _, nope = w_uk.shape[1:]
    rope = qk - nope
    dims = (q_lora, kv_lora, rope, heads, nope)
    sb_heads, sb_hd = cache_sb_k.shape[3:]
    n_mem, mem_heads, mem_hd = cache_mem_k.shape[2:]
    pos_p = jnp.arange(seq)
    pos_s = past_len + jnp.arange(dec)

    xp = x_prompt.reshape(bp * seq, d)
    xs = x_sample.reshape(bs * dec, d)
    outs = [[] for _ in range(10)]
    for l in range(depth):
        w = _prepare_weights(w_in[l], w_uq[l], w_uk[l], w_uv[l], w_branch_a[l], w_branch_b[l],
                             w_out[l], w_mq[l], w_mk[l], w_mv[l], w_mo[l], w_gate[l], w_up[l],
                             w_down[l], dims)
        gains = (g_mix[l], g_q_lat[l], g_kv_lat[l], g_xattn[l], g_ffn[l])
        mn = _rmsnorm(mem_prompt.reshape(bp * n_mem, d), g_mem[l], BF16, f"l{l}_norm_mem")
        mem_w = w["w_mk"].shape[1]
        mk, mv = _fused_matmul(
            [mn], [(0, w["w_mk"], 0), (0, w["w_mv"], 0)], [], [(F32, "tile"), (F32, "tile")],
            lambda accs, ex: (accs[0], accs[1]), n=mem_w, tm=_tile(bp * n_mem, 512),
            tn=_tile(mem_w, 512), name=f"l{l}_mem_kv")
        mk = mk.reshape(bp, n_mem, mem_heads, mem_hd)
        mv = mv.reshape(bp, n_mem, mem_heads, mem_hd)
        xp, (ckv, kr, k, v) = _layer(xp, pos_p, None, mk, mv, w, gains, b_gate[l], dims,
                                     batch=bp, tag=f"l{l}p")
        for lst, val in zip(outs[:6], (ckv.reshape(bp, seq, -1), kr.reshape(bp, seq, -1),
                                       k.reshape(bp, seq, sb_heads, sb_hd),
                                       v.reshape(bp, seq, sb_heads, sb_hd), mk, mv)):
            lst.append(val)
        past = (cache_mla_ckv[l], cache_mla_krope[l], cache_sb_k[l], cache_sb_v[l])
        xs, (ckv, kr, k, v) = _layer(xs, pos_s, past, cache_mem_k[l], cache_mem_v[l], w, gains,
                                     b_gate[l], dims, batch=bs, tag=f"l{l}s")
        for lst, val in zip(outs[6:], (ckv.reshape(bs, dec, -1), kr.reshape(bs, dec, -1),
                                       k.reshape(bs, dec, sb_heads, sb_hd),
                                       v.reshape(bs, dec, sb_heads, sb_hd))):
            lst.append(val)
    y_prompt = _rmsnorm(xp, g_final, F32, "final_norm_p").reshape(bp, seq, d)
    y_sample = _rmsnorm(xs, g_final, F32, "final_norm_s").reshape(bs, dec, d)
    return (y_prompt, y_sample) + tuple(jnp.stack(o) for o in outs)
```

```python
import functools
import math

import jax
import jax.numpy as jnp
from jax import lax
from jax.experimental import pallas as pl
from jax.experimental.pallas import tpu as pltpu

F32 = jnp.float32
BF16 = jnp.bfloat16

CHUNK = 64
EPS = 1e-6
ROPE_THETA = 10000.0
NEG_BIG = -1e30
LOG2E = math.log2(math.e)
MIB = 1024 * 1024
LANE = 128
_SB_GROUP = 4
_MLA_GROUP = 2
_MLA_TQ = 256
_MLA_TK = 1024


def _cparams(sem, vmem_mib):
    return pltpu.CompilerParams(dimension_semantics=sem, vmem_limit_bytes=vmem_mib * MIB)


def _dot(a, b):
    return jnp.dot(a, b, preferred_element_type=F32)


def _dot_nt(a, b):
    return lax.dot_general(a, b, (((1,), (1,)), ((), ())), preferred_element_type=F32)


def _sigmoid(x):
    return 1.0 / (1.0 + jnp.exp(-x))


def _rms(x, g):
    return x * lax.rsqrt(jnp.mean(x * x, axis=-1, keepdims=True) + EPS) * g


def _tile(n, pref):
    if n <= pref:
        return n
    t = pref
    while n % t:
        t //= 2
    return t


def _norm_kernel(x_ref, g_ref, o_ref):
    o_ref[...] = _rms(x_ref[...], g_ref[...]).astype(o_ref.dtype)


def _rmsnorm(x, g, out_dtype, name):
    m, d = x.shape
    tm = _tile(m, 256)
    return pl.pallas_call(
        _norm_kernel,
        out_shape=jax.ShapeDtypeStruct((m, d), out_dtype),
        grid=(m // tm,),
        in_specs=[pl.BlockSpec((tm, d), lambda i: (i, 0)),
                  pl.BlockSpec((1, d), lambda i: (0, 0))],
        out_specs=pl.BlockSpec((tm, d), lambda i: (i, 0)),
        compiler_params=_cparams(("parallel",), 40),
        name=name,
    )(x, g.reshape(1, d))


def _fused_matmul(lhs, dots, extras, outs, epilogue, *, n, tm, tn, name, vmem_mib=48):
    m = lhs[0].shape[0]
    na, nd, ne = len(lhs), len(dots), len(extras)
    hpt = tn // LANE

    def kernel(*refs):
        a_refs, w_refs = refs[:na], refs[na:na + nd]
        e_refs = refs[na + nd:na + nd + ne]
        o_refs = refs[na + nd + ne:]
        accs = [_dot(a_refs[k][...], w[...]) for (k, _, _), w in zip(dots, w_refs)]
        vals = epilogue(accs, [e[...] for e in e_refs])
        for o_ref, v, (_, kind) in zip(o_refs, vals, outs):
            if kind == "tile":
                o_ref[...] = v.astype(o_ref.dtype)
            else:
                for hh in range(hpt):
                    o_ref[hh] = v[:, hh * LANE:(hh + 1) * LANE].astype(o_ref.dtype)

    in_specs, args = [], []
    for a in lhs:
        in_specs.append(pl.BlockSpec((tm, a.shape[1]), lambda i, j: (i, 0)))
        args.append(a)
    for _, w, off in dots:
        in_specs.append(pl.BlockSpec((w.shape[0], tn), lambda i, j, off=off: (0, j + off)))
        args.append(w)
    for e, kind, off in extras:
        if kind == "row":
            in_specs.append(pl.BlockSpec((1, tn), lambda i, j, off=off: (0, j + off)))
        else:
            in_specs.append(pl.BlockSpec((tm, tn), lambda i, j, off=off: (i, j + off)))
        args.append(e)
    out_shape, out_specs = [], []
    for dt, kind in outs:
        if kind == "tile":
            out_shape.append(jax.ShapeDtypeStruct((m, n), dt))
            out_specs.append(pl.BlockSpec((tm, tn), lambda i, j: (i, j)))
        else:
            out_shape.append(jax.ShapeDtypeStruct((n // LANE, m, LANE), dt))
            out_specs.append(pl.BlockSpec((hpt, tm, LANE), lambda i, j: (j, i, 0)))
    return pl.pallas_call(
        kernel,
        out_shape=out_shape,
        grid=(m // tm, n // tn),
        in_specs=in_specs,
        out_specs=out_specs,
        compiler_params=_cparams(("parallel", "arbitrary"), vmem_mib),
        name=name,
    )(*args)


def _lat_post_kernel(p_ref, gq_ref, gkv_ref, cs_ref, cqn_ref, ckv_ref, ckvb_ref, kr_ref, krb_ref,
                     *, q_lora, kv_lora, rope):
    p = p_ref[...]
    cqn_ref[...] = _rms(p[:, :q_lora], gq_ref[...]).astype(cqn_ref.dtype)
    ckv = _rms(p[:, q_lora:q_lora + kv_lora], gkv_ref[...])
    ckv_ref[...] = ckv
    ckvb_ref[...] = ckv.astype(ckvb_ref.dtype)
    t = p[:, q_lora + kv_lora:] * cs_ref[...]
    kr = (t + pltpu.roll(t, rope, axis=1))[:, :rope]
    kr_ref[...] = kr
    krb_ref[...] = kr.astype(krb_ref.dtype)


def _lat_post(p, g_q, g_kv, cs, *, q_lora, kv_lora, rope, name):
    m, w = p.shape
    tm = _tile(m, 512)
    row = lambda i: (i, 0)
    fix = lambda i: (0, 0)
    return pl.pallas_call(
        functools.partial(_lat_post_kernel, q_lora=q_lora, kv_lora=kv_lora, rope=rope),
        out_shape=[jax.ShapeDtypeStruct((m, q_lora), BF16),
                   jax.ShapeDtypeStruct((m, kv_lora), F32),
                   jax.ShapeDtypeStruct((m, kv_lora), BF16),
                   jax.ShapeDtypeStruct((m, rope), F32),
                   jax.ShapeDtypeStruct((m, rope), BF16)],
        grid=(m // tm,),
        in_specs=[pl.BlockSpec((tm, w), row), pl.BlockSpec((1, q_lora), fix),
                  pl.BlockSpec((1, kv_lora), fix), pl.BlockSpec((tm, 2 * rope), row)],
        out_specs=[pl.BlockSpec((tm, q_lora), row), pl.BlockSpec((tm, kv_lora), row),
                   pl.BlockSpec((tm, kv_lora), row), pl.BlockSpec((tm, rope), row),
                   pl.BlockSpec((tm, rope), row)],
        compiler_params=_cparams(("parallel",), 32),
        name=name,
    )(p, g_q.reshape(1, -1), g_kv.reshape(1, -1), cs)


def _mla_q_kernel(cqn_ref, wq_ref, wuk_ref, cs_ref, qlat_ref, qrope_ref, *, nope, rope, scale):
    qh = _dot(cqn_ref[...], wq_ref[...])
    qn = qh[:, :nope].astype(BF16)
    qlat_ref[0] = (_dot(qn, wuk_ref[...]) * scale).astype(qlat_ref.dtype)
    t = qh[:, nope:] * cs_ref[...]
    qr = (t + pltpu.roll(t, rope, axis=1))[:, :rope]
    qrope_ref[0] = (qr * scale).astype(qrope_ref.dtype)


def _mla_q(cqn, wq_cat, wuk_t, cs, *, heads, nope, rope, kv_lora, scale, name):
    m, q_lora = cqn.shape
    tm = _tile(m, 1024)
    hw = nope + 2 * rope
    return pl.pallas_call(
        functools.partial(_mla_q_kernel, nope=nope, rope=rope, scale=scale),
        out_shape=[jax.ShapeDtypeStruct((heads, m, kv_lora), BF16),
                   jax.ShapeDtypeStruct((heads, m, rope), BF16)],
        grid=(m // tm, heads),
        in_specs=[pl.BlockSpec((tm, q_lora), lambda i, h: (i, 0)),
                  pl.BlockSpec((q_lora, hw), lambda i, h: (0, h)),
                  pl.BlockSpec((nope, kv_lora), lambda i, h: (h, 0)),
                  pl.BlockSpec((tm, 2 * rope), lambda i, h: (i, 0))],
        out_specs=[pl.BlockSpec((1, tm, kv_lora), lambda i, h: (h, i, 0)),
                   pl.BlockSpec((1, tm, rope), lambda i, h: (h, i, 0))],
        compiler_params=_cparams(("parallel", "arbitrary"), 32),
        name=name,
    )(cqn, wq_cat, wuk_t, cs)


def _flash_step(qlat_ref, qrope_ref, k, kr, mask, m_ref, l_ref, acc_ref, *, heads, group, tq):
    rows = group * tq
    tk = k.shape[0]

    def scores(g0):
        q = qlat_ref[g0:g0 + group].reshape(rows, qlat_ref.shape[-1])
        qr = qrope_ref[g0:g0 + group].reshape(rows, qrope_ref.shape[-1])
        return _dot_nt(q, k) + _dot_nt(qr, kr)

    s_next = scores(0)
    for g0 in range(0, heads, group):
        s = s_next
        if g0 + group < heads:
            s_next = scores(g0 + group)
        if mask is not None:
            s = jnp.where(mask[None], s.reshape(group, tq, tk), NEG_BIG).reshape(rows, tk)
        sl = slice(g0 * tq, g0 * tq + rows)
        m_prev = m_ref[sl]
        m_new = jnp.maximum(m_prev, jnp.max(s, axis=-1, keepdims=True))
        p = jnp.exp2(s - m_new)
        alpha = jnp.exp2(m_prev - m_new)
        l_ref[sl] = alpha * l_ref[sl] + jnp.sum(p, axis=-1, keepdims=True)
        acc_ref[sl] = alpha * acc_ref[sl] + _dot(p.astype(BF16), k)
        m_ref[sl] = m_new


def _flash_init(m_ref, l_ref, acc_ref):
    m_ref[...] = jnp.full(m_ref.shape, NEG_BIG, F32)
    l_ref[...] = jnp.zeros(l_ref.shape, F32)
    acc_ref[...] = jnp.zeros(acc_ref.shape, F32)


def _flash_finish(wuv_ref, o_ref, l_ref, acc_ref, heads, vh):
    tq = acc_ref.shape[0] // heads
    o = (acc_ref[...] * (1.0 / l_ref[...])).astype(BF16)
    for h in range(heads):
        o_ref[:, h * vh:(h + 1) * vh] = _dot(o[h * tq:(h + 1) * tq], wuv_ref[h]).astype(o_ref.dtype)


def _chunk_mask(q0, k0, tq, tk):
    qc = (q0 + lax.broadcasted_iota(jnp.int32, (tq, tk), 0)) // CHUNK
    kc = (k0 + lax.broadcasted_iota(jnp.int32, (tq, tk), 1)) // CHUNK
    return kc <= qc


def _mla_prompt_kernel(qlat_ref, qrope_ref, ckv_ref, kr_ref, wuv_ref, o_ref, m_ref, l_ref, acc_ref,
                       *, heads, group, tq, tk, vh):
    qi, ki = pl.program_id(1), pl.program_id(2)
    k_last = ((qi + 1) * tq - 1) // tk
    partial = (ki + 1) * tk > qi * tq + CHUNK
    step = functools.partial(_flash_step, qlat_ref, qrope_ref, m_ref=m_ref, l_ref=l_ref,
                             acc_ref=acc_ref, heads=heads, group=group, tq=tq)

    @pl.when(ki == 0)
    def _():
        _flash_init(m_ref, l_ref, acc_ref)

    @pl.when((ki <= k_last) & jnp.logical_not(partial))
    def _():
        step(ckv_ref[...], kr_ref[...], None)

    @pl.when((ki <= k_last) & partial)
    def _():
        step(ckv_ref[...], kr_ref[...], _chunk_mask(qi * tq, ki * tk, tq, tk))

    @pl.when(ki == k_last)
    def _():
        _flash_finish(wuv_ref, o_ref, l_ref, acc_ref, heads, vh)


def _mla_prompt(qlat, qrope, ckv, krope, wuv, *, batch, seq, name):
    heads, _, c = qlat.shape
    rope = qrope.shape[-1]
    vh = wuv.shape[-1]
    tq = _tile(seq, _MLA_TQ)
    tk = _tile(seq, _MLA_TK)
    nq, nk = seq // tq, seq // tk

    def kmap(b, qi, ki):
        return (b * nk + jnp.minimum(ki, ((qi + 1) * tq - 1) // tk), 0)

    return pl.pallas_call(
        functools.partial(_mla_prompt_kernel, heads=heads, group=_MLA_GROUP, tq=tq, tk=tk, vh=vh),
        out_shape=jax.ShapeDtypeStruct((batch * seq, heads * vh), BF16),
        grid=(batch, nq, nk),
        in_specs=[pl.BlockSpec((heads, tq, c), lambda b, qi, ki: (0, b * nq + qi, 0)),
                  pl.BlockSpec((heads, tq, rope), lambda b, qi, ki: (0, b * nq + qi, 0)),
                  pl.BlockSpec((tk, c), kmap),
                  pl.BlockSpec((tk, rope), kmap),
                  pl.BlockSpec((heads, c, vh), lambda b, qi, ki: (0, 0, 0))],
        out_specs=pl.BlockSpec((tq, heads * vh), lambda b, qi, ki: (b * nq + qi, 0)),
        scratch_shapes=[pltpu.VMEM((heads * tq, 1), F32), pltpu.VMEM((heads * tq, 1), F32),
                        pltpu.VMEM((heads * tq, c), F32)],
        compiler_params=_cparams(("parallel", "parallel", "arbitrary"), 48),
        name=name,
    )(qlat, qrope, ckv, krope, wuv)


def _mla_sample_kernel(qlat_ref, qrope_ref, cckv_ref, ckr_ref, nckv_ref, nkr_ref, wuv_ref, o_ref,
                       m_ref, l_ref, acc_ref, *, heads, group, tq, nkc, past, vh):
    ki = pl.program_id(1)

    @pl.when(ki == 0)
    def _():
        _flash_init(m_ref, l_ref, acc_ref)

    step = functools.partial(_flash_step, qlat_ref, qrope_ref, m_ref=m_ref, l_ref=l_ref,
                             acc_ref=acc_ref, heads=heads, group=group, tq=tq)

    @pl.when(ki < nkc)
    def _():
        step(cckv_ref[0].astype(BF16), ckr_ref[0].astype(BF16), None)

    @pl.when(ki == nkc)
    def _():
        step(nckv_ref[...], nkr_ref[...], _chunk_mask(past, past, tq, tq))
        _flash_finish(wuv_ref, o_ref, l_ref, acc_ref, heads, vh)


def _mla_sample(qlat, qrope, cache_ckv, cache_kr, ckv, krope, wuv, *, name):
    heads, _, c = qlat.shape
    rope = qrope.shape[-1]
    vh = wuv.shape[-1]
    batch, past, _ = cache_ckv.shape
    tq = ckv.shape[0] // batch
    tk = _tile(past, 512)
    nkc = past // tk
    cmap = lambda b, ki: (b, jnp.minimum(ki, nkc - 1), 0)
    return pl.pallas_call(
        functools.partial(_mla_sample_kernel, heads=heads, group=_MLA_GROUP, tq=tq, nkc=nkc,
                          past=past, vh=vh),
        out_shape=jax.ShapeDtypeStruct((batch * tq, heads * vh), BF16),
        grid=(batch, nkc + 1),
        in_specs=[pl.BlockSpec((heads, tq, c), lambda b, ki: (0, b, 0)),
                  pl.BlockSpec((heads, tq, rope), lambda b, ki: (0, b, 0)),
                  pl.BlockSpec((1, tk, c), cmap),
                  pl.BlockSpec((1, tk, rope), cmap),
                  pl.BlockSpec((tq, c), lambda b, ki: (b, 0)),
                  pl.BlockSpec((tq, rope), lambda b, ki: (b, 0)),
                  pl.BlockSpec((heads, c, vh), lambda b, ki: (0, 0, 0))],
        out_specs=pl.BlockSpec((tq, heads * vh), lambda b, ki: (b, 0)),
        scratch_shapes=[pltpu.VMEM((heads * tq, 1), F32), pltpu.VMEM((heads * tq, 1), F32),
                        pltpu.VMEM((heads * tq, c), F32)],
        compiler_params=_cparams(("parallel", "arbitrary"), 48),
        name=name,
    )(qlat, qrope, cache_ckv, cache_kr, ckv, krope, wuv)


SB_SKIP = -160.0


def _sb_block(q_ref, get_kv, tri, valid, run_ref, acc_ref, *, heads, group, tq):
    rows = group * tq
    for g0 in range(0, heads, group):
        kv = [get_kv(h) for h in range(g0, g0 + group)]
        z = jnp.concatenate([_dot_nt(q_ref[g0 + i], kv[i][0]) for i in range(group)], axis=0)
        tk = z.shape[-1]
        sp = jnp.log2(1.0 + jnp.exp2(-jnp.abs(z)))
        lk = -(jnp.maximum(z, 0.0) + sp)
        if valid is not None:
            lk = jnp.where(valid[None], lk.reshape(group, tq, tk), 0.0).reshape(rows, tk)
        hi = lk.astype(BF16)
        lo = (lk - hi.astype(F32)).astype(BF16)
        suffix = _dot(hi, tri) + _dot(lo, tri)
        run = run_ref[g0 * tq:g0 * tq + rows]
        w = jnp.exp2((jnp.minimum(z, 0.0) - sp) + suffix + run)
        if valid is not None:
            w = jnp.where(valid[None], w.reshape(group, tq, tk), 0.0).reshape(rows, tk)
        w = w.astype(BF16)
        run_ref[g0 * tq:g0 * tq + rows] = run + suffix[:, :1] + lk[:, :1]
        for i in range(group):
            r0 = (g0 + i) * tq
            acc_ref[r0:r0 + tq] = acc_ref[r0:r0 + tq] + _dot(w[i * tq:(i + 1) * tq], kv[i][1])


def _strict_lower(n):
    row = lax.broadcasted_iota(jnp.int32, (n, n), 0)
    col = lax.broadcasted_iota(jnp.int32, (n, n), 1)
    return col < row


def _all_below(run_ref, bound):
    m = jnp.max(run_ref[...], axis=0, keepdims=True)
    return m[0, 0] <= bound


def _sb_write(o_ref, acc_ref, heads, tq, hd):
    for h in range(heads):
        o_ref[:, h * hd:(h + 1) * hd] = acc_ref[h * tq:(h + 1) * tq].astype(o_ref.dtype)


def _sb_prompt_kernel(q_ref, kd_ref, vd_ref, tri_ref, k_hbm, v_hbm, o_ref, kbuf, vbuf, sem,
                      run_ref, acc_ref, *, heads, group, tq, hd, nq):
    b, qi = pl.program_id(0), pl.program_id(1)

    def copies(j, slot):
        row0 = pl.multiple_of((b * nq + j) * tq, tq)
        return (pltpu.make_async_copy(k_hbm.at[:, pl.ds(row0, tq), :], kbuf.at[slot], sem.at[0, slot]),
                pltpu.make_async_copy(v_hbm.at[:, pl.ds(row0, tq), :], vbuf.at[slot], sem.at[1, slot]))

    @pl.when(qi > 0)
    def _():
        for cp in copies(qi - 1, 0):
            cp.start()

    run_ref[...] = jnp.zeros(run_ref.shape, F32)
    acc_ref[...] = jnp.zeros(acc_ref.shape, F32)
    tri = tri_ref[...]
    blk = functools.partial(_sb_block, run_ref=run_ref, acc_ref=acc_ref, heads=heads, group=group,
                            tq=tq)
    blk(q_ref, lambda h: (kd_ref[h], vd_ref[h]), tri, _strict_lower(tq))

    def cond(c):
        return (c[0] >= 0) & (c[1] > 0)

    def body(c):
        j = c[0]
        slot = lax.rem(qi - 1 - j, 2)
        for cp in copies(j, slot):
            cp.wait()

        @pl.when(j > 0)
        def _():
            for cp in copies(j - 1, 1 - slot):
                cp.start()

        blk(q_ref, lambda h: (kbuf[slot, h], vbuf[slot, h]), tri, None)
        go = jnp.where(_all_below(run_ref, SB_SKIP), 0, 1).astype(jnp.int32)
        return (j - 1, go)

    j_end, _ = lax.while_loop(cond, body, (qi - 1, jnp.int32(1)))

    @pl.when(j_end >= 0)
    def _():
        for cp in copies(j_end, lax.rem(qi - 1 - j_end, 2)):
            cp.wait()

    _sb_write(o_ref, acc_ref, heads, tq, hd)


def _sb_prompt(q, k, v, tri, *, batch, seq, name):
    heads, _, hd = q.shape
    tq = tri.shape[0]
    nq = seq // tq
    blk = lambda b, qi: (0, b * nq + qi, 0)
    return pl.pallas_call(
        functools.partial(_sb_prompt_kernel, heads=heads, group=_SB_GROUP, tq=tq, hd=hd, nq=nq),
        out_shape=jax.ShapeDtypeStruct((batch * seq, heads * hd), BF16),
        grid=(batch, nq),
        in_specs=[pl.BlockSpec((heads, tq, hd), blk), pl.BlockSpec((heads, tq, hd), blk),
                  pl.BlockSpec((heads, tq, hd), blk),
                  pl.BlockSpec((tq, tq), lambda b, qi: (0, 0)),
                  pl.BlockSpec(memory_space=pl.ANY), pl.BlockSpec(memory_space=pl.ANY)],
        out_specs=pl.BlockSpec((tq, heads * hd), lambda b, qi: (b * nq + qi, 0)),
        scratch_shapes=[pltpu.VMEM((2, heads, tq, hd), BF16), pltpu.VMEM((2, heads, tq, hd), BF16),
                        pltpu.SemaphoreType.DMA((2, 2)),
                        pltpu.VMEM((heads * tq, 1), F32), pltpu.VMEM((heads * tq, hd), F32)],
        compiler_params=_cparams(("arbitrary", "arbitrary"), 40),
        name=name,
    )(q, k, v, tri, k, v)


def _sb_sample_kernel(q_ref, nk_ref, nv_ref, tri_ref, ck_hbm, cv_hbm, o_ref, kbuf, vbuf, sem,
                      run_ref, acc_ref, *, heads, group, tq, tk, hd, nkc):
    b = pl.program_id(0)

    def copies(j, slot):
        p0 = pl.multiple_of(j * tk, tk)
        return (pltpu.make_async_copy(ck_hbm.at[b, pl.ds(p0, tk)], kbuf.at[slot], sem.at[0, slot]),
                pltpu.make_async_copy(cv_hbm.at[b, pl.ds(p0, tk)], vbuf.at[slot], sem.at[1, slot]))

    for cp in copies(nkc - 1, 0):
        cp.start()

    run_ref[...] = jnp.zeros(run_ref.shape, F32)
    acc_ref[...] = jnp.zeros(acc_ref.shape, F32)
    blk = functools.partial(_sb_block, run_ref=run_ref, acc_ref=acc_ref, heads=heads, group=group,
                            tq=tq)
    blk(q_ref, lambda h: (nk_ref[h], nv_ref[h]), tri_ref[:tq, :tq], _strict_lower(tq))
    tri = tri_ref[...]

    def cond(c):
        return (c[0] >= 0) & (c[1] > 0)

    def body(c):
        j = c[0]
        slot = lax.rem(nkc - 1 - j, 2)
        for cp in copies(j, slot):
            cp.wait()

        @pl.when(j > 0)
        def _():
            for cp in copies(j - 1, 1 - slot):
                cp.start()

        blk(q_ref, lambda h: (kbuf[slot, :, h, :].astype(BF16), vbuf[slot, :, h, :].astype(BF16)),
            tri, None)
        go = jnp.where(_all_below(run_ref, SB_SKIP), 0, 1).astype(jnp.int32)
        return (j - 1, go)

    j_end, _ = lax.while_loop(cond, body, (jnp.int32(nkc - 1), jnp.int32(1)))

    @pl.when(j_end >= 0)
    def _():
        for cp in copies(j_end, lax.rem(nkc - 1 - j_end, 2)):
            cp.wait()

    _sb_write(o_ref, acc_ref, heads, tq, hd)


def _sb_sample(q, k_new, v_new, cache_k, cache_v, tri, *, name):
    heads, rows, hd = q.shape
    batch, past = cache_k.shape[:2]
    tq = rows // batch
    tk = tri.shape[0]
    nkc = past // tk
    new = lambda b: (0, b, 0)
    return pl.pallas_call(
        functools.partial(_sb_sample_kernel, heads=heads, group=_SB_GROUP, tq=tq, tk=tk, hd=hd,
                          nkc=nkc),
        out_shape=jax.ShapeDtypeStruct((rows, heads * hd), BF16),
        grid=(batch,),
        in_specs=[pl.BlockSpec((heads, tq, hd), new), pl.BlockSpec((heads, tq, hd), new),
                  pl.BlockSpec((heads, tq, hd), new),
                  pl.BlockSpec((tk, tk), lambda b: (0, 0)),
                  pl.BlockSpec(memory_space=pl.ANY), pl.BlockSpec(memory_space=pl.ANY)],
        out_specs=pl.BlockSpec((tq, heads * hd), lambda b: (b, 0)),
        scratch_shapes=[pltpu.VMEM((2, tk, heads, hd), F32), pltpu.VMEM((2, tk, heads, hd), F32),
                        pltpu.SemaphoreType.DMA((2, 2)),
                        pltpu.VMEM((heads * tq, 1), F32), pltpu.VMEM((heads * tq, hd), F32)],
        compiler_params=_cparams(("arbitrary",), 40),
        name=name,
    )(q, k_new, v_new, tri, cache_k, cache_v)


def _mem_attn_kernel(x_ref, g_ref, wq_ref, mk_ref, mv_ref, wo_ref, gn_ref, xo_ref, hn_ref, o_scr,
                     *, nsub, sub, heads, hd, scale):
    x = x_ref[...]
    mq = (_dot(_rms(x, g_ref[...]).astype(BF16), wq_ref[...]) * scale).astype(BF16)
    for s in range(nsub):
        for h in range(heads):
            q = mq[s * sub:(s + 1) * sub, h * hd:(h + 1) * hd]
            k = mk_ref[s, :, h * hd:(h + 1) * hd]
            v = mv_ref[s, :, h * hd:(h + 1) * hd]
            sc = _dot_nt(q, k)
            p = jnp.exp(sc - jnp.max(sc, axis=-1, keepdims=True))
            p = p * (1.0 / jnp.sum(p, axis=-1, keepdims=True))
            o_scr[s * sub:(s + 1) * sub, h * hd:(h + 1) * hd] = _dot(p.astype(BF16), v).astype(BF16)
    xn = x + _dot(o_scr[...], wo_ref[...])
    xo_ref[...] = xn
    hn_ref[...] = _rms(xn, gn_ref[...]).astype(hn_ref.dtype)


def _mem_attn(x, g, w_mq, mem_k, mem_v, w_mo, g_next, *, sub, heads, name):
    m, d = x.shape
    nb, n_mem, width = mem_k.shape
    hd = width // heads
    tm = _tile(m, 256)
    if sub >= tm:
        nsub, rows = 1, tm
        per = sub // tm
        mmap = lambda i: (i // per, 0, 0)
    else:
        nsub, rows = tm // sub, sub
        mmap = lambda i: (i, 0, 0)
    row = lambda i: (i, 0)
    fix = lambda i: (0, 0)
    return pl.pallas_call(
        functools.partial(_mem_attn_kernel, nsub=nsub, sub=rows, heads=heads, hd=hd,
                          scale=hd ** -0.5),
        out_shape=[jax.ShapeDtypeStruct((m, d), F32), jax.ShapeDtypeStruct((m, d), BF16)],
        grid=(m // tm,),
        in_specs=[pl.BlockSpec((tm, d), row), pl.BlockSpec((1, d), fix),
                  pl.BlockSpec((d, width), fix),
                  pl.BlockSpec((nsub, n_mem, width), mmap),
                  pl.BlockSpec((nsub, n_mem, width), mmap),
                  pl.BlockSpec((width, d), fix), pl.BlockSpec((1, d), fix)],
        out_specs=[pl.BlockSpec((tm, d), row), pl.BlockSpec((tm, d), row)],
        scratch_shapes=[pltpu.VMEM((tm, width), BF16)],
        compiler_params=_cparams(("parallel",), 48),
        name=name,
    )(x, g.reshape(1, d), w_mq, mem_k, mem_v, w_mo, g_next.reshape(1, d))


def _rotate_half_cols(w):
    half = w.shape[-1] // 2
    return jnp.concatenate([-w[..., half:], w[..., :half]], axis=-1)


def _rope_table(pos, half):
    inv = ROPE_THETA ** (-jnp.arange(half, dtype=F32) / half)
    ang = pos.astype(F32)[:, None] * inv[None, :]
    c, s = jnp.cos(ang), jnp.sin(ang)
    return jnp.concatenate([c, c, s, s], axis=-1)


def _prepare_weights(w_in, w_uq, w_uk, w_uv, w_branch_a, w_branch_b, w_out, w_mq, w_mk, w_mv, w_mo,
                     w_gate, w_up, w_down, dims):
    q_lora, kv_lora, rope, heads, nope = dims
    o_kr = q_lora + kv_lora
    o_sb = o_kr + rope
    w_lat = jnp.concatenate([w_in[:, :o_sb], _rotate_half_cols(w_in[:, o_kr:o_sb])], axis=1)
    wq_cat = jnp.concatenate([w_uq, _rotate_half_cols(w_uq[..., nope:])], axis=-1)
    return dict(
        w_lat=w_lat.astype(BF16),
        w_rest=w_in[:, o_sb:].astype(BF16),
        wq_cat=wq_cat.reshape(q_lora, -1).astype(BF16),
        wuk_t=jnp.transpose(w_uk, (1, 2, 0)).reshape(heads * nope, kv_lora).astype(BF16),
        wuv=jnp.transpose(w_uv, (1, 0, 2)).astype(BF16),
        w_ba=w_branch_a.astype(BF16), w_bb=w_branch_b.astype(BF16), w_out=w_out.astype(BF16),
        w_mq=w_mq.astype(BF16), w_mk=w_mk.astype(BF16), w_mv=w_mv.astype(BF16),
        w_mo=w_mo.astype(BF16),
        w_gate=w_gate.astype(BF16), w_up=w_up.astype(BF16), w_down=w_down.astype(BF16),
    )


def _layer(x, pos, past, mem_k, mem_v, w, gains, b_gate, dims, *, batch, tag):
    g_mix, g_q_lat, g_kv_lat, g_xattn, g_ffn = gains
    q_lora, kv_lora, rope, heads, nope = dims
    m, d = x.shape
    t = m // batch
    sb_width = (w["w_rest"].shape[1] - 2 * d) // 3
    sb_heads = sb_width // LANE
    mla_scale = (nope + rope) ** -0.5 * LOG2E
    sb_scale = LANE ** -0.5 * LOG2E
    tm = _tile(m, 1024)
    tn = 512

    h = _rmsnorm(x, g_mix, BF16, f"{tag}_norm_mix")
    lat_w = w["w_lat"].shape[1]
    (p_lat,) = _fused_matmul(
        [h], [(0, w["w_lat"], 0)], [], [(F32, "tile")], lambda accs, ex: (accs[0],),
        n=lat_w, tm=_tile(m, 256), tn=lat_w, name=f"{tag}_proj_lat")
    cs = jnp.tile(_rope_table(pos, rope // 2), (batch, 1))
    cqn, ckv, ckv_b, krope, krope_b = _lat_post(p_lat, g_q_lat, g_kv_lat, cs, q_lora=q_lora,
                                                kv_lora=kv_lora, rope=rope, name=f"{tag}_lat_post")
    nsb = sb_width // tn
    (sbq,) = _fused_matmul(
        [h], [(0, w["w_rest"], 0)], [], [(BF16, "heads")],
        lambda accs, ex: (accs[0] * sb_scale,), n=sb_width, tm=tm, tn=tn, name=f"{tag}_proj_sbq")
    sbk, sbk_b = _fused_matmul(
        [h], [(0, w["w_rest"], nsb)], [], [(F32, "tile"), (BF16, "heads")],
        lambda accs, ex: (accs[0], accs[0]), n=sb_width, tm=tm, tn=tn, name=f"{tag}_proj_sbk")
    sbv, sbv_b = _fused_matmul(
        [h], [(0, w["w_rest"], 2 * nsb)], [], [(F32, "tile"), (BF16, "heads")],
        lambda accs, ex: (accs[0], accs[0]), n=sb_width, tm=tm, tn=tn, name=f"{tag}_proj_sbv")
    (gates,) = _fused_matmul(
        [h], [(0, w["w_rest"], 3 * nsb)], [(b_gate.reshape(1, -1), "row", 0)], [(BF16, "tile")],
        lambda accs, ex: (_sigmoid(accs[0] + ex[0]),), n=2 * d, tm=tm, tn=tn,
        name=f"{tag}_proj_gates")

    qlat, qrope = _mla_q(cqn, w["wq_cat"], w["wuk_t"], cs, heads=heads, nope=nope, rope=rope,
                         kv_lora=kv_lora, scale=mla_scale, name=f"{tag}_mla_q")
    tri_n = 256 if t % 256 == 0 else t
    tri = (jnp.arange(tri_n)[:, None] > jnp.arange(tri_n)[None, :]).astype(BF16)
    if past is None:
        o_a = _mla_prompt(qlat, qrope, ckv_b, krope_b, w["wuv"], batch=batch, seq=t,
                          name=f"{tag}_mla_attn")
        o_b = _sb_prompt(sbq, sbk_b, sbv_b, tri, batch=batch, seq=t, name=f"{tag}_sb_attn")
    else:
        c_ckv, c_kr, c_k, c_v = past
        o_a = _mla_sample(qlat, qrope, c_ckv, c_kr, ckv_b, krope_b, w["wuv"], name=f"{tag}_mla_attn")
        tri = (jnp.arange(256)[:, None] > jnp.arange(256)[None, :]).astype(BF16)
        o_b = _sb_sample(sbq, sbk_b, sbv_b, c_k, c_v, tri, name=f"{tag}_sb_attn")

    ng = d // tn
    (merged,) = _fused_matmul(
        [o_a, o_b], [(0, w["w_ba"], 0), (1, w["w_bb"], 0)],
        [(gates, "tile", 0), (gates, "tile", ng)], [(BF16, "tile")],
        lambda accs, ex: (ex[0].astype(F32) * accs[0] + ex[1].astype(F32) * accs[1],),
        n=d, tm=tm, tn=tn, name=f"{tag}_merge")
    (x,) = _fused_matmul(
        [merged], [(0, w["w_out"], 0)], [(x, "tile", 0)], [(F32, "tile")],
        lambda accs, ex: (ex[0] + accs[0],), n=d, tm=tm, tn=tn, name=f"{tag}_out_proj")

    mem_heads = mem_k.shape[2]
    mk = mem_k.reshape(mem_k.shape[0], mem_k.shape[1], -1).astype(BF16)
    mv = mem_v.reshape(mem_v.shape[0], mem_v.shape[1], -1).astype(BF16)
    x, hf = _mem_attn(x, g_xattn, w["w_mq"], mk, mv, w["w_mo"], g_ffn, sub=t, heads=mem_heads,
                      name=f"{tag}_mem_attn")

    d_ff = w["w_gate"].shape[1]
    tn_ff = _tile(d_ff, 256) if d_ff % 512 else 512
    (act,) = _fused_matmul(
        [hf], [(0, w["w_gate"], 0), (0, w["w_up"], 0)], [], [(BF16, "tile")],
        lambda accs, ex: (accs[0] * _sigmoid(accs[0]) * accs[1],), n=d_ff, tm=tm, tn=tn_ff,
        name=f"{tag}_ffn_up")
    (x,) = _fused_matmul(
        [act], [(0, w["w_down"], 0)], [(x, "tile", 0)], [(F32, "tile")],
        lambda accs, ex: (ex[0] + accs[0],), n=d, tm=_tile(m, 512), tn=256, name=f"{tag}_ffn_down")
    return x, (ckv, krope, sbk, sbv)


def kernel(x_prompt, x_sample, cache_mla_ckv, cache_mla_krope, cache_sb_k, cache_sb_v, cache_mem_k, cache_mem_v, mem_prompt, g_mix, w_in, b_gate, g_q_lat, w_uq, g_kv_lat, w_uk, w_uv, w_branch_a, w_branch_b, w_out, g_xattn, g_mem, w_mq, w_mk, w_mv, w_mo, g_ffn, w_gate, w_up, w_down, g_final):
    depth = w_in.shape[0]
    bp, seq, d = x_prompt.shape
    bs, dec, _ = x_sample.shape
    past_len = cache_mla_ckv.shape[2]
    q_lora, heads, qk = w_uq.shape[1:]
    kv_lora, _, nope = w_uk.shape[1:]
    rope = qk - nope
    dims = (q_lora, kv_lora, rope, heads, nope)
    sb_heads, sb_hd = cache_sb_k.shape[3:]
    n_mem, mem_heads, mem_hd = cache_mem_k.shape[2:]
    pos_p = jnp.arange(seq)
    pos_s = past_len + jnp.arange(dec)

    xp = x_prompt.reshape(bp * seq, d)
    xs = x_sample.reshape(bs * dec, d)
    outs = [[] for _ in range(10)]
    for l in range(depth):
        w = _prepare_weights(w_in[l], w_uq[l], w_uk[l], w_uv[l], w_branch_a[l], w_branch_b[l],
                             w_out[l], w_mq[l], w_mk[l], w_mv[l], w_mo[l], w_gate[l], w_up[l],
                             w_down[l], dims)
        gains = (g_mix[l], g_q_lat[l], g_kv_lat[l], g_xattn[l], g_ffn[l])
        mn = _rmsnorm(mem_prompt.reshape(bp * n_mem, d), g_mem[l], BF16, f"l{l}_norm_mem")
        mem_w = w["w_mk"].shape[1]
        mk, mv = _fused_matmul(
            [mn], [(0, w["w_mk"], 0), (0, w["w_mv"], 0)], [], [(F32, "tile"), (F32, "tile")],
            lambda accs, ex: (accs[0], accs[1]), n=mem_w, tm=_tile(bp * n_mem, 512),
            tn=_tile(mem_w, 512), name=f"l{l}_mem_kv")
        mk = mk.reshape(bp, n_mem, mem_heads, mem_hd)
        mv = mv.reshape(bp, n_mem, mem_heads, mem_hd)
        xp, (ckv, kr, k, v) = _layer(xp, pos_p, None, mk, mv, w, gains, b_gate[l], dims,
                                     batch=bp, tag=f"l{l}p")
        for lst, val in zip(outs[:6], (ckv.reshape(bp, seq, -1), kr.reshape(bp, seq, -1),
                                       k.reshape(bp, seq, sb_heads, sb_hd),
                                       v.reshape(bp, seq, sb_heads, sb_hd), mk, mv)):
            lst.append(val)
        past = (cache_mla_ckv[l], cache_mla_krope[l], cache_sb_k[l], cache_sb_v[l])
        xs, (ckv, kr, k, v) = _layer(xs, pos_s, past, cache_mem_k[l], cache_mem_v[l], w, gains,
                                     b_gate[l], dims, batch=bs, tag=f"l{l}s")
        for lst, val in zip(outs[6:], (ckv.reshape(bs, dec, -1), kr.reshape(bs, dec, -1),
                                       k.reshape(bs, dec, sb_heads, sb_hd),
                                       v.reshape(bs, dec, sb_heads, sb_hd))):
            lst.append(val)
    y_prompt = _rmsnorm(xp, g_final, F32, "final_norm_p").reshape(bp, seq, d)
    y_sample = _rmsnorm(xs, g_final, F32, "final_norm_s").reshape(bs, dec, d)
    return (y_prompt, y_sample) + tuple(jnp.stack(o) for o in outs)
```

```python
import functools
import math

import jax
import jax.numpy as jnp
from jax import lax
from jax.experimental import pallas as pl
from jax.experimental.pallas import tpu as pltpu

F32 = jnp.float32
BF16 = jnp.bfloat16

CHUNK = 64
EPS = 1e-6
ROPE_THETA = 10000.0
NEG_BIG = -1e30
LOG2E = math.log2(math.e)
MIB = 1024 * 1024
LANE = 128
_SB_GROUP = 4
_MLA_GROUP = 2
_MHA_GROUP = 2
_MLA_TQ = 256
_MLA_TK = 1024


def _cparams(sem, vmem_mib):
    return pltpu.CompilerParams(dimension_semantics=sem, vmem_limit_bytes=vmem_mib * MIB)


def _dot(a, b):
    return jnp.dot(a, b, preferred_element_type=F32)


def _dot_nt(a, b):
    return lax.dot_general(a, b, (((1,), (1,)), ((), ())), preferred_element_type=F32)


def _sigmoid(x):
    return 1.0 / (1.0 + jnp.exp(-x))


def _rms(x, g):
    return x * lax.rsqrt(jnp.mean(x * x, axis=-1, keepdims=True) + EPS) * g


def _tile(n, pref):
    if n <= pref:
        return n
    t = pref
    while n % t:
        t //= 2
    return t


def _norm_kernel(x_ref, g_ref, o_ref):
    o_ref[...] = _rms(x_ref[...], g_ref[...]).astype(o_ref.dtype)


def _rmsnorm(x, g, out_dtype, name):
    m, d = x.shape
    tm = _tile(m, 256)
    return pl.pallas_call(
        _norm_kernel,
        out_shape=jax.ShapeDtypeStruct((m, d), out_dtype),
        grid=(m // tm,),
        in_specs=[pl.BlockSpec((tm, d), lambda i: (i, 0)),
                  pl.BlockSpec((1, d), lambda i: (0, 0))],
        out_specs=pl.BlockSpec((tm, d), lambda i: (i, 0)),
        compiler_params=_cparams(("parallel",), 40),
        name=name,
    )(x, g.reshape(1, d))


def _fused_matmul(lhs, dots, extras, outs, epilogue, *, n, tm, tn, name, vmem_mib=48):
    m = lhs[0].shape[0]
    na, nd, ne = len(lhs), len(dots), len(extras)
    hpt = tn // LANE

    def kernel(*refs):
        a_refs, w_refs = refs[:na], refs[na:na + nd]
        e_refs = refs[na + nd:na + nd + ne]
        o_refs = refs[na + nd + ne:]
        accs = [_dot(a_refs[k][...], w[...]) for (k, _, _), w in zip(dots, w_refs)]
        vals = epilogue(accs, [e[...] for e in e_refs])
        for o_ref, v, (_, kind) in zip(o_refs, vals, outs):
            if kind == "tile":
                o_ref[...] = v.astype(o_ref.dtype)
            else:
                for hh in range(hpt):
                    o_ref[hh] = v[:, hh * LANE:(hh + 1) * LANE].astype(o_ref.dtype)

    in_specs, args = [], []
    for a in lhs:
        in_specs.append(pl.BlockSpec((tm, a.shape[1]), lambda i, j: (i, 0)))
        args.append(a)
    for _, w, off in dots:
        in_specs.append(pl.BlockSpec((w.shape[0], tn), lambda i, j, off=off: (0, j + off)))
        args.append(w)
    for e, kind, off in extras:
        if kind == "row":
            in_specs.append(pl.BlockSpec((1, tn), lambda i, j, off=off: (0, j + off)))
        else:
            in_specs.append(pl.BlockSpec((tm, tn), lambda i, j, off=off: (i, j + off)))
        args.append(e)
    out_shape, out_specs = [], []
    for dt, kind in outs:
        if kind == "tile":
            out_shape.append(jax.ShapeDtypeStruct((m, n), dt))
            out_specs.append(pl.BlockSpec((tm, tn), lambda i, j: (i, j)))
        else:
            out_shape.append(jax.ShapeDtypeStruct((n // LANE, m, LANE), dt))
            out_specs.append(pl.BlockSpec((hpt, tm, LANE), lambda i, j: (j, i, 0)))
    return pl.pallas_call(
        kernel,
        out_shape=out_shape,
        grid=(m // tm, n // tn),
        in_specs=in_specs,
        out_specs=out_specs,
        compiler_params=_cparams(("parallel", "arbitrary"), vmem_mib),
        name=name,
    )(*args)


def _lat_post_kernel(p_ref, gq_ref, gkv_ref, cs_ref, cqn_ref, ckv_ref, ckvb_ref, kr_ref, krb_ref,
                     *, q_lora, kv_lora, rope):
    p = p_ref[...]
    cqn_ref[...] = _rms(p[:, :q_lora], gq_ref[...]).astype(cqn_ref.dtype)
    ckv = _rms(p[:, q_lora:q_lora + kv_lora], gkv_ref[...])
    ckv_ref[...] = ckv
    ckvb_ref[...] = ckv.astype(ckvb_ref.dtype)
    t = p[:, q_lora + kv_lora:] * cs_ref[...]
    kr = (t + pltpu.roll(t, rope, axis=1))[:, :rope]
    kr_ref[...] = kr
    krb_ref[...] = kr.astype(krb_ref.dtype)


def _lat_post(p, g_q, g_kv, cs, *, q_lora, kv_lora, rope, name):
    m, w = p.shape
    tm = _tile(m, 512)
    row = lambda i: (i, 0)
    fix = lambda i: (0, 0)
    return pl.pallas_call(
        functools.partial(_lat_post_kernel, q_lora=q_lora, kv_lora=kv_lora, rope=rope),
        out_shape=[jax.ShapeDtypeStruct((m, q_lora), BF16),
                   jax.ShapeDtypeStruct((m, kv_lora), F32),
                   jax.ShapeDtypeStruct((m, kv_lora), BF16),
                   jax.ShapeDtypeStruct((m, rope), F32),
                   jax.ShapeDtypeStruct((m, rope), BF16)],
        grid=(m // tm,),
        in_specs=[pl.BlockSpec((tm, w), row), pl.BlockSpec((1, q_lora), fix),
                  pl.BlockSpec((1, kv_lora), fix), pl.BlockSpec((tm, 2 * rope), row)],
        out_specs=[pl.BlockSpec((tm, q_lora), row), pl.BlockSpec((tm, kv_lora), row),
                   pl.BlockSpec((tm, kv_lora), row), pl.BlockSpec((tm, rope), row),
                   pl.BlockSpec((tm, rope), row)],
        compiler_params=_cparams(("parallel",), 32),
        name=name,
    )(p, g_q.reshape(1, -1), g_kv.reshape(1, -1), cs)


def _mla_q_kernel(cqn_ref, wq_ref, wuk_ref, cs_ref, qlat_ref, qrope_ref, *, nope, rope, scale):
    qh = _dot(cqn_ref[...], wq_ref[...])
    qn = qh[:, :nope].astype(BF16)
    qlat_ref[0] = (_dot(qn, wuk_ref[...]) * scale).astype(qlat_ref.dtype)
    t = qh[:, nope:] * cs_ref[...]
    qr = (t + pltpu.roll(t, rope, axis=1))[:, :rope]
    qrope_ref[0] = (qr * scale).astype(qrope_ref.dtype)


def _mla_q(cqn, wq_cat, wuk_t, cs, *, heads, nope, rope, kv_lora, scale, name):
    m, q_lora = cqn.shape
    tm = _tile(m, 1024)
    hw = nope + 2 * rope
    return pl.pallas_call(
        functools.partial(_mla_q_kernel, nope=nope, rope=rope, scale=scale),
        out_shape=[jax.ShapeDtypeStruct((heads, m, kv_lora), BF16),
                   jax.ShapeDtypeStruct((heads, m, rope), BF16)],
        grid=(m // tm, heads),
        in_specs=[pl.BlockSpec((tm, q_lora), lambda i, h: (i, 0)),
                  pl.BlockSpec((q_lora, hw), lambda i, h: (0, h)),
                  pl.BlockSpec((nope, kv_lora), lambda i, h: (h, 0)),
                  pl.BlockSpec((tm, 2 * rope), lambda i, h: (i, 0))],
        out_specs=[pl.BlockSpec((1, tm, kv_lora), lambda i, h: (h, i, 0)),
                   pl.BlockSpec((1, tm, rope), lambda i, h: (h, i, 0))],
        compiler_params=_cparams(("parallel", "arbitrary"), 32),
        name=name,
    )(cqn, wq_cat, wuk_t, cs)


def _flash_step(qlat_ref, qrope_ref, k, kr, mask, m_ref, l_ref, acc_ref, *, heads, group, tq):
    rows = group * tq
    tk = k.shape[0]

    def scores(g0):
        q = qlat_ref[g0:g0 + group].reshape(rows, qlat_ref.shape[-1])
        qr = qrope_ref[g0:g0 + group].reshape(rows, qrope_ref.shape[-1])
        return _dot_nt(q, k) + _dot_nt(qr, kr)

    s_next = scores(0)
    for g0 in range(0, heads, group):
        s = s_next
        if g0 + group < heads:
            s_next = scores(g0 + group)
        if mask is not None:
            s = jnp.where(mask[None], s.reshape(group, tq, tk), NEG_BIG).reshape(rows, tk)
        sl = slice(g0 * tq, g0 * tq + rows)
        m_prev = m_ref[sl]
        m_new = jnp.maximum(m_prev, jnp.max(s, axis=-1, keepdims=True))
        p = jnp.exp2(s - m_new)
        alpha = jnp.exp2(m_prev - m_new)
        l_ref[sl] = alpha * l_ref[sl] + jnp.sum(p, axis=-1, keepdims=True)
        acc_ref[sl] = alpha * acc_ref[sl] + _dot(p.astype(BF16), k)
        m_ref[sl] = m_new


def _flash_init(m_ref, l_ref, acc_ref):
    m_ref[...] = jnp.full(m_ref.shape, NEG_BIG, F32)
    l_ref[...] = jnp.zeros(l_ref.shape, F32)
    acc_ref[...] = jnp.zeros(acc_ref.shape, F32)


def _flash_finish(wuv_ref, o_ref, l_ref, acc_ref, heads, vh):
    tq = acc_ref.shape[0] // heads
    o = (acc_ref[...] * (1.0 / l_ref[...])).astype(BF16)
    for h in range(heads):
        o_ref[:, h * vh:(h + 1) * vh] = _dot(o[h * tq:(h + 1) * tq], wuv_ref[h]).astype(o_ref.dtype)


def _chunk_mask(q0, k0, tq, tk):
    qc = (q0 + lax.broadcasted_iota(jnp.int32, (tq, tk), 0)) // CHUNK
    kc = (k0 + lax.broadcasted_iota(jnp.int32, (tq, tk), 1)) // CHUNK
    return kc <= qc


def _mla_sample_kernel(qlat_ref, qrope_ref, cckv_ref, ckr_ref, nckv_ref, nkr_ref, wuv_ref, o_ref,
                       m_ref, l_ref, acc_ref, *, heads, group, tq, nkc, past, vh):
    ki = pl.program_id(1)

    @pl.when(ki == 0)
    def _():
        _flash_init(m_ref, l_ref, acc_ref)

    step = functools.partial(_flash_step, qlat_ref, qrope_ref, m_ref=m_ref, l_ref=l_ref,
                             acc_ref=acc_ref, heads=heads, group=group, tq=tq)

    @pl.when(ki < nkc)
    def _():
        step(cckv_ref[0].astype(BF16), ckr_ref[0].astype(BF16), None)

    @pl.when(ki == nkc)
    def _():
        step(nckv_ref[...], nkr_ref[...], _chunk_mask(past, past, tq, tq))
        _flash_finish(wuv_ref, o_ref, l_ref, acc_ref, heads, vh)


def _mla_sample(qlat, qrope, cache_ckv, cache_kr, ckv, krope, wuv, *, name):
    heads, _, c = qlat.shape
    rope = qrope.shape[-1]
    vh = wuv.shape[-1]
    batch, past, _ = cache_ckv.shape
    tq = ckv.shape[0] // batch
    tk = _tile(past, 512)
    nkc = past // tk
    cmap = lambda b, ki: (b, jnp.minimum(ki, nkc - 1), 0)
    return pl.pallas_call(
        functools.partial(_mla_sample_kernel, heads=heads, group=_MLA_GROUP, tq=tq, nkc=nkc,
                          past=past, vh=vh),
        out_shape=jax.ShapeDtypeStruct((batch * tq, heads * vh), BF16),
        grid=(batch, nkc + 1),
        in_specs=[pl.BlockSpec((heads, tq, c), lambda b, ki: (0, b, 0)),
                  pl.BlockSpec((heads, tq, rope), lambda b, ki: (0, b, 0)),
                  pl.BlockSpec((1, tk, c), cmap),
                  pl.BlockSpec((1, tk, rope), cmap),
                  pl.BlockSpec((tq, c), lambda b, ki: (b, 0)),
                  pl.BlockSpec((tq, rope), lambda b, ki: (b, 0)),
                  pl.BlockSpec((heads, c, vh), lambda b, ki: (0, 0, 0))],
        out_specs=pl.BlockSpec((tq, heads * vh), lambda b, ki: (b, 0)),
        scratch_shapes=[pltpu.VMEM((heads * tq, 1), F32), pltpu.VMEM((heads * tq, 1), F32),
                        pltpu.VMEM((heads * tq, c), F32)],
        compiler_params=_cparams(("parallel", "arbitrary"), 48),
        name=name,
    )(qlat, qrope, cache_ckv, cache_kr, ckv, krope, wuv)


def _mla_kv_up_kernel(ckv_ref, kr_ref, wuk_ref, wuv_ref, kcat_ref, v_ref, *, hpt, nope, vh):
    c = ckv_ref[...]
    k = _dot(c, wuk_ref[...])
    v = _dot(c, wuv_ref[...])
    kr = kr_ref[...]
    for hh in range(hpt):
        kcat_ref[hh, :, :nope] = k[:, hh * nope:(hh + 1) * nope].astype(kcat_ref.dtype)
        kcat_ref[hh, :, nope:] = kr
        v_ref[hh] = v[:, hh * vh:(hh + 1) * vh].astype(v_ref.dtype)


def _mla_kv_up(ckv_b, krope_b, wuk_flat, wuv_flat, *, heads, name):
    m, c = ckv_b.shape
    rope = krope_b.shape[1]
    nope, vh = wuk_flat.shape[1] // heads, wuv_flat.shape[1] // heads
    hpt = _tile(heads, 4)
    tm = _tile(m, 1024)
    return pl.pallas_call(
        functools.partial(_mla_kv_up_kernel, hpt=hpt, nope=nope, vh=vh),
        out_shape=[jax.ShapeDtypeStruct((heads, m, nope + rope), BF16),
                   jax.ShapeDtypeStruct((heads, m, vh), BF16)],
        grid=(m // tm, heads // hpt),
        in_specs=[pl.BlockSpec((tm, c), lambda i, j: (i, 0)),
                  pl.BlockSpec((tm, rope), lambda i, j: (i, 0)),
                  pl.BlockSpec((c, hpt * nope), lambda i, j: (0, j)),
                  pl.BlockSpec((c, hpt * vh), lambda i, j: (0, j))],
        out_specs=[pl.BlockSpec((hpt, tm, nope + rope), lambda i, j: (j, i, 0)),
                   pl.BlockSpec((hpt, tm, vh), lambda i, j: (j, i, 0))],
        compiler_params=_cparams(("parallel", "arbitrary"), 32),
        name=name,
    )(ckv_b, krope_b, wuk_flat, wuv_flat)


def _mla_qcat_kernel(cqn_ref, wq_ref, cs_ref, qcat_ref, *, hpt, nope, rope, scale):
    qh = _dot(cqn_ref[...], wq_ref[...])
    hw = nope + 2 * rope
    cs = cs_ref[...]
    for hh in range(hpt):
        qcat_ref[hh, :, :nope] = (qh[:, hh * hw:hh * hw + nope] * scale).astype(qcat_ref.dtype)
        t = qh[:, hh * hw + nope:(hh + 1) * hw] * cs
        qr = (t + pltpu.roll(t, rope, axis=1))[:, :rope]
        qcat_ref[hh, :, nope:] = (qr * scale).astype(qcat_ref.dtype)


def _mla_qcat(cqn, wq_cat, cs, *, heads, nope, rope, scale, name):
    m, q_lora = cqn.shape
    hw = nope + 2 * rope
    hpt = _tile(heads, 4)
    tm = _tile(m, 1024)
    return pl.pallas_call(
        functools.partial(_mla_qcat_kernel, hpt=hpt, nope=nope, rope=rope, scale=scale),
        out_shape=jax.ShapeDtypeStruct((heads, m, nope + rope), BF16),
        grid=(m // tm, heads // hpt),
        in_specs=[pl.BlockSpec((tm, q_lora), lambda i, j: (i, 0)),
                  pl.BlockSpec((q_lora, hpt * hw), lambda i, j: (0, j)),
                  pl.BlockSpec((tm, 2 * rope), lambda i, j: (i, 0))],
        out_specs=pl.BlockSpec((hpt, tm, nope + rope), lambda i, j: (j, i, 0)),
        compiler_params=_cparams(("parallel", "arbitrary"), 32),
        name=name,
    )(cqn, wq_cat, cs)


def _mha_step(q_ref, k_ref, v_ref, mask, m_ref, l_ref, acc_ref, *, heads, group, tq):
    rows = group * tq
    tk = k_ref.shape[1]

    def scores(g0):
        return jnp.concatenate([_dot_nt(q_ref[g0 + i], k_ref[g0 + i]) for i in range(group)], axis=0)

    s_next = scores(0)
    for g0 in range(0, heads, group):
        s = s_next
        if g0 + group < heads:
            s_next = scores(g0 + group)
        if mask is not None:
            s = jnp.where(mask[None], s.reshape(group, tq, tk), NEG_BIG).reshape(rows, tk)
        sl = slice(g0 * tq, g0 * tq + rows)
        m_prev = m_ref[sl]
        m_new = jnp.maximum(m_prev, jnp.max(s, axis=-1, keepdims=True))
        p = jnp.exp2(s - m_new)
        alpha = jnp.exp2(m_prev - m_new)
        l_ref[sl] = alpha * l_ref[sl] + jnp.sum(p, axis=-1, keepdims=True)
        p = p.astype(BF16)
        pv = jnp.concatenate([_dot(p[i * tq:(i + 1) * tq], v_ref[g0 + i]) for i in range(group)], axis=0)
        acc_ref[sl] = alpha * acc_ref[sl] + pv
        m_ref[sl] = m_new


def _mha_prompt_kernel(q_ref, k_ref, v_ref, o_ref, m_ref, l_ref, acc_ref, *, heads, group, tq, tk, vh):
    qi, ki = pl.program_id(1), pl.program_id(2)
    k_last = ((qi + 1) * tq - 1) // tk
    partial = (ki + 1) * tk > qi * tq + CHUNK
    step = functools.partial(_mha_step, q_ref, k_ref, v_ref, m_ref=m_ref, l_ref=l_ref,
                             acc_ref=acc_ref, heads=heads, group=group, tq=tq)

    @pl.when(ki == 0)
    def _():
        _flash_init(m_ref, l_ref, acc_ref)

    @pl.when((ki <= k_last) & jnp.logical_not(partial))
    def _():
        step(None)

    @pl.when((ki <= k_last) & partial)
    def _():
        step(_chunk_mask(qi * tq, ki * tk, tq, tk))

    @pl.when(ki == k_last)
    def _():
        o = acc_ref[...] * (1.0 / l_ref[...])
        for h in range(heads):
            o_ref[:, h * vh:(h + 1) * vh] = o[h * tq:(h + 1) * tq].astype(o_ref.dtype)


def _mha_prompt(qcat, kcat, v, *, batch, seq, name):
    heads, _, dk = qcat.shape
    vh = v.shape[-1]
    tq = _tile(seq, _MLA_TQ)
    tk = _tile(seq, _MLA_TK)
    nq, nk = seq // tq, seq // tk

    def kmap(b, qi, ki):
        return (0, b * nk + jnp.minimum(ki, ((qi + 1) * tq - 1) // tk), 0)

    return pl.pallas_call(
        functools.partial(_mha_prompt_kernel, heads=heads, group=_MHA_GROUP, tq=tq, tk=tk, vh=vh),
        out_shape=jax.ShapeDtypeStruct((batch * seq, heads * vh), BF16),
        grid=(batch, nq, nk),
        in_specs=[pl.BlockSpec((heads, tq, dk), lambda b, qi, ki: (0, b * nq + qi, 0)),
                  pl.BlockSpec((heads, tk, dk), kmap),
                  pl.BlockSpec((heads, tk, vh), kmap)],
        out_specs=pl.BlockSpec((tq, heads * vh), lambda b, qi, ki: (b * nq + qi, 0)),
        scratch_shapes=[pltpu.VMEM((heads * tq, 1), F32), pltpu.VMEM((heads * tq, 1), F32),
                        pltpu.VMEM((heads * tq, vh), F32)],
        compiler_params=_cparams(("parallel", "parallel", "arbitrary"), 56),
        name=name,
    )(qcat, kcat, v)


SB_SKIP = -160.0


def _sb_block(q_ref, get_kv, tri, valid, run_ref, acc_ref, *, heads, group, tq):
    rows = group * tq
    for g0 in range(0, heads, group):
        kv = [get_kv(h) for h in range(g0, g0 + group)]
        z = jnp.concatenate([_dot_nt(q_ref[g0 + i], kv[i][0]) for i in range(group)], axis=0)
        tk = z.shape[-1]
        sp = jnp.log2(1.0 + jnp.exp2(-jnp.abs(z)))
        lk = -(jnp.maximum(z, 0.0) + sp)
        if valid is not None:
            lk = jnp.where(valid[None], lk.reshape(group, tq, tk), 0.0).reshape(rows, tk)
        hi = lk.astype(BF16)
        lo = (lk - hi.astype(F32)).astype(BF16)
        suffix = _dot(hi, tri) + _dot(lo, tri)
        run = run_ref[g0 * tq:g0 * tq + rows]
        w = jnp.exp2((jnp.minimum(z, 0.0) - sp) + suffix + run)
        if valid is not None:
            w = jnp.where(valid[None], w.reshape(group, tq, tk), 0.0).reshape(rows, tk)
        w = w.astype(BF16)
        run_ref[g0 * tq:g0 * tq + rows] = run + suffix[:, :1] + lk[:, :1]
        for i in range(group):
            r0 = (g0 + i) * tq
            acc_ref[r0:r0 + tq] = acc_ref[r0:r0 + tq] + _dot(w[i * tq:(i + 1) * tq], kv[i][1])


def _strict_lower(n):
    row = lax.broadcasted_iota(jnp.int32, (n, n), 0)
    col = lax.broadcasted_iota(jnp.int32, (n, n), 1)
    return col < row


def _all_below(run_ref, bound):
    m = jnp.max(run_ref[...], axis=0, keepdims=True)
    return m[0, 0] <= bound


def _sb_write(o_ref, acc_ref, heads, tq, hd):
    for h in range(heads):
        o_ref[:, h * hd:(h + 1) * hd] = acc_ref[h * tq:(h + 1) * tq].astype(o_ref.dtype)


def _sb_prompt_kernel(q_ref, kd_ref, vd_ref, tri_ref, k_hbm, v_hbm, o_ref, kbuf, vbuf, sem,
                      run_ref, acc_ref, *, heads, group, tq, hd, nq):
    b, qi = pl.program_id(0), pl.program_id(1)

    def copies(j, slot):
        row0 = pl.multiple_of((b * nq + j) * tq, tq)
        return (pltpu.make_async_copy(k_hbm.at[:, pl.ds(row0, tq), :], kbuf.at[slot], sem.at[0, slot]),
                pltpu.make_async_copy(v_hbm.at[:, pl.ds(row0, tq), :], vbuf.at[slot], sem.at[1, slot]))

    @pl.when(qi > 0)
    def _():
        for cp in copies(qi - 1, 0):
            cp.start()

    run_ref[...] = jnp.zeros(run_ref.shape, F32)
    acc_ref[...] = jnp.zeros(acc_ref.shape, F32)
    tri = tri_ref[...]
    blk = functools.partial(_sb_block, run_ref=run_ref, acc_ref=acc_ref, heads=heads, group=group,
                            tq=tq)
    blk(q_ref, lambda h: (kd_ref[h], vd_ref[h]), tri, _strict_lower(tq))

    def cond(c):
        return (c[0] >= 0) & (c[1] > 0)

    def body(c):
        j = c[0]
        slot = lax.rem(qi - 1 - j, 2)
        for cp in copies(j, slot):
            cp.wait()

        @pl.when(j > 0)
        def _():
            for cp in copies(j - 1, 1 - slot):
                cp.start()

        blk(q_ref, lambda h: (kbuf[slot, h], vbuf[slot, h]), tri, None)
        go = jnp.where(_all_below(run_ref, SB_SKIP), 0, 1).astype(jnp.int32)
        return (j - 1, go)

    j_end, _ = lax.while_loop(cond, body, (qi - 1, jnp.int32(1)))

    @pl.when(j_end >= 0)
    def _():
        for cp in copies(j_end, lax.rem(qi - 1 - j_end, 2)):
            cp.wait()

    _sb_write(o_ref, acc_ref, heads, tq, hd)


def _sb_prompt(q, k, v, tri, *, batch, seq, name):
    heads, _, hd = q.shape
    tq = tri.shape[0]
    nq = seq // tq
    blk = lambda b, qi: (0, b * nq + qi, 0)
    return pl.pallas_call(
        functools.partial(_sb_prompt_kernel, heads=heads, group=_SB_GROUP, tq=tq, hd=hd, nq=nq),
        out_shape=jax.ShapeDtypeStruct((batch * seq, heads * hd), BF16),
        grid=(batch, nq),
        in_specs=[pl.BlockSpec((heads, tq, hd), blk), pl.BlockSpec((heads, tq, hd), blk),
                  pl.BlockSpec((heads, tq, hd), blk),
                  pl.BlockSpec((tq, tq), lambda b, qi: (0, 0)),
                  pl.BlockSpec(memory_space=pl.ANY), pl.BlockSpec(memory_space=pl.ANY)],
        out_specs=pl.BlockSpec((tq, heads * hd), lambda b, qi: (b * nq + qi, 0)),
        scratch_shapes=[pltpu.VMEM((2, heads, tq, hd), BF16), pltpu.VMEM((2, heads, tq, hd), BF16),
                        pltpu.SemaphoreType.DMA((2, 2)),
                        pltpu.VMEM((heads * tq, 1), F32), pltpu.VMEM((heads * tq, hd), F32)],
        compiler_params=_cparams(("arbitrary", "arbitrary"), 40),
        name=name,
    )(q, k, v, tri, k, v)


def _sb_sample_kernel(q_ref, nk_ref, nv_ref, tri_ref, ck_hbm, cv_hbm, o_ref, kbuf, vbuf, sem,
                      run_ref, acc_ref, *, heads, group, tq, tk, hd, nkc):
    b = pl.program_id(0)

    def copies(j, slot):
        p0 = pl.multiple_of(j * tk, tk)
        return (pltpu.make_async_copy(ck_hbm.at[b, pl.ds(p0, tk)], kbuf.at[slot], sem.at[0, slot]),
                pltpu.make_async_copy(cv_hbm.at[b, pl.ds(p0, tk)], vbuf.at[slot], sem.at[1, slot]))

    for cp in copies(nkc - 1, 0):
        cp.start()

    run_ref[...] = jnp.zeros(run_ref.shape, F32)
    acc_ref[...] = jnp.zeros(acc_ref.shape, F32)
    blk = functools.partial(_sb_block, run_ref=run_ref, acc_ref=acc_ref, heads=heads, group=group,
                            tq=tq)
    blk(q_ref, lambda h: (nk_ref[h], nv_ref[h]), tri_ref[:tq, :tq], _strict_lower(tq))
    tri = tri_ref[...]

    def cond(c):
        return (c[0] >= 0) & (c[1] > 0)

    def body(c):
        j = c[0]
        slot = lax.rem(nkc - 1 - j, 2)
        for cp in copies(j, slot):
            cp.wait()

        @pl.when(j > 0)
        def _():
            for cp in copies(j - 1, 1 - slot):
                cp.start()

        blk(q_ref, lambda h: (kbuf[slot, :, h, :].astype(BF16), vbuf[slot, :, h, :].astype(BF16)),
            tri, None)
        go = jnp.where(_all_below(run_ref, SB_SKIP), 0, 1).astype(jnp.int32)
        return (j - 1, go)

    j_end, _ = lax.while_loop(cond, body, (jnp.int32(nkc - 1), jnp.int32(1)))

    @pl.when(j_end >= 0)
    def _():
        for cp in copies(j_end, lax.rem(nkc - 1 - j_end, 2)):
            cp.wait()

    _sb_write(o_ref, acc_ref, heads, tq, hd)


def _sb_sample(q, k_new, v_new, cache_k, cache_v, tri, *, name):
    heads, rows, hd = q.shape
    batch, past = cache_k.shape[:2]
    tq = rows // batch
    tk = tri.shape[0]
    nkc = past // tk
    new = lambda b: (0, b, 0)
    return pl.pallas_call(
        functools.partial(_sb_sample_kernel, heads=heads, group=_SB_GROUP, tq=tq, tk=tk, hd=hd,
                          nkc=nkc),
        out_shape=jax.ShapeDtypeStruct((rows, heads * hd), BF16),
        grid=(batch,),
        in_specs=[pl.BlockSpec((heads, tq, hd), new), pl.BlockSpec((heads, tq, hd), new),
                  pl.BlockSpec((heads, tq, hd), new),
                  pl.BlockSpec((tk, tk), lambda b: (0, 0)),
                  pl.BlockSpec(memory_space=pl.ANY), pl.BlockSpec(memory_space=pl.ANY)],
        out_specs=pl.BlockSpec((tq, heads * hd), lambda b: (b, 0)),
        scratch_shapes=[pltpu.VMEM((2, tk, heads, hd), F32), pltpu.VMEM((2, tk, heads, hd), F32),
                        pltpu.SemaphoreType.DMA((2, 2)),
                        pltpu.VMEM((heads * tq, 1), F32), pltpu.VMEM((heads * tq, hd), F32)],
        compiler_params=_cparams(("arbitrary",), 40),
        name=name,
    )(q, k_new, v_new, tri, cache_k, cache_v)


def _mem_attn_kernel(x_ref, g_ref, wq_ref, mk_ref, mv_ref, wo_ref, gn_ref, xo_ref, hn_ref, o_scr,
                     *, nsub, sub, heads, hd, scale):
    x = x_ref[...]
    mq = (_dot(_rms(x, g_ref[...]).astype(BF16), wq_ref[...]) * scale).astype(BF16)
    for s in range(nsub):
        for h in range(heads):
            q = mq[s * sub:(s + 1) * sub, h * hd:(h + 1) * hd]
            k = mk_ref[s, :, h * hd:(h + 1) * hd]
            v = mv_ref[s, :, h * hd:(h + 1) * hd]
            sc = _dot_nt(q, k)
            p = jnp.exp(sc - jnp.max(sc, axis=-1, keepdims=True))
            p = p * (1.0 / jnp.sum(p, axis=-1, keepdims=True))
            o_scr[s * sub:(s + 1) * sub, h * hd:(h + 1) * hd] = _dot(p.astype(BF16), v).astype(BF16)
    xn = x + _dot(o_scr[...], wo_ref[...])
    xo_ref[...] = xn
    hn_ref[...] = _rms(xn, gn_ref[...]).astype(hn_ref.dtype)


def _mem_attn(x, g, w_mq, mem_k, mem_v, w_mo, g_next, *, sub, heads, name):
    m, d = x.shape
    nb, n_mem, width = mem_k.shape
    hd = width // heads
    tm = _tile(m, 256)
    if sub >= tm:
        nsub, rows = 1, tm
        per = sub // tm
        mmap = lambda i: (i // per, 0, 0)
    else:
        nsub, rows = tm // sub, sub
        mmap = lambda i: (i, 0, 0)
    row = lambda i: (i, 0)
    fix = lambda i: (0, 0)
    return pl.pallas_call(
        functools.partial(_mem_attn_kernel, nsub=nsub, sub=rows, heads=heads, hd=hd,
                          scale=hd ** -0.5),
        out_shape=[jax.ShapeDtypeStruct((m, d), F32), jax.ShapeDtypeStruct((m, d), BF16)],
        grid=(m // tm,),
        in_specs=[pl.BlockSpec((tm, d), row), pl.BlockSpec((1, d), fix),
                  pl.BlockSpec((d, width), fix),
                  pl.BlockSpec((nsub, n_mem, width), mmap),
                  pl.BlockSpec((nsub, n_mem, width), mmap),
                  pl.BlockSpec((width, d), fix), pl.BlockSpec((1, d), fix)],
        out_specs=[pl.BlockSpec((tm, d), row), pl.BlockSpec((tm, d), row)],
        scratch_shapes=[pltpu.VMEM((tm, width), BF16)],
        compiler_params=_cparams(("parallel",), 48),
        name=name,
    )(x, g.reshape(1, d), w_mq, mem_k, mem_v, w_mo, g_next.reshape(1, d))


def _rotate_half_cols(w):
    half = w.shape[-1] // 2
    return jnp.concatenate([-w[..., half:], w[..., :half]], axis=-1)


def _rope_table(pos, half):
    inv = ROPE_THETA ** (-jnp.arange(half, dtype=F32) / half)
    ang = pos.astype(F32)[:, None] * inv[None, :]
    c, s = jnp.cos(ang), jnp.sin(ang)
    return jnp.concatenate([c, c, s, s], axis=-1)


def _prepare_weights(w_in, w_uq, w_uk, w_uv, w_branch_a, w_branch_b, w_out, w_mq, w_mk, w_mv, w_mo,
                     w_gate, w_up, w_down, dims):
    q_lora, kv_lora, rope, heads, nope = dims
    o_kr = q_lora + kv_lora
    o_sb = o_kr + rope
    w_lat = jnp.concatenate([w_in[:, :o_sb], _rotate_half_cols(w_in[:, o_kr:o_sb])], axis=1)
    wq_cat = jnp.concatenate([w_uq, _rotate_half_cols(w_uq[..., nope:])], axis=-1)
    return dict(
        w_lat=w_lat.astype(BF16),
        w_rest=w_in[:, o_sb:].astype(BF16),
        wq_cat=wq_cat.reshape(q_lora, -1).astype(BF16),
        wuk_t=jnp.transpose(w_uk, (1, 2, 0)).reshape(heads * nope, kv_lora).astype(BF16),
        wuv=jnp.transpose(w_uv, (1, 0, 2)).astype(BF16),
        wuk_flat=w_uk.reshape(kv_lora, -1).astype(BF16),
        wuv_flat=w_uv.reshape(kv_lora, -1).astype(BF16),
        w_ba=w_branch_a.astype(BF16), w_bb=w_branch_b.astype(BF16), w_out=w_out.astype(BF16),
        w_mq=w_mq.astype(BF16), w_mk=w_mk.astype(BF16), w_mv=w_mv.astype(BF16),
        w_mo=w_mo.astype(BF16),
        w_gate=w_gate.astype(BF16), w_up=w_up.astype(BF16), w_down=w_down.astype(BF16),
    )


def _layer(x, pos, past, mem_k, mem_v, w, gains, b_gate, dims, *, batch, tag):
    g_mix, g_q_lat, g_kv_lat, g_xattn, g_ffn = gains
    q_lora, kv_lora, rope, heads, nope = dims
    m, d = x.shape
    t = m // batch
    sb_width = (w["w_rest"].shape[1] - 2 * d) // 3
    sb_heads = sb_width // LANE
    mla_scale = (nope + rope) ** -0.5 * LOG2E
    sb_scale = LANE ** -0.5 * LOG2E
    tm = _tile(m, 1024)
    tn = 512

    h = _rmsnorm(x, g_mix, BF16, f"{tag}_norm_mix")
    lat_w = w["w_lat"].shape[1]
    (p_lat,) = _fused_matmul(
        [h], [(0, w["w_lat"], 0)], [], [(F32, "tile")], lambda accs, ex: (accs[0],),
        n=lat_w, tm=_tile(m, 256), tn=lat_w, name=f"{tag}_proj_lat")
    cs = jnp.tile(_rope_table(pos, rope // 2), (batch, 1))
    cqn, ckv, ckv_b, krope, krope_b = _lat_post(p_lat, g_q_lat, g_kv_lat, cs, q_lora=q_lora,
                                                kv_lora=kv_lora, rope=rope, name=f"{tag}_lat_post")
    nsb = sb_width // tn
    (sbq,) = _fused_matmul(
        [h], [(0, w["w_rest"], 0)], [], [(BF16, "heads")],
        lambda accs, ex: (accs[0] * sb_scale,), n=sb_width, tm=tm, tn=tn, name=f"{tag}_proj_sbq")
    sbk, sbk_b = _fused_matmul(
        [h], [(0, w["w_rest"], nsb)], [], [(F32, "tile"), (BF16, "heads")],
        lambda accs, ex: (accs[0], accs[0]), n=sb_width, tm=tm, tn=tn, name=f"{tag}_proj_sbk")
    sbv, sbv_b = _fused_matmul(
        [h], [(0, w["w_rest"], 2 * nsb)], [], [(F32, "tile"), (BF16, "heads")],
        lambda accs, ex: (accs[0], accs[0]), n=sb_width, tm=tm, tn=tn, name=f"{tag}_proj_sbv")
    (gates,) = _fused_matmul(
        [h], [(0, w["w_rest"], 3 * nsb)], [(b_gate.reshape(1, -1), "row", 0)], [(BF16, "tile")],
        lambda accs, ex: (_sigmoid(accs[0] + ex[0]),), n=2 * d, tm=tm, tn=tn,
        name=f"{tag}_proj_gates")

    tri_n = 256 if t % 256 == 0 else t
    tri = (jnp.arange(tri_n)[:, None] > jnp.arange(tri_n)[None, :]).astype(BF16)
    if past is None:
        qcat = _mla_qcat(cqn, w["wq_cat"], cs, heads=heads, nope=nope, rope=rope, scale=mla_scale,
                         name=f"{tag}_mla_q")
        kcat, vmla = _mla_kv_up(ckv_b, krope_b, w["wuk_flat"], w["wuv_flat"], heads=heads,
                                name=f"{tag}_mla_kv")
        o_a = _mha_prompt(qcat, kcat, vmla, batch=batch, seq=t, name=f"{tag}_mla_attn")
        o_b = _sb_prompt(sbq, sbk_b, sbv_b, tri, batch=batch, seq=t, name=f"{tag}_sb_attn")
    else:
        c_ckv, c_kr, c_k, c_v = past
        qlat, qrope = _mla_q(cqn, w["wq_cat"], w["wuk_t"], cs, heads=heads, nope=nope, rope=rope,
                             kv_lora=kv_lora, scale=mla_scale, name=f"{tag}_mla_q")
        o_a = _mla_sample(qlat, qrope, c_ckv, c_kr, ckv_b, krope_b, w["wuv"], name=f"{tag}_mla_attn")
        tri = (jnp.arange(256)[:, None] > jnp.arange(256)[None, :]).astype(BF16)
        o_b = _sb_sample(sbq, sbk_b, sbv_b, c_k, c_v, tri, name=f"{tag}_sb_attn")

    ng = d // tn
    (merged,) = _fused_matmul(
        [o_a, o_b], [(0, w["w_ba"], 0), (1, w["w_bb"], 0)],
        [(gates, "tile", 0), (gates, "tile", ng)], [(BF16, "tile")],
        lambda accs, ex: (ex[0].astype(F32) * accs[0] + ex[1].astype(F32) * accs[1],),
        n=d, tm=tm, tn=tn, name=f"{tag}_merge")
    (x,) = _fused_matmul(
        [merged], [(0, w["w_out"], 0)], [(x, "tile", 0)], [(F32, "tile")],
        lambda accs, ex: (ex[0] + accs[0],), n=d, tm=tm, tn=tn, name=f"{tag}_out_proj")

    mem_heads = mem_k.shape[2]
    mk = mem_k.reshape(mem_k.shape[0], mem_k.shape[1], -1).astype(BF16)
    mv = mem_v.reshape(mem_v.shape[0], mem_v.shape[1], -1).astype(BF16)
    x, hf = _mem_attn(x, g_xattn, w["w_mq"], mk, mv, w["w_mo"], g_ffn, sub=t, heads=mem_heads,
                      name=f"{tag}_mem_attn")

    d_ff = w["w_gate"].shape[1]
    tn_ff = _tile(d_ff, 256) if d_ff % 512 else 512
    (act,) = _fused_matmul(
        [hf], [(0, w["w_gate"], 0), (0, w["w_up"], 0)], [], [(BF16, "tile")],
        lambda accs, ex: (accs[0] * _sigmoid(accs[0]) * accs[1],), n=d_ff, tm=tm, tn=tn_ff,
        name=f"{tag}_ffn_up")
    (x,) = _fused_matmul(
        [act], [(0, w["w_down"], 0)], [(x, "tile", 0)], [(F32, "tile")],
        lambda accs, ex: (ex[0] + accs[0],), n=d, tm=_tile(m, 512), tn=256, name=f"{tag}_ffn_down")
    return x, (ckv, krope, sbk, sbv)


def kernel(x_prompt, x_sample, cache_mla_ckv, cache_mla_krope, cache_sb_k, cache_sb_v, cache_mem_k, cache_mem_v, mem_prompt, g_mix, w_in, b_gate, g_q_lat, w_uq, g_kv_lat, w_uk, w_uv, w_branch_a, w_branch_b, w_out, g_xattn, g_mem, w_mq, w_mk, w_mv, w_mo, g_ffn, w_gate, w_up, w_down, g_final):
    depth = w_in.shape[0]
    bp, seq, d = x_prompt.shape
    bs, dec, _ = x_sample.shape
    past_len = cache_mla_ckv.shape[2]
    q_lora, heads, qk = w_uq.shape[1:]
    kv_lora, _, nope = w_uk.shape[1:]
    rope = qk - nope
    dims = (q_lora, kv_lora, rope, heads, nope)
    sb_heads, sb_hd = cache_sb_k.shape[3:]
    n_mem, mem_heads, mem_hd = cache_mem_k.shape[2:]
    pos_p = jnp.arange(seq)
    pos_s = past_len + jnp.arange(dec)

    xp = x_prompt.reshape(bp * seq, d)
    xs = x_sample.reshape(bs * dec, d)
    outs = [[] for _ in range(10)]
    for l in range(depth):
        w = _prepare_weights(w_in[l], w_uq[l], w_uk[l], w_uv[l], w_branch_a[l], w_branch_b[l],
                             w_out[l], w_mq[l], w_mk[l], w_mv[l], w_mo[l], w_gate[l], w_up[l],
                             w_down[l], dims)
        gains = (g_mix[l], g_q_lat[l], g_kv_lat[l], g_xattn[l], g_ffn[l])
        mn = _rmsnorm(mem_prompt.reshape(bp * n_mem, d), g_mem[l], BF16, f"l{l}_norm_mem")
        mem_w = w["w_mk"].shape[1]
        mk, mv = _fused_matmul(
            [mn], [(0, w["w_mk"], 0), (0, w["w_mv"], 0)], [], [(F32, "tile"), (F32, "tile")],
            lambda accs, ex: (accs[0], accs[1]), n=mem_w, tm=_tile(bp * n_mem, 512),
            tn=_tile(mem_w, 512), name=f"l{l}_mem_kv")
        mk = mk.reshape(bp, n_mem, mem_heads, mem_hd)
        mv = mv.reshape(bp, n_mem, mem_heads, mem_hd)
        xp, (ckv, kr, k, v) = _layer(xp, pos_p, None, mk, mv, w, gains, b_gate[l], dims,
                                     batch=bp, tag=f"l{l}p")
        for lst, val in zip(outs[:6], (ckv.reshape(bp, seq, -1), kr.reshape(bp, seq, -1),
                                       k.reshape(bp, seq, sb_heads, sb_hd),
                                       v.reshape(bp, seq, sb_heads, sb_hd), mk, mv)):
            lst.append(val)
        past = (cache_mla_ckv[l], cache_mla_krope[l], cache_sb_k[l], cache_sb_v[l])
        xs, (ckv, kr, k, v) = _layer(xs, pos_s, past, cache_mem_k[l], cache_mem_v[l], w, gains,
                                     b_gate[l], dims, batch=bs, tag=f"l{l}s")
        for lst, val in zip(outs[6:], (ckv.reshape(bs, dec, -1), kr.reshape(bs, dec, -1),
                                       k.reshape(bs, dec, sb_heads, sb_hd),
                                       v.reshape(bs, dec, sb_heads, sb_hd))):
            lst.append(val)
    y_prompt = _rmsnorm(xp, g_final, F32, "final_norm_p").reshape(bp, seq, d)
    y_sample = _rmsnorm(xs, g_final, F32, "final_norm_s").reshape(bs, dec, d)
    return (y_prompt, y_sample) + tuple(jnp.stack(o) for o in outs)
```

```python
import functools
import math

import jax
import jax.numpy as jnp
from jax import lax
from jax.experimental import pallas as pl
from jax.experimental.pallas import tpu as pltpu

F32 = jnp.float32
BF16 = jnp.bfloat16

CHUNK = 64
EPS = 1e-6
ROPE_THETA = 10000.0
NEG_BIG = -1e30
LOG2E = math.log2(math.e)
MIB = 1024 * 1024
LANE = 128
_SB_GROUP = 4
_MLA_GROUP = 2
_MHA_GROUP = 2
_MLA_TQ = 256
_MLA_TK = 1024


def _cparams(sem, vmem_mib):
    return pltpu.CompilerParams(dimension_semantics=sem, vmem_limit_bytes=vmem_mib * MIB)


def _dot(a, b):
    return jnp.dot(a, b, preferred_element_type=F32)


def _dot_nt(a, b):
    return lax.dot_general(a, b, (((1,), (1,)), ((), ())), preferred_element_type=F32)


def _sigmoid(x):
    return 1.0 / (1.0 + jnp.exp(-x))


def _rms(x, g):
    return x * lax.rsqrt(jnp.mean(x * x, axis=-1, keepdims=True) + EPS) * g


def _tile(n, pref):
    if n <= pref:
        return n
    t = pref
    while n % t:
        t //= 2
    return t


def _norm_kernel(x_ref, g_ref, o_ref):
    o_ref[...] = _rms(x_ref[...], g_ref[...]).astype(o_ref.dtype)


def _rmsnorm(x, g, out_dtype, name):
    m, d = x.shape
    tm = _tile(m, 256)
    return pl.pallas_call(
        _norm_kernel,
        out_shape=jax.ShapeDtypeStruct((m, d), out_dtype),
        grid=(m // tm,),
        in_specs=[pl.BlockSpec((tm, d), lambda i: (i, 0)),
                  pl.BlockSpec((1, d), lambda i: (0, 0))],
        out_specs=pl.BlockSpec((tm, d), lambda i: (i, 0)),
        compiler_params=_cparams(("parallel",), 40),
        name=name,
    )(x, g.reshape(1, d))


def _fused_matmul(lhs, dots, extras, outs, epilogue, *, n, tm, tn, name, vmem_mib=48):
    m = lhs[0].shape[0]
    na, nd, ne = len(lhs), len(dots), len(extras)
    hpt = tn // LANE

    def kernel(*refs):
        a_refs, w_refs = refs[:na], refs[na:na + nd]
        e_refs = refs[na + nd:na + nd + ne]
        o_refs = refs[na + nd + ne:]
        accs = [_dot(a_refs[k][...], w[...].astype(BF16)) for (k, _, _), w in zip(dots, w_refs)]
        vals = epilogue(accs, [e[...] for e in e_refs])
        for o_ref, v, (_, kind) in zip(o_refs, vals, outs):
            if kind == "tile":
                o_ref[...] = v.astype(o_ref.dtype)
            else:
                for hh in range(hpt):
                    o_ref[hh] = v[:, hh * LANE:(hh + 1) * LANE].astype(o_ref.dtype)

    in_specs, args = [], []
    for a in lhs:
        in_specs.append(pl.BlockSpec((tm, a.shape[1]), lambda i, j: (i, 0)))
        args.append(a)
    for _, w, off in dots:
        in_specs.append(pl.BlockSpec((w.shape[0], tn), lambda i, j, off=off: (0, j + off)))
        args.append(w)
    for e, kind, off in extras:
        if kind == "row":
            in_specs.append(pl.BlockSpec((1, tn), lambda i, j, off=off: (0, j + off)))
        else:
            in_specs.append(pl.BlockSpec((tm, tn), lambda i, j, off=off: (i, j + off)))
        args.append(e)
    out_shape, out_specs = [], []
    for dt, kind in outs:
        if kind == "tile":
            out_shape.append(jax.ShapeDtypeStruct((m, n), dt))
            out_specs.append(pl.BlockSpec((tm, tn), lambda i, j: (i, j)))
        else:
            out_shape.append(jax.ShapeDtypeStruct((n // LANE, m, LANE), dt))
            out_specs.append(pl.BlockSpec((hpt, tm, LANE), lambda i, j: (j, i, 0)))
    return pl.pallas_call(
        kernel,
        out_shape=out_shape,
        grid=(m // tm, n // tn),
        in_specs=in_specs,
        out_specs=out_specs,
        compiler_params=_cparams(("parallel", "arbitrary"), vmem_mib),
        name=name,
    )(*args)


def _lat_post_kernel(p_ref, gq_ref, gkv_ref, cs_ref, cqn_ref, ckv_ref, ckvb_ref, kr_ref, krb_ref,
                     *, q_lora, kv_lora, rope):
    p = p_ref[...]
    cqn_ref[...] = _rms(p[:, :q_lora], gq_ref[...]).astype(cqn_ref.dtype)
    ckv = _rms(p[:, q_lora:q_lora + kv_lora], gkv_ref[...])
    ckv_ref[...] = ckv
    ckvb_ref[...] = ckv.astype(ckvb_ref.dtype)
    t = p[:, q_lora + kv_lora:] * cs_ref[...]
    kr = (t + pltpu.roll(t, rope, axis=1))[:, :rope]
    kr_ref[...] = kr
    krb_ref[...] = kr.astype(krb_ref.dtype)


def _lat_post(p, g_q, g_kv, cs, *, q_lora, kv_lora, rope, name):
    m, w = p.shape
    tm = _tile(m, 512)
    row = lambda i: (i, 0)
    fix = lambda i: (0, 0)
    return pl.pallas_call(
        functools.partial(_lat_post_kernel, q_lora=q_lora, kv_lora=kv_lora, rope=rope),
        out_shape=[jax.ShapeDtypeStruct((m, q_lora), BF16),
                   jax.ShapeDtypeStruct((m, kv_lora), F32),
                   jax.ShapeDtypeStruct((m, kv_lora), BF16),
                   jax.ShapeDtypeStruct((m, rope), F32),
                   jax.ShapeDtypeStruct((m, rope), BF16)],
        grid=(m // tm,),
        in_specs=[pl.BlockSpec((tm, w), row), pl.BlockSpec((1, q_lora), fix),
                  pl.BlockSpec((1, kv_lora), fix), pl.BlockSpec((tm, 2 * rope), row)],
        out_specs=[pl.BlockSpec((tm, q_lora), row), pl.BlockSpec((tm, kv_lora), row),
                   pl.BlockSpec((tm, kv_lora), row), pl.BlockSpec((tm, rope), row),
                   pl.BlockSpec((tm, rope), row)],
        compiler_params=_cparams(("parallel",), 32),
        name=name,
    )(p, g_q.reshape(1, -1), g_kv.reshape(1, -1), cs)


def _mla_q_kernel(cqn_ref, wq_ref, wuk_ref, cs_ref, qlat_ref, qrope_ref, *, nope, rope, scale):
    qh = _dot(cqn_ref[...], wq_ref[...])
    qn = qh[:, :nope].astype(BF16)
    qlat_ref[0] = (_dot(qn, wuk_ref[...]) * scale).astype(qlat_ref.dtype)
    t = qh[:, nope:] * cs_ref[...]
    qr = (t + pltpu.roll(t, rope, axis=1))[:, :rope]
    qrope_ref[0] = (qr * scale).astype(qrope_ref.dtype)


def _mla_q(cqn, wq_cat, wuk_t, cs, *, heads, nope, rope, kv_lora, scale, name):
    m, q_lora = cqn.shape
    tm = _tile(m, 1024)
    hw = nope + 2 * rope
    return pl.pallas_call(
        functools.partial(_mla_q_kernel, nope=nope, rope=rope, scale=scale),
        out_shape=[jax.ShapeDtypeStruct((heads, m, kv_lora), BF16),
                   jax.ShapeDtypeStruct((heads, m, rope), BF16)],
        grid=(m // tm, heads),
        in_specs=[pl.BlockSpec((tm, q_lora), lambda i, h: (i, 0)),
                  pl.BlockSpec((q_lora, hw), lambda i, h: (0, h)),
                  pl.BlockSpec((nope, kv_lora), lambda i, h: (h, 0)),
                  pl.BlockSpec((tm, 2 * rope), lambda i, h: (i, 0))],
        out_specs=[pl.BlockSpec((1, tm, kv_lora), lambda i, h: (h, i, 0)),
                   pl.BlockSpec((1, tm, rope), lambda i, h: (h, i, 0))],
        compiler_params=_cparams(("parallel", "arbitrary"), 32),
        name=name,
    )(cqn, wq_cat, wuk_t, cs)


def _flash_step(qlat_ref, qrope_ref, k, kr, mask, m_ref, l_ref, acc_ref, *, heads, group, tq):
    rows = group * tq
    tk = k.shape[0]

    def scores(g0):
        q = qlat_ref[g0:g0 + group].reshape(rows, qlat_ref.shape[-1])
        qr = qrope_ref[g0:g0 + group].reshape(rows, qrope_ref.shape[-1])
        return _dot_nt(q, k) + _dot_nt(qr, kr)

    s_next = scores(0)
    for g0 in range(0, heads, group):
        s = s_next
        if g0 + group < heads:
            s_next = scores(g0 + group)
        if mask is not None:
            s = jnp.where(mask[None], s.reshape(group, tq, tk), NEG_BIG).reshape(rows, tk)
        sl = slice(g0 * tq, g0 * tq + rows)
        m_prev = m_ref[sl]
        m_new = jnp.maximum(m_prev, jnp.max(s, axis=-1, keepdims=True))
        p = jnp.exp2(s - m_new)
        alpha = jnp.exp2(m_prev - m_new)
        l_ref[sl] = alpha * l_ref[sl] + jnp.sum(p, axis=-1, keepdims=True)
        acc_ref[sl] = alpha * acc_ref[sl] + _dot(p.astype(BF16), k)
        m_ref[sl] = m_new


def _flash_init(m_ref, l_ref, acc_ref):
    m_ref[...] = jnp.full(m_ref.shape, NEG_BIG, F32)
    l_ref[...] = jnp.zeros(l_ref.shape, F32)
    acc_ref[...] = jnp.zeros(acc_ref.shape, F32)


def _flash_finish(wuv_ref, o_ref, l_ref, acc_ref, heads, vh):
    tq = acc_ref.shape[0] // heads
    o = (acc_ref[...] * (1.0 / l_ref[...])).astype(BF16)
    for h in range(heads):
        o_ref[:, h * vh:(h + 1) * vh] = _dot(o[h * tq:(h + 1) * tq], wuv_ref[h]).astype(o_ref.dtype)


def _chunk_mask(q0, k0, tq, tk):
    qc = (q0 + lax.broadcasted_iota(jnp.int32, (tq, tk), 0)) // CHUNK
    kc = (k0 + lax.broadcasted_iota(jnp.int32, (tq, tk), 1)) // CHUNK
    return kc <= qc


def _mla_sample_kernel(qlat_ref, qrope_ref, cckv_ref, ckr_ref, nckv_ref, nkr_ref, wuv_ref, o_ref,
                       m_ref, l_ref, acc_ref, *, heads, group, tq, nkc, past, vh):
    ki = pl.program_id(1)

    @pl.when(ki == 0)
    def _():
        _flash_init(m_ref, l_ref, acc_ref)

    step = functools.partial(_flash_step, qlat_ref, qrope_ref, m_ref=m_ref, l_ref=l_ref,
                             acc_ref=acc_ref, heads=heads, group=group, tq=tq)

    @pl.when(ki < nkc)
    def _():
        step(cckv_ref[0].astype(BF16), ckr_ref[0].astype(BF16), None)

    @pl.when(ki == nkc)
    def _():
        step(nckv_ref[...], nkr_ref[...], _chunk_mask(past, past, tq, tq))
        _flash_finish(wuv_ref, o_ref, l_ref, acc_ref, heads, vh)


def _mla_sample(qlat, qrope, cache_ckv, cache_kr, ckv, krope, wuv, *, name):
    heads, _, c = qlat.shape
    rope = qrope.shape[-1]
    vh = wuv.shape[-1]
    batch, past, _ = cache_ckv.shape
    tq = ckv.shape[0] // batch
    tk = _tile(past, 1024)
    nkc = past // tk
    cmap = lambda b, ki: (b, jnp.minimum(ki, nkc - 1), 0)
    return pl.pallas_call(
        functools.partial(_mla_sample_kernel, heads=heads, group=_MLA_GROUP, tq=tq, nkc=nkc,
                          past=past, vh=vh),
        out_shape=jax.ShapeDtypeStruct((batch * tq, heads * vh), BF16),
        grid=(batch, nkc + 1),
        in_specs=[pl.BlockSpec((heads, tq, c), lambda b, ki: (0, b, 0)),
                  pl.BlockSpec((heads, tq, rope), lambda b, ki: (0, b, 0)),
                  pl.BlockSpec((1, tk, c), cmap),
                  pl.BlockSpec((1, tk, rope), cmap),
                  pl.BlockSpec((tq, c), lambda b, ki: (b, 0)),
                  pl.BlockSpec((tq, rope), lambda b, ki: (b, 0)),
                  pl.BlockSpec((heads, c, vh), lambda b, ki: (0, 0, 0))],
        out_specs=pl.BlockSpec((tq, heads * vh), lambda b, ki: (b, 0)),
        scratch_shapes=[pltpu.VMEM((heads * tq, 1), F32), pltpu.VMEM((heads * tq, 1), F32),
                        pltpu.VMEM((heads * tq, c), F32)],
        compiler_params=_cparams(("parallel", "arbitrary"), 48),
        name=name,
    )(qlat, qrope, cache_ckv, cache_kr, ckv, krope, wuv)


def _mla_kv_up_kernel(ckv_ref, kr_ref, wuk_ref, wuv_ref, kcat_ref, v_ref, *, hpt, nope, vh):
    c = ckv_ref[...]
    k = _dot(c, wuk_ref[...])
    v = _dot(c, wuv_ref[...])
    kr = kr_ref[...]
    for hh in range(hpt):
        kcat_ref[hh, :, :nope] = k[:, hh * nope:(hh + 1) * nope].astype(kcat_ref.dtype)
        kcat_ref[hh, :, nope:] = kr
        v_ref[hh] = v[:, hh * vh:(hh + 1) * vh].astype(v_ref.dtype)


def _mla_kv_up(ckv_b, krope_b, wuk_flat, wuv_flat, *, heads, name):
    m, c = ckv_b.shape
    rope = krope_b.shape[1]
    nope, vh = wuk_flat.shape[1] // heads, wuv_flat.shape[1] // heads
    hpt = _tile(heads, 4)
    tm = _tile(m, 1024)
    return pl.pallas_call(
        functools.partial(_mla_kv_up_kernel, hpt=hpt, nope=nope, vh=vh),
        out_shape=[jax.ShapeDtypeStruct((heads, m, nope + rope), BF16),
                   jax.ShapeDtypeStruct((heads, m, vh), BF16)],
        grid=(m // tm, heads // hpt),
        in_specs=[pl.BlockSpec((tm, c), lambda i, j: (i, 0)),
                  pl.BlockSpec((tm, rope), lambda i, j: (i, 0)),
                  pl.BlockSpec((c, hpt * nope), lambda i, j: (0, j)),
                  pl.BlockSpec((c, hpt * vh), lambda i, j: (0, j))],
        out_specs=[pl.BlockSpec((hpt, tm, nope + rope), lambda i, j: (j, i, 0)),
                   pl.BlockSpec((hpt, tm, vh), lambda i, j: (j, i, 0))],
        compiler_params=_cparams(("parallel", "arbitrary"), 32),
        name=name,
    )(ckv_b, krope_b, wuk_flat, wuv_flat)


def _mla_qcat_kernel(cqn_ref, wq_ref, cs_ref, qcat_ref, *, hpt, nope, rope, scale):
    qh = _dot(cqn_ref[...], wq_ref[...])
    hw = nope + 2 * rope
    cs = cs_ref[...]
    for hh in range(hpt):
        qcat_ref[hh, :, :nope] = (qh[:, hh * hw:hh * hw + nope] * scale).astype(qcat_ref.dtype)
        t = qh[:, hh * hw + nope:(hh + 1) * hw] * cs
        qr = (t + pltpu.roll(t, rope, axis=1))[:, :rope]
        qcat_ref[hh, :, nope:] = (qr * scale).astype(qcat_ref.dtype)


def _mla_qcat(cqn, wq_cat, cs, *, heads, nope, rope, scale, name):
    m, q_lora = cqn.shape
    hw = nope + 2 * rope
    hpt = _tile(heads, 4)
    tm = _tile(m, 1024)
    return pl.pallas_call(
        functools.partial(_mla_qcat_kernel, hpt=hpt, nope=nope, rope=rope, scale=scale),
        out_shape=jax.ShapeDtypeStruct((heads, m, nope + rope), BF16),
        grid=(m // tm, heads // hpt),
        in_specs=[pl.BlockSpec((tm, q_lora), lambda i, j: (i, 0)),
                  pl.BlockSpec((q_lora, hpt * hw), lambda i, j: (0, j)),
                  pl.BlockSpec((tm, 2 * rope), lambda i, j: (i, 0))],
        out_specs=pl.BlockSpec((hpt, tm, nope + rope), lambda i, j: (j, i, 0)),
        compiler_params=_cparams(("parallel", "arbitrary"), 32),
        name=name,
    )(cqn, wq_cat, cs)


def _mha_step(q_ref, k_ref, v_ref, mask, m_ref, l_ref, acc_ref, *, heads, group, tq):
    rows = group * tq
    tk = k_ref.shape[1]

    def scores(g0):
        return jnp.concatenate([_dot_nt(q_ref[g0 + i], k_ref[g0 + i]) for i in range(group)], axis=0)

    s_next = scores(0)
    for g0 in range(0, heads, group):
        s = s_next
        if g0 + group < heads:
            s_next = scores(g0 + group)
        if mask is not None:
            s = jnp.where(mask[None], s.reshape(group, tq, tk), NEG_BIG).reshape(rows, tk)
        sl = slice(g0 * tq, g0 * tq + rows)
        m_prev = m_ref[sl]
        m_new = jnp.maximum(m_prev, jnp.max(s, axis=-1, keepdims=True))
        p = jnp.exp2(s - m_new)
        alpha = jnp.exp2(m_prev - m_new)
        l_ref[sl] = alpha * l_ref[sl] + jnp.sum(p, axis=-1, keepdims=True)
        p = p.astype(BF16)
        pv = jnp.concatenate([_dot(p[i * tq:(i + 1) * tq], v_ref[g0 + i]) for i in range(group)], axis=0)
        acc_ref[sl] = alpha * acc_ref[sl] + pv
        m_ref[sl] = m_new


def _mha_prompt_kernel(q_ref, k_ref, v_ref, o_ref, m_ref, l_ref, acc_ref, *, heads, group, tq, tk, vh):
    qi, ki = pl.program_id(1), pl.program_id(2)
    k_last = ((qi + 1) * tq - 1) // tk
    partial = (ki + 1) * tk > qi * tq + CHUNK
    step = functools.partial(_mha_step, q_ref, k_ref, v_ref, m_ref=m_ref, l_ref=l_ref,
                             acc_ref=acc_ref, heads=heads, group=group, tq=tq)

    @pl.when(ki == 0)
    def _():
        _flash_init(m_ref, l_ref, acc_ref)

    @pl.when((ki <= k_last) & jnp.logical_not(partial))
    def _():
        step(None)

    @pl.when((ki <= k_last) & partial)
    def _():
        step(_chunk_mask(qi * tq, ki * tk, tq, tk))

    @pl.when(ki == k_last)
    def _():
        o = acc_ref[...] * (1.0 / l_ref[...])
        for h in range(heads):
            o_ref[:, h * vh:(h + 1) * vh] = o[h * tq:(h + 1) * tq].astype(o_ref.dtype)


def _mha_prompt(qcat, kcat, v, *, batch, seq, name):
    heads, _, dk = qcat.shape
    vh = v.shape[-1]
    tq = _tile(seq, _MLA_TQ)
    tk = _tile(seq, _MLA_TK)
    nq, nk = seq // tq, seq // tk

    def kmap(b, qi, ki):
        return (0, b * nk + jnp.minimum(ki, ((qi + 1) * tq - 1) // tk), 0)

    return pl.pallas_call(
        functools.partial(_mha_prompt_kernel, heads=heads, group=_MHA_GROUP, tq=tq, tk=tk, vh=vh),
        out_shape=jax.ShapeDtypeStruct((batch * seq, heads * vh), BF16),
        grid=(batch, nq, nk),
        in_specs=[pl.BlockSpec((heads, tq, dk), lambda b, qi, ki: (0, b * nq + qi, 0)),
                  pl.BlockSpec((heads, tk, dk), kmap),
                  pl.BlockSpec((heads, tk, vh), kmap)],
        out_specs=pl.BlockSpec((tq, heads * vh), lambda b, qi, ki: (b * nq + qi, 0)),
        scratch_shapes=[pltpu.VMEM((heads * tq, 1), F32), pltpu.VMEM((heads * tq, 1), F32),
                        pltpu.VMEM((heads * tq, vh), F32)],
        compiler_params=_cparams(("parallel", "parallel", "arbitrary"), 56),
        name=name,
    )(qcat, kcat, v)


SB_SKIP = -160.0


def _sb_block(q_ref, get_kv, tri, valid, run_ref, acc_ref, *, heads, group, tq):
    rows = group * tq
    for g0 in range(0, heads, group):
        kv = [get_kv(h) for h in range(g0, g0 + group)]
        z = jnp.concatenate([_dot_nt(q_ref[g0 + i], kv[i][0]) for i in range(group)], axis=0)
        tk = z.shape[-1]
        sp = jnp.log2(1.0 + jnp.exp2(-jnp.abs(z)))
        lk = -(jnp.maximum(z, 0.0) + sp)
        if valid is not None:
            lk = jnp.where(valid[None], lk.reshape(group, tq, tk), 0.0).reshape(rows, tk)
        hi = lk.astype(BF16)
        lo = (lk - hi.astype(F32)).astype(BF16)
        suffix = _dot(hi, tri) + _dot(lo, tri)
        run = run_ref[g0 * tq:g0 * tq + rows]
        w = jnp.exp2((jnp.minimum(z, 0.0) - sp) + suffix + run)
        if valid is not None:
            w = jnp.where(valid[None], w.reshape(group, tq, tk), 0.0).reshape(rows, tk)
        w = w.astype(BF16)
        run_ref[g0 * tq:g0 * tq + rows] = run + suffix[:, :1] + lk[:, :1]
        for i in range(group):
            r0 = (g0 + i) * tq
            acc_ref[r0:r0 + tq] = acc_ref[r0:r0 + tq] + _dot(w[i * tq:(i + 1) * tq], kv[i][1])


def _strict_lower(n):
    row = lax.broadcasted_iota(jnp.int32, (n, n), 0)
    col = lax.broadcasted_iota(jnp.int32, (n, n), 1)
    return col < row


def _all_below(run_ref, bound):
    m = jnp.max(run_ref[...], axis=0, keepdims=True)
    return m[0, 0] <= bound


def _sb_write(o_ref, acc_ref, heads, tq, hd):
    for h in range(heads):
        o_ref[:, h * hd:(h + 1) * hd] = acc_ref[h * tq:(h + 1) * tq].astype(o_ref.dtype)


def _sb_prompt_kernel(q_ref, kd_ref, vd_ref, tri_ref, k_hbm, v_hbm, o_ref, kbuf, vbuf, sem,
                      run_ref, acc_ref, *, heads, group, tq, hd, nq):
    b, qi = pl.program_id(0), pl.program_id(1)

    def copies(j, slot):
        row0 = pl.multiple_of((b * nq + j) * tq, tq)
        return (pltpu.make_async_copy(k_hbm.at[:, pl.ds(row0, tq), :], kbuf.at[slot], sem.at[0, slot]),
                pltpu.make_async_copy(v_hbm.at[:, pl.ds(row0, tq), :], vbuf.at[slot], sem.at[1, slot]))

    @pl.when(qi > 0)
    def _():
        for cp in copies(qi - 1, 0):
            cp.start()

    run_ref[...] = jnp.zeros(run_ref.shape, F32)
    acc_ref[...] = jnp.zeros(acc_ref.shape, F32)
    tri = tri_ref[...]
    blk = functools.partial(_sb_block, run_ref=run_ref, acc_ref=acc_ref, heads=heads, group=group,
                            tq=tq)
    blk(q_ref, lambda h: (kd_ref[h], vd_ref[h]), tri, _strict_lower(tq))

    def cond(c):
        return (c[0] >= 0) & (c[1] > 0)

    def body(c):
        j = c[0]
        slot = lax.rem(qi - 1 - j, 2)
        for cp in copies(j, slot):
            cp.wait()

        @pl.when(j > 0)
        def _():
            for cp in copies(j - 1, 1 - slot):
                cp.start()

        blk(q_ref, lambda h: (kbuf[slot, h], vbuf[slot, h]), tri, None)
        go = jnp.where(_all_below(run_ref, SB_SKIP), 0, 1).astype(jnp.int32)
        return (j - 1, go)

    j_end, _ = lax.while_loop(cond, body, (qi - 1, jnp.int32(1)))

    @pl.when(j_end >= 0)
    def _():
        for cp in copies(j_end, lax.rem(qi - 1 - j_end, 2)):
            cp.wait()

    _sb_write(o_ref, acc_ref, heads, tq, hd)


def _sb_prompt(q, k, v, tri, *, batch, seq, name):
    heads, _, hd = q.shape
    tq = tri.shape[0]
    nq = seq // tq
    blk = lambda b, qi: (0, b * nq + qi, 0)
    return pl.pallas_call(
        functools.partial(_sb_prompt_kernel, heads=heads, group=_SB_GROUP, tq=tq, hd=hd, nq=nq),
        out_shape=jax.ShapeDtypeStruct((batch * seq, heads * hd), BF16),
        grid=(batch, nq),
        in_specs=[pl.BlockSpec((heads, tq, hd), blk), pl.BlockSpec((heads, tq, hd), blk),
                  pl.BlockSpec((heads, tq, hd), blk),
                  pl.BlockSpec((tq, tq), lambda b, qi: (0, 0)),
                  pl.BlockSpec(memory_space=pl.ANY), pl.BlockSpec(memory_space=pl.ANY)],
        out_specs=pl.BlockSpec((tq, heads * hd), lambda b, qi: (b * nq + qi, 0)),
        scratch_shapes=[pltpu.VMEM((2, heads, tq, hd), BF16), pltpu.VMEM((2, heads, tq, hd), BF16),
                        pltpu.SemaphoreType.DMA((2, 2)),
                        pltpu.VMEM((heads * tq, 1), F32), pltpu.VMEM((heads * tq, hd), F32)],
        compiler_params=_cparams(("arbitrary", "arbitrary"), 40),
        name=name,
    )(q, k, v, tri, k, v)


def _sb_sample_kernel(q_ref, nk_ref, nv_ref, tri_ref, ck_hbm, cv_hbm, o_ref, kbuf, vbuf, sem,
                      run_ref, acc_ref, *, heads, group, tq, tk, hd, nkc):
    b = pl.program_id(0)

    def copies(j, slot):
        p0 = pl.multiple_of(j * tk, tk)
        return (pltpu.make_async_copy(ck_hbm.at[b, pl.ds(p0, tk)], kbuf.at[slot], sem.at[0, slot]),
                pltpu.make_async_copy(cv_hbm.at[b, pl.ds(p0, tk)], vbuf.at[slot], sem.at[1, slot]))

    for cp in copies(nkc - 1, 0):
        cp.start()

    run_ref[...] = jnp.zeros(run_ref.shape, F32)
    acc_ref[...] = jnp.zeros(acc_ref.shape, F32)
    blk = functools.partial(_sb_block, run_ref=run_ref, acc_ref=acc_ref, heads=heads, group=group,
                            tq=tq)
    blk(q_ref, lambda h: (nk_ref[h], nv_ref[h]), tri_ref[:tq, :tq], _strict_lower(tq))
    tri = tri_ref[...]

    def cond(c):
        return (c[0] >= 0) & (c[1] > 0)

    def body(c):
        j = c[0]
        slot = lax.rem(nkc - 1 - j, 2)
        for cp in copies(j, slot):
            cp.wait()

        @pl.when(j > 0)
        def _():
            for cp in copies(j - 1, 1 - slot):
                cp.start()

        blk(q_ref, lambda h: (kbuf[slot, :, h, :].astype(BF16), vbuf[slot, :, h, :].astype(BF16)),
            tri, None)
        go = jnp.where(_all_below(run_ref, SB_SKIP), 0, 1).astype(jnp.int32)
        return (j - 1, go)

    j_end, _ = lax.while_loop(cond, body, (jnp.int32(nkc - 1), jnp.int32(1)))

    @pl.when(j_end >= 0)
    def _():
        for cp in copies(j_end, lax.rem(nkc - 1 - j_end, 2)):
            cp.wait()

    _sb_write(o_ref, acc_ref, heads, tq, hd)


def _sb_sample(q, k_new, v_new, cache_k, cache_v, tri, *, name):
    heads, rows, hd = q.shape
    batch, past = cache_k.shape[:2]
    tq = rows // batch
    tk = tri.shape[0]
    nkc = past // tk
    new = lambda b: (0, b, 0)
    return pl.pallas_call(
        functools.partial(_sb_sample_kernel, heads=heads, group=_SB_GROUP, tq=tq, tk=tk, hd=hd,
                          nkc=nkc),
        out_shape=jax.ShapeDtypeStruct((rows, heads * hd), BF16),
        grid=(batch,),
        in_specs=[pl.BlockSpec((heads, tq, hd), new), pl.BlockSpec((heads, tq, hd), new),
                  pl.BlockSpec((heads, tq, hd), new),
                  pl.BlockSpec((tk, tk), lambda b: (0, 0)),
                  pl.BlockSpec(memory_space=pl.ANY), pl.BlockSpec(memory_space=pl.ANY)],
        out_specs=pl.BlockSpec((tq, heads * hd), lambda b: (b, 0)),
        scratch_shapes=[pltpu.VMEM((2, tk, heads, hd), F32), pltpu.VMEM((2, tk, heads, hd), F32),
                        pltpu.SemaphoreType.DMA((2, 2)),
                        pltpu.VMEM((heads * tq, 1), F32), pltpu.VMEM((heads * tq, hd), F32)],
        compiler_params=_cparams(("arbitrary",), 40),
        name=name,
    )(q, k_new, v_new, tri, cache_k, cache_v)


def _mem_attn_kernel(x_ref, g_ref, wq_ref, mk_ref, mv_ref, wo_ref, gn_ref, xo_ref, hn_ref, o_scr,
                     *, nsub, sub, heads, hd, scale):
    x = x_ref[...]
    mq = (_dot(_rms(x, g_ref[...]).astype(BF16), wq_ref[...]) * scale).astype(BF16)
    for s in range(nsub):
        for h in range(heads):
            q = mq[s * sub:(s + 1) * sub, h * hd:(h + 1) * hd]
            k = mk_ref[s, :, h * hd:(h + 1) * hd]
            v = mv_ref[s, :, h * hd:(h + 1) * hd]
            sc = _dot_nt(q, k)
            p = jnp.exp(sc - jnp.max(sc, axis=-1, keepdims=True))
            p = p * (1.0 / jnp.sum(p, axis=-1, keepdims=True))
            o_scr[s * sub:(s + 1) * sub, h * hd:(h + 1) * hd] = _dot(p.astype(BF16), v).astype(BF16)
    xn = x + _dot(o_scr[...], wo_ref[...])
    xo_ref[...] = xn
    hn_ref[...] = _rms(xn, gn_ref[...]).astype(hn_ref.dtype)


def _mem_attn(x, g, w_mq, mem_k, mem_v, w_mo, g_next, *, sub, heads, name):
    m, d = x.shape
    nb, n_mem, width = mem_k.shape
    hd = width // heads
    tm = _tile(m, 256)
    if sub >= tm:
        nsub, rows = 1, tm
        per = sub // tm
        mmap = lambda i: (i // per, 0, 0)
    else:
        nsub, rows = tm // sub, sub
        mmap = lambda i: (i, 0, 0)
    row = lambda i: (i, 0)
    fix = lambda i: (0, 0)
    return pl.pallas_call(
        functools.partial(_mem_attn_kernel, nsub=nsub, sub=rows, heads=heads, hd=hd,
                          scale=hd ** -0.5),
        out_shape=[jax.ShapeDtypeStruct((m, d), F32), jax.ShapeDtypeStruct((m, d), BF16)],
        grid=(m // tm,),
        in_specs=[pl.BlockSpec((tm, d), row), pl.BlockSpec((1, d), fix),
                  pl.BlockSpec((d, width), fix),
                  pl.BlockSpec((nsub, n_mem, width), mmap),
                  pl.BlockSpec((nsub, n_mem, width), mmap),
                  pl.BlockSpec((width, d), fix), pl.BlockSpec((1, d), fix)],
        out_specs=[pl.BlockSpec((tm, d), row), pl.BlockSpec((tm, d), row)],
        scratch_shapes=[pltpu.VMEM((tm, width), BF16)],
        compiler_params=_cparams(("parallel",), 48),
        name=name,
    )(x, g.reshape(1, d), w_mq, mem_k, mem_v, w_mo, g_next.reshape(1, d))


def _rotate_half_cols(w):
    half = w.shape[-1] // 2
    return jnp.concatenate([-w[..., half:], w[..., :half]], axis=-1)


def _rope_table(pos, half):
    inv = ROPE_THETA ** (-jnp.arange(half, dtype=F32) / half)
    ang = pos.astype(F32)[:, None] * inv[None, :]
    c, s = jnp.cos(ang), jnp.sin(ang)
    return jnp.concatenate([c, c, s, s], axis=-1)


def _split_w_in_kernel(w_ref, lat_ref, rest_ref, *, o_kr, o_sb):
    w = w_ref[...]
    half = (o_sb - o_kr) // 2
    rot = jnp.concatenate([-w[:, o_kr + half:o_sb], w[:, o_kr:o_kr + half]], axis=1)
    lat_ref[...] = jnp.concatenate([w[:, :o_sb], rot], axis=1).astype(lat_ref.dtype)
    rest_ref[...] = w[:, o_sb:].astype(rest_ref.dtype)


def _split_w_in(w_in, o_kr, o_sb):
    k, n = w_in.shape
    tr = _tile(k, 128)
    lat_w = o_sb + (o_sb - o_kr)
    return pl.pallas_call(
        functools.partial(_split_w_in_kernel, o_kr=o_kr, o_sb=o_sb),
        out_shape=[jax.ShapeDtypeStruct((k, lat_w), BF16), jax.ShapeDtypeStruct((k, n - o_sb), BF16)],
        grid=(k // tr,),
        in_specs=[pl.BlockSpec((tr, n), lambda i: (i, 0))],
        out_specs=[pl.BlockSpec((tr, lat_w), lambda i: (i, 0)),
                   pl.BlockSpec((tr, n - o_sb), lambda i: (i, 0))],
        compiler_params=_cparams(("parallel",), 48),
        name="split_w_in",
    )(w_in)


def _prepare_weights(w_in, w_uq, w_uk, w_uv, w_branch_a, w_branch_b, w_out, w_mq, w_mk, w_mv, w_mo,
                     w_gate, w_up, w_down, dims):
    q_lora, kv_lora, rope, heads, nope = dims
    o_kr = q_lora + kv_lora
    o_sb = o_kr + rope
    w_lat, w_rest = _split_w_in(w_in, o_kr, o_sb)
    wq_cat = jnp.concatenate([w_uq, _rotate_half_cols(w_uq[..., nope:])], axis=-1)
    return dict(
        w_lat=w_lat,
        w_rest=w_rest,
        wq_cat=wq_cat.reshape(q_lora, -1).astype(BF16),
        wuk_t=jnp.transpose(w_uk, (1, 2, 0)).reshape(heads * nope, kv_lora).astype(BF16),
        wuv=jnp.transpose(w_uv, (1, 0, 2)).astype(BF16),
        wuk_flat=w_uk.reshape(kv_lora, -1).astype(BF16),
        wuv_flat=w_uv.reshape(kv_lora, -1).astype(BF16),
        w_ba=w_branch_a, w_bb=w_branch_b, w_out=w_out,
        w_mq=w_mq.astype(BF16), w_mk=w_mk.astype(BF16), w_mv=w_mv.astype(BF16),
        w_mo=w_mo.astype(BF16),
        w_gate=w_gate, w_up=w_up, w_down=w_down.astype(BF16),
    )


def _layer(x, pos, past, mem_k, mem_v, w, gains, b_gate, dims, *, batch, tag):
    g_mix, g_q_lat, g_kv_lat, g_xattn, g_ffn = gains
    q_lora, kv_lora, rope, heads, nope = dims
    m, d = x.shape
    t = m // batch
    sb_width = (w["w_rest"].shape[1] - 2 * d) // 3
    sb_heads = sb_width // LANE
    mla_scale = (nope + rope) ** -0.5 * LOG2E
    sb_scale = LANE ** -0.5 * LOG2E
    tm = _tile(m, 1024)
    tn = 512

    h = _rmsnorm(x, g_mix, BF16, f"{tag}_norm_mix")
    lat_w = w["w_lat"].shape[1]
    (p_lat,) = _fused_matmul(
        [h], [(0, w["w_lat"], 0)], [], [(F32, "tile")], lambda accs, ex: (accs[0],),
        n=lat_w, tm=_tile(m, 256), tn=lat_w, name=f"{tag}_proj_lat")
    cs = jnp.tile(_rope_table(pos, rope // 2), (batch, 1))
    cqn, ckv, ckv_b, krope, krope_b = _lat_post(p_lat, g_q_lat, g_kv_lat, cs, q_lora=q_lora,
                                                kv_lora=kv_lora, rope=rope, name=f"{tag}_lat_post")
    nsb = sb_width // tn
    (sbq,) = _fused_matmul(
        [h], [(0, w["w_rest"], 0)], [], [(BF16, "heads")],
        lambda accs, ex: (accs[0] * sb_scale,), n=sb_width, tm=tm, tn=tn, name=f"{tag}_proj_sbq")
    sbk, sbk_b = _fused_matmul(
        [h], [(0, w["w_rest"], nsb)], [], [(F32, "tile"), (BF16, "heads")],
        lambda accs, ex: (accs[0], accs[0]), n=sb_width, tm=tm, tn=tn, name=f"{tag}_proj_sbk")
    sbv, sbv_b = _fused_matmul(
        [h], [(0, w["w_rest"], 2 * nsb)], [], [(F32, "tile"), (BF16, "heads")],
        lambda accs, ex: (accs[0], accs[0]), n=sb_width, tm=tm, tn=tn, name=f"{tag}_proj_sbv")
    (gates,) = _fused_matmul(
        [h], [(0, w["w_rest"], 3 * nsb)], [(b_gate.reshape(1, -1), "row", 0)], [(BF16, "tile")],
        lambda accs, ex: (_sigmoid(accs[0] + ex[0]),), n=2 * d, tm=tm, tn=tn,
        name=f"{tag}_proj_gates")

    tri_n = 256 if t % 256 == 0 else t
    tri = (jnp.arange(tri_n)[:, None] > jnp.arange(tri_n)[None, :]).astype(BF16)
    if past is None:
        qcat = _mla_qcat(cqn, w["wq_cat"], cs, heads=heads, nope=nope, rope=rope, scale=mla_scale,
                         name=f"{tag}_mla_q")
        kcat, vmla = _mla_kv_up(ckv_b, krope_b, w["wuk_flat"], w["wuv_flat"], heads=heads,
                                name=f"{tag}_mla_kv")
        o_a = _mha_prompt(qcat, kcat, vmla, batch=batch, seq=t, name=f"{tag}_mla_attn")
        o_b = _sb_prompt(sbq, sbk_b, sbv_b, tri, batch=batch, seq=t, name=f"{tag}_sb_attn")
    else:
        c_ckv, c_kr, c_k, c_v = past
        qlat, qrope = _mla_q(cqn, w["wq_cat"], w["wuk_t"], cs, heads=heads, nope=nope, rope=rope,
                             kv_lora=kv_lora, scale=mla_scale, name=f"{tag}_mla_q")
        o_a = _mla_sample(qlat, qrope, c_ckv, c_kr, ckv_b, krope_b, w["wuv"], name=f"{tag}_mla_attn")
        tri = (jnp.arange(256)[:, None] > jnp.arange(256)[None, :]).astype(BF16)
        o_b = _sb_sample(sbq, sbk_b, sbv_b, c_k, c_v, tri, name=f"{tag}_sb_attn")

    ng = d // tn
    (merged,) = _fused_matmul(
        [o_a, o_b], [(0, w["w_ba"], 0), (1, w["w_bb"], 0)],
        [(gates, "tile", 0), (gates, "tile", ng)], [(BF16, "tile")],
        lambda accs, ex: (ex[0].astype(F32) * accs[0] + ex[1].astype(F32) * accs[1],),
        n=d, tm=tm, tn=tn, name=f"{tag}_merge", vmem_mib=56)
    (x,) = _fused_matmul(
        [merged], [(0, w["w_out"], 0)], [(x, "tile", 0)], [(F32, "tile")],
        lambda accs, ex: (ex[0] + accs[0],), n=d, tm=tm, tn=tn, name=f"{tag}_out_proj", vmem_mib=56)

    mem_heads = mem_k.shape[2]
    mk = mem_k.reshape(mem_k.shape[0], mem_k.shape[1], -1).astype(BF16)
    mv = mem_v.reshape(mem_v.shape[0], mem_v.shape[1], -1).astype(BF16)
    x, hf = _mem_attn(x, g_xattn, w["w_mq"], mk, mv, w["w_mo"], g_ffn, sub=t, heads=mem_heads,
                      name=f"{tag}_mem_attn")

    d_ff = w["w_gate"].shape[1]
    tn_ff = _tile(d_ff, 256) if d_ff % 512 else 512
    (act,) = _fused_matmul(
        [hf], [(0, w["w_gate"], 0), (0, w["w_up"], 0)], [], [(BF16, "tile")],
        lambda accs, ex: (accs[0] * _sigmoid(accs[0]) * accs[1],), n=d_ff, tm=tm, tn=tn_ff,
        name=f"{tag}_ffn_up", vmem_mib=56)
    (x,) = _fused_matmul(
        [act], [(0, w["w_down"], 0)], [(x, "tile", 0)], [(F32, "tile")],
        lambda accs, ex: (ex[0] + accs[0],), n=d, tm=_tile(m, 512), tn=512, name=f"{tag}_ffn_down",
        vmem_mib=56)
    return x, (ckv, krope, sbk, sbv)


def kernel(x_prompt, x_sample, cache_mla_ckv, cache_mla_krope, cache_sb_k, cache_sb_v, cache_mem_k, cache_mem_v, mem_prompt, g_mix, w_in, b_gate, g_q_lat, w_uq, g_kv_lat, w_uk, w_uv, w_branch_a, w_branch_b, w_out, g_xattn, g_mem, w_mq, w_mk, w_mv, w_mo, g_ffn, w_gate, w_up, w_down, g_final):
    depth = w_in.shape[0]
    bp, seq, d = x_prompt.shape
    bs, dec, _ = x_sample.shape
    past_len = cache_mla_ckv.shape[2]
    q_lora, heads, qk = w_uq.shape[1:]
    kv_lora, _, nope = w_uk.shape[1:]
    rope = qk - nope
    dims = (q_lora, kv_lora, rope, heads, nope)
    sb_heads, sb_hd = cache_sb_k.shape[3:]
    n_mem, mem_heads, mem_hd = cache_mem_k.shape[2:]
    pos_p = jnp.arange(seq)
    pos_s = past_len + jnp.arange(dec)

    xp = x_prompt.reshape(bp * seq, d)
    xs = x_sample.reshape(bs * dec, d)
    outs = [[] for _ in range(10)]
    for l in range(depth):
        w = _prepare_weights(w_in[l], w_uq[l], w_uk[l], w_uv[l], w_branch_a[l], w_branch_b[l],
                             w_out[l], w_mq[l], w_mk[l], w_mv[l], w_mo[l], w_gate[l], w_up[l],
                             w_down[l], dims)
        gains = (g_mix[l], g_q_lat[l], g_kv_lat[l], g_xattn[l], g_ffn[l])
        mn = _rmsnorm(mem_prompt.reshape(bp * n_mem, d), g_mem[l], BF16, f"l{l}_norm_mem")
        mem_w = w["w_mk"].shape[1]
        mk, mv = _fused_matmul(
            [mn], [(0, w["w_mk"], 0), (0, w["w_mv"], 0)], [], [(F32, "tile"), (F32, "tile")],
            lambda accs, ex: (accs[0], accs[1]), n=mem_w, tm=_tile(bp * n_mem, 512),
            tn=_tile(mem_w, 512), name=f"l{l}_mem_kv")
        mk = mk.reshape(bp, n_mem, mem_heads, mem_hd)
        mv = mv.reshape(bp, n_mem, mem_heads, mem_hd)
        xp, (ckv, kr, k, v) = _layer(xp, pos_p, None, mk, mv, w, gains, b_gate[l], dims,
                                     batch=bp, tag=f"l{l}p")
        for lst, val in zip(outs[:6], (ckv.reshape(bp, seq, -1), kr.reshape(bp, seq, -1),
                                       k.reshape(bp, seq, sb_heads, sb_hd),
                                       v.reshape(bp, seq, sb_heads, sb_hd), mk, mv)):
            lst.append(val)
        past = (cache_mla_ckv[l], cache_mla_krope[l], cache_sb_k[l], cache_sb_v[l])
        xs, (ckv, kr, k, v) = _layer(xs, pos_s, past, cache_mem_k[l], cache_mem_v[l], w, gains,
                                     b_gate[l], dims, batch=bs, tag=f"l{l}s")
        for lst, val in zip(outs[6:], (ckv.reshape(bs, dec, -1), kr.reshape(bs, dec, -1),
                                       k.reshape(bs, dec, sb_heads, sb_hd),
                                       v.reshape(bs, dec, sb_heads, sb_hd))):
            lst.append(val)
    y_prompt = _rmsnorm(xp, g_final, F32, "final_norm_p").reshape(bp, seq, d)
    y_sample = _rmsnorm(xs, g_final, F32, "final_norm_s").reshape(bs, dec, d)
    return (y_prompt, y_sample) + tuple(jnp.stack(o) for o in outs)
```

```python
import functools
import math

import jax
import jax.numpy as jnp
from jax import lax
from jax.experimental import pallas as pl
from jax.experimental.pallas import tpu as pltpu

F32 = jnp.float32
BF16 = jnp.bfloat16

CHUNK = 64
EPS = 1e-6
ROPE_THETA = 10000.0
NEG_BIG = -1e30
LOG2E = math.log2(math.e)
MIB = 1024 * 1024
LANE = 128
_SB_GROUP = 4
_MLA_GROUP = 2
_MHA_GROUP = 2
_MLA_TQ = 256
_MLA_TK = 1024


def _cparams(sem, vmem_mib):
    return pltpu.CompilerParams(dimension_semantics=sem, vmem_limit_bytes=vmem_mib * MIB)


def _dot(a, b):
    return jnp.dot(a, b, preferred_element_type=F32)


def _dot_nt(a, b):
    return lax.dot_general(a, b, (((1,), (1,)), ((), ())), preferred_element_type=F32)


def _sigmoid(x):
    return 1.0 / (1.0 + jnp.exp(-x))


def _rms(x, g):
    return x * lax.rsqrt(jnp.mean(x * x, axis=-1, keepdims=True) + EPS) * g


def _tile(n, pref):
    if n <= pref:
        return n
    t = pref
    while n % t:
        t //= 2
    return t


def _norm_kernel(x_ref, g_ref, o_ref):
    o_ref[...] = _rms(x_ref[...], g_ref[...]).astype(o_ref.dtype)


def _rmsnorm(x, g, out_dtype, name):
    m, d = x.shape
    tm = _tile(m, 256)
    return pl.pallas_call(
        _norm_kernel,
        out_shape=jax.ShapeDtypeStruct((m, d), out_dtype),
        grid=(m // tm,),
        in_specs=[pl.BlockSpec((tm, d), lambda i: (i, 0)),
                  pl.BlockSpec((1, d), lambda i: (0, 0))],
        out_specs=pl.BlockSpec((tm, d), lambda i: (i, 0)),
        compiler_params=_cparams(("parallel",), 40),
        name=name,
    )(x, g.reshape(1, d))


def _fused_matmul(lhs, dots, extras, outs, epilogue, *, n, tm, tn, name, vmem_mib=48, w_rows=False):
    m = lhs[0].shape[0]
    na, nd, ne = len(lhs), len(dots), len(extras)
    hpt = tn // LANE

    def kernel(*refs):
        a_refs, w_refs = refs[:na], refs[na:na + nd]
        e_refs = refs[na + nd:na + nd + ne]
        o_refs = refs[na + nd + ne:]
        mm = _dot_nt if w_rows else _dot
        accs = [mm(a_refs[k][...], w[...].astype(BF16)) for (k, _, _), w in zip(dots, w_refs)]
        vals = epilogue(accs, [e[...] for e in e_refs])
        for o_ref, v, (_, kind) in zip(o_refs, vals, outs):
            if kind == "tile":
                o_ref[...] = v.astype(o_ref.dtype)
            else:
                for hh in range(hpt):
                    o_ref[hh] = v[:, hh * LANE:(hh + 1) * LANE].astype(o_ref.dtype)

    in_specs, args = [], []
    for a in lhs:
        in_specs.append(pl.BlockSpec((tm, a.shape[1]), lambda i, j: (i, 0)))
        args.append(a)
    for _, w, off in dots:
        if w_rows:
            in_specs.append(pl.BlockSpec((tn, w.shape[1]), lambda i, j, off=off: (j + off, 0)))
        else:
            in_specs.append(pl.BlockSpec((w.shape[0], tn), lambda i, j, off=off: (0, j + off)))
        args.append(w)
    for e, kind, off in extras:
        if kind == "row":
            in_specs.append(pl.BlockSpec((1, tn), lambda i, j, off=off: (0, j + off)))
        else:
            in_specs.append(pl.BlockSpec((tm, tn), lambda i, j, off=off: (i, j + off)))
        args.append(e)
    out_shape, out_specs = [], []
    for dt, kind in outs:
        if kind == "tile":
            out_shape.append(jax.ShapeDtypeStruct((m, n), dt))
            out_specs.append(pl.BlockSpec((tm, tn), lambda i, j: (i, j)))
        else:
            out_shape.append(jax.ShapeDtypeStruct((n // LANE, m, LANE), dt))
            out_specs.append(pl.BlockSpec((hpt, tm, LANE), lambda i, j: (j, i, 0)))
    return pl.pallas_call(
        kernel,
        out_shape=out_shape,
        grid=(m // tm, n // tn),
        in_specs=in_specs,
        out_specs=out_specs,
        compiler_params=_cparams(("parallel", "arbitrary"), vmem_mib),
        name=name,
    )(*args)


def _lat_post_kernel(p_ref, gq_ref, gkv_ref, cs_ref, cqn_ref, ckv_ref, ckvb_ref, kr_ref, krb_ref,
                     *, q_lora, kv_lora, rope):
    p = p_ref[...]
    cqn_ref[...] = _rms(p[:, :q_lora], gq_ref[...]).astype(cqn_ref.dtype)
    ckv = _rms(p[:, q_lora:q_lora + kv_lora], gkv_ref[...])
    ckv_ref[...] = ckv
    ckvb_ref[...] = ckv.astype(ckvb_ref.dtype)
    t = p[:, q_lora + kv_lora:] * cs_ref[...]
    kr = (t + pltpu.roll(t, rope, axis=1))[:, :rope]
    kr_ref[...] = kr
    krb_ref[...] = kr.astype(krb_ref.dtype)


def _lat_post(p, g_q, g_kv, cs, *, q_lora, kv_lora, rope, name):
    m, w = p.shape
    tm = _tile(m, 512)
    row = lambda i: (i, 0)
    fix = lambda i: (0, 0)
    return pl.pallas_call(
        functools.partial(_lat_post_kernel, q_lora=q_lora, kv_lora=kv_lora, rope=rope),
        out_shape=[jax.ShapeDtypeStruct((m, q_lora), BF16),
                   jax.ShapeDtypeStruct((m, kv_lora), F32),
                   jax.ShapeDtypeStruct((m, kv_lora), BF16),
                   jax.ShapeDtypeStruct((m, rope), F32),
                   jax.ShapeDtypeStruct((m, rope), BF16)],
        grid=(m // tm,),
        in_specs=[pl.BlockSpec((tm, w), row), pl.BlockSpec((1, q_lora), fix),
                  pl.BlockSpec((1, kv_lora), fix), pl.BlockSpec((tm, 2 * rope), row)],
        out_specs=[pl.BlockSpec((tm, q_lora), row), pl.BlockSpec((tm, kv_lora), row),
                   pl.BlockSpec((tm, kv_lora), row), pl.BlockSpec((tm, rope), row),
                   pl.BlockSpec((tm, rope), row)],
        compiler_params=_cparams(("parallel",), 32),
        name=name,
    )(p, g_q.reshape(1, -1), g_kv.reshape(1, -1), cs)


def _mla_q_kernel(cqn_ref, wq_ref, wuk_ref, cs_ref, qlat_ref, qrope_ref, *, nope, rope, scale):
    qh = _dot(cqn_ref[...], wq_ref[...])
    qn = qh[:, :nope].astype(BF16)
    qlat_ref[0] = (_dot(qn, wuk_ref[...]) * scale).astype(qlat_ref.dtype)
    t = qh[:, nope:] * cs_ref[...]
    qr = (t + pltpu.roll(t, rope, axis=1))[:, :rope]
    qrope_ref[0] = (qr * scale).astype(qrope_ref.dtype)


def _mla_q(cqn, wq_cat, wuk_t, cs, *, heads, nope, rope, kv_lora, scale, name):
    m, q_lora = cqn.shape
    tm = _tile(m, 1024)
    hw = nope + 2 * rope
    return pl.pallas_call(
        functools.partial(_mla_q_kernel, nope=nope, rope=rope, scale=scale),
        out_shape=[jax.ShapeDtypeStruct((heads, m, kv_lora), BF16),
                   jax.ShapeDtypeStruct((heads, m, rope), BF16)],
        grid=(m // tm, heads),
        in_specs=[pl.BlockSpec((tm, q_lora), lambda i, h: (i, 0)),
                  pl.BlockSpec((q_lora, hw), lambda i, h: (0, h)),
                  pl.BlockSpec((nope, kv_lora), lambda i, h: (h, 0)),
                  pl.BlockSpec((tm, 2 * rope), lambda i, h: (i, 0))],
        out_specs=[pl.BlockSpec((1, tm, kv_lora), lambda i, h: (h, i, 0)),
                   pl.BlockSpec((1, tm, rope), lambda i, h: (h, i, 0))],
        compiler_params=_cparams(("parallel", "arbitrary"), 32),
        name=name,
    )(cqn, wq_cat, wuk_t, cs)


def _flash_step(qlat_ref, qrope_ref, k, kr, mask, m_ref, l_ref, acc_ref, *, heads, group, tq):
    rows = group * tq
    tk = k.shape[0]

    def scores(g0):
        q = qlat_ref[g0:g0 + group].reshape(rows, qlat_ref.shape[-1])
        qr = qrope_ref[g0:g0 + group].reshape(rows, qrope_ref.shape[-1])
        return _dot_nt(q, k) + _dot_nt(qr, kr)

    s_next = scores(0)
    for g0 in range(0, heads, group):
        s = s_next
        if g0 + group < heads:
            s_next = scores(g0 + group)
        if mask is not None:
            s = jnp.where(mask[None], s.reshape(group, tq, tk), NEG_BIG).reshape(rows, tk)
        sl = slice(g0 * tq, g0 * tq + rows)
        m_prev = m_ref[sl]
        m_new = jnp.maximum(m_prev, jnp.max(s, axis=-1, keepdims=True))
        p = jnp.exp2(s - m_new)
        alpha = jnp.exp2(m_prev - m_new)
        l_ref[sl] = alpha * l_ref[sl] + jnp.sum(p, axis=-1, keepdims=True)
        acc_ref[sl] = alpha * acc_ref[sl] + _dot(p.astype(BF16), k)
        m_ref[sl] = m_new


def _flash_init(m_ref, l_ref, acc_ref):
    m_ref[...] = jnp.full(m_ref.shape, NEG_BIG, F32)
    l_ref[...] = jnp.zeros(l_ref.shape, F32)
    acc_ref[...] = jnp.zeros(acc_ref.shape, F32)


def _flash_finish(wuv_ref, o_ref, l_ref, acc_ref, heads, vh):
    tq = acc_ref.shape[0] // heads
    o = (acc_ref[...] * (1.0 / l_ref[...])).astype(BF16)
    for h in range(heads):
        o_ref[:, h * vh:(h + 1) * vh] = _dot(o[h * tq:(h + 1) * tq], wuv_ref[h]).astype(o_ref.dtype)


def _chunk_mask(q0, k0, tq, tk):
    qc = (q0 + lax.broadcasted_iota(jnp.int32, (tq, tk), 0)) // CHUNK
    kc = (k0 + lax.broadcasted_iota(jnp.int32, (tq, tk), 1)) // CHUNK
    return kc <= qc


def _mla_sample_kernel(qlat_ref, qrope_ref, cckv_ref, ckr_ref, nckv_ref, nkr_ref, wuv_ref, o_ref,
                       m_ref, l_ref, acc_ref, *, heads, group, tq, nkc, past, vh):
    ki = pl.program_id(1)

    @pl.when(ki == 0)
    def _():
        _flash_init(m_ref, l_ref, acc_ref)

    step = functools.partial(_flash_step, qlat_ref, qrope_ref, m_ref=m_ref, l_ref=l_ref,
                             acc_ref=acc_ref, heads=heads, group=group, tq=tq)

    @pl.when(ki < nkc)
    def _():
        step(cckv_ref[0].astype(BF16), ckr_ref[0].astype(BF16), None)

    @pl.when(ki == nkc)
    def _():
        step(nckv_ref[...], nkr_ref[...], _chunk_mask(past, past, tq, tq))
        _flash_finish(wuv_ref, o_ref, l_ref, acc_ref, heads, vh)


def _mla_sample(qlat, qrope, cache_ckv, cache_kr, ckv, krope, wuv, *, name):
    heads, _, c = qlat.shape
    rope = qrope.shape[-1]
    vh = wuv.shape[-1]
    batch, past, _ = cache_ckv.shape
    tq = ckv.shape[0] // batch
    tk = _tile(past, 1024)
    nkc = past // tk
    cmap = lambda b, ki: (b, jnp.minimum(ki, nkc - 1), 0)
    return pl.pallas_call(
        functools.partial(_mla_sample_kernel, heads=heads, group=_MLA_GROUP, tq=tq, nkc=nkc,
                          past=past, vh=vh),
        out_shape=jax.ShapeDtypeStruct((batch * tq, heads * vh), BF16),
        grid=(batch, nkc + 1),
        in_specs=[pl.BlockSpec((heads, tq, c), lambda b, ki: (0, b, 0)),
                  pl.BlockSpec((heads, tq, rope), lambda b, ki: (0, b, 0)),
                  pl.BlockSpec((1, tk, c), cmap),
                  pl.BlockSpec((1, tk, rope), cmap),
                  pl.BlockSpec((tq, c), lambda b, ki: (b, 0)),
                  pl.BlockSpec((tq, rope), lambda b, ki: (b, 0)),
                  pl.BlockSpec((heads, c, vh), lambda b, ki: (0, 0, 0))],
        out_specs=pl.BlockSpec((tq, heads * vh), lambda b, ki: (b, 0)),
        scratch_shapes=[pltpu.VMEM((heads * tq, 1), F32), pltpu.VMEM((heads * tq, 1), F32),
                        pltpu.VMEM((heads * tq, c), F32)],
        compiler_params=_cparams(("parallel", "arbitrary"), 48),
        name=name,
    )(qlat, qrope, cache_ckv, cache_kr, ckv, krope, wuv)


def _mla_kv_up_kernel(ckv_ref, kr_ref, wuk_ref, wuv_ref, kcat_ref, v_ref, *, hpt, nope, vh):
    c = ckv_ref[...]
    k = _dot(c, wuk_ref[...])
    v = _dot(c, wuv_ref[...])
    kr = kr_ref[...]
    for hh in range(hpt):
        kcat_ref[hh, :, :nope] = k[:, hh * nope:(hh + 1) * nope].astype(kcat_ref.dtype)
        kcat_ref[hh, :, nope:] = kr
        v_ref[hh] = v[:, hh * vh:(hh + 1) * vh].astype(v_ref.dtype)


def _mla_kv_up(ckv_b, krope_b, wuk_flat, wuv_flat, *, heads, name):
    m, c = ckv_b.shape
    rope = krope_b.shape[1]
    nope, vh = wuk_flat.shape[1] // heads, wuv_flat.shape[1] // heads
    hpt = _tile(heads, 4)
    tm = _tile(m, 1024)
    return pl.pallas_call(
        functools.partial(_mla_kv_up_kernel, hpt=hpt, nope=nope, vh=vh),
        out_shape=[jax.ShapeDtypeStruct((heads, m, nope + rope), BF16),
                   jax.ShapeDtypeStruct((heads, m, vh), BF16)],
        grid=(m // tm, heads // hpt),
        in_specs=[pl.BlockSpec((tm, c), lambda i, j: (i, 0)),
                  pl.BlockSpec((tm, rope), lambda i, j: (i, 0)),
                  pl.BlockSpec((c, hpt * nope), lambda i, j: (0, j)),
                  pl.BlockSpec((c, hpt * vh), lambda i, j: (0, j))],
        out_specs=[pl.BlockSpec((hpt, tm, nope + rope), lambda i, j: (j, i, 0)),
                   pl.BlockSpec((hpt, tm, vh), lambda i, j: (j, i, 0))],
        compiler_params=_cparams(("parallel", "arbitrary"), 32),
        name=name,
    )(ckv_b, krope_b, wuk_flat, wuv_flat)


def _mla_qcat_kernel(cqn_ref, wq_ref, cs_ref, qcat_ref, *, hpt, nope, rope, scale):
    qh = _dot(cqn_ref[...], wq_ref[...])
    hw = nope + 2 * rope
    cs = cs_ref[...]
    for hh in range(hpt):
        qcat_ref[hh, :, :nope] = (qh[:, hh * hw:hh * hw + nope] * scale).astype(qcat_ref.dtype)
        t = qh[:, hh * hw + nope:(hh + 1) * hw] * cs
        qr = (t + pltpu.roll(t, rope, axis=1))[:, :rope]
        qcat_ref[hh, :, nope:] = (qr * scale).astype(qcat_ref.dtype)


def _mla_qcat(cqn, wq_cat, cs, *, heads, nope, rope, scale, name):
    m, q_lora = cqn.shape
    hw = nope + 2 * rope
    hpt = _tile(heads, 4)
    tm = _tile(m, 1024)
    return pl.pallas_call(
        functools.partial(_mla_qcat_kernel, hpt=hpt, nope=nope, rope=rope, scale=scale),
        out_shape=jax.ShapeDtypeStruct((heads, m, nope + rope), BF16),
        grid=(m // tm, heads // hpt),
        in_specs=[pl.BlockSpec((tm, q_lora), lambda i, j: (i, 0)),
                  pl.BlockSpec((q_lora, hpt * hw), lambda i, j: (0, j)),
                  pl.BlockSpec((tm, 2 * rope), lambda i, j: (i, 0))],
        out_specs=pl.BlockSpec((hpt, tm, nope + rope), lambda i, j: (j, i, 0)),
        compiler_params=_cparams(("parallel", "arbitrary"), 32),
        name=name,
    )(cqn, wq_cat, cs)


def _mha_step(q_ref, k_ref, v_ref, mask, m_ref, acc_ref, *, heads, group, tq):
    rows = group * tq
    tk = k_ref.shape[1]
    ones = jnp.ones((tk, v_ref.shape[-1]), BF16)

    def scores(g0):
        return jnp.concatenate([_dot_nt(q_ref[g0 + i], k_ref[g0 + i]) for i in range(group)], axis=0)

    s_next = scores(0)
    for g0 in range(0, heads, group):
        s = s_next
        if g0 + group < heads:
            s_next = scores(g0 + group)
        if mask is not None:
            s = jnp.where(mask[None], s.reshape(group, tq, tk), NEG_BIG).reshape(rows, tk)
        sl = slice(g0 * tq, g0 * tq + rows)
        m_prev = m_ref[sl]
        m_new = jnp.maximum(m_prev, jnp.max(s, axis=-1, keepdims=True))
        p = jnp.exp2(s - m_new)
        alpha = jnp.exp2(m_prev - m_new)
        p = p.astype(BF16)
        pv = jnp.concatenate(
            [_dot(p[i * tq:(i + 1) * tq], jnp.concatenate([v_ref[g0 + i], ones], axis=1))
             for i in range(group)], axis=0)
        acc_ref[sl] = alpha * acc_ref[sl] + pv
        m_ref[sl] = m_new


def _mha_prompt_kernel(q_ref, k_ref, v_ref, o_ref, m_ref, acc_ref, *, heads, group, tq, tk, vh):
    qi, ki = pl.program_id(1), pl.program_id(2)
    k_last = ((qi + 1) * tq - 1) // tk
    partial = (ki + 1) * tk > qi * tq + CHUNK
    step = functools.partial(_mha_step, q_ref, k_ref, v_ref, m_ref=m_ref, acc_ref=acc_ref,
                             heads=heads, group=group, tq=tq)

    @pl.when(ki == 0)
    def _():
        m_ref[...] = jnp.full(m_ref.shape, NEG_BIG, F32)
        acc_ref[...] = jnp.zeros(acc_ref.shape, F32)

    @pl.when((ki <= k_last) & jnp.logical_not(partial))
    def _():
        step(None)

    @pl.when((ki <= k_last) & partial)
    def _():
        step(_chunk_mask(qi * tq, ki * tk, tq, tk))

    @pl.when(ki == k_last)
    def _():
        o = acc_ref[:, :vh] * (1.0 / acc_ref[:, vh:])
        for h in range(heads):
            o_ref[:, h * vh:(h + 1) * vh] = o[h * tq:(h + 1) * tq].astype(o_ref.dtype)


def _mha_prompt(qcat, kcat, v, *, batch, seq, name):
    heads, _, dk = qcat.shape
    vh = v.shape[-1]
    tq = _tile(seq, _MLA_TQ)
    tk = _tile(seq, _MLA_TK)
    nq, nk = seq // tq, seq // tk

    def kmap(b, qi, ki):
        return (0, b * nk + jnp.minimum(ki, ((qi + 1) * tq - 1) // tk), 0)

    return pl.pallas_call(
        functools.partial(_mha_prompt_kernel, heads=heads, group=_MHA_GROUP, tq=tq, tk=tk, vh=vh),
        out_shape=jax.ShapeDtypeStruct((batch * seq, heads * vh), BF16),
        grid=(batch, nq, nk),
        in_specs=[pl.BlockSpec((heads, tq, dk), lambda b, qi, ki: (0, b * nq + qi, 0)),
                  pl.BlockSpec((heads, tk, dk), kmap),
                  pl.BlockSpec((heads, tk, vh), kmap)],
        out_specs=pl.BlockSpec((tq, heads * vh), lambda b, qi, ki: (b * nq + qi, 0)),
        scratch_shapes=[pltpu.VMEM((heads * tq, 1), F32), pltpu.VMEM((heads * tq, 2 * vh), F32)],
        compiler_params=_cparams(("parallel", "parallel", "arbitrary"), 56),
        name=name,
    )(qcat, kcat, v)


SB_SKIP = -160.0


def _sb_block(q_ref, get_kv, tri, valid, run_ref, acc_ref, *, heads, group, tq):
    rows = group * tq
    for g0 in range(0, heads, group):
        kv = [get_kv(h) for h in range(g0, g0 + group)]
        z = jnp.concatenate([_dot_nt(q_ref[g0 + i], kv[i][0]) for i in range(group)], axis=0)
        tk = z.shape[-1]
        sp = jnp.log2(1.0 + jnp.exp2(-jnp.abs(z)))
        lk = -(jnp.maximum(z, 0.0) + sp)
        if valid is not None:
            lk = jnp.where(valid[None], lk.reshape(group, tq, tk), 0.0).reshape(rows, tk)
        hi = lk.astype(BF16)
        lo = (lk - hi.astype(F32)).astype(BF16)
        suffix = _dot(hi, tri) + _dot(lo, tri)
        run = run_ref[g0 * tq:g0 * tq + rows]
        w = jnp.exp2((jnp.minimum(z, 0.0) - sp) + suffix + run)
        if valid is not None:
            w = jnp.where(valid[None], w.reshape(group, tq, tk), 0.0).reshape(rows, tk)
        w = w.astype(BF16)
        run_ref[g0 * tq:g0 * tq + rows] = run + suffix[:, :1] + lk[:, :1]
        for i in range(group):
            r0 = (g0 + i) * tq
            acc_ref[r0:r0 + tq] = acc_ref[r0:r0 + tq] + _dot(w[i * tq:(i + 1) * tq], kv[i][1])


def _strict_lower(n):
    row = lax.broadcasted_iota(jnp.int32, (n, n), 0)
    col = lax.broadcasted_iota(jnp.int32, (n, n), 1)
    return col < row


def _all_below(run_ref, bound):
    m = jnp.max(run_ref[...], axis=0, keepdims=True)
    return m[0, 0] <= bound


def _sb_write(o_ref, acc_ref, heads, tq, hd):
    for h in range(heads):
        o_ref[:, h * hd:(h + 1) * hd] = acc_ref[h * tq:(h + 1) * tq].astype(o_ref.dtype)


def _sb_prompt_kernel(q_ref, kd_ref, vd_ref, tri_ref, k_hbm, v_hbm, o_ref, kbuf, vbuf, sem,
                      run_ref, acc_ref, *, heads, group, tq, hd, nq):
    b, qi = pl.program_id(0), pl.program_id(1)

    def copies(j, slot):
        row0 = pl.multiple_of((b * nq + j) * tq, tq)
        return (pltpu.make_async_copy(k_hbm.at[:, pl.ds(row0, tq), :], kbuf.at[slot], sem.at[0, slot]),
                pltpu.make_async_copy(v_hbm.at[:, pl.ds(row0, tq), :], vbuf.at[slot], sem.at[1, slot]))

    @pl.when(qi > 0)
    def _():
        for cp in copies(qi - 1, 0):
            cp.start()

    run_ref[...] = jnp.zeros(run_ref.shape, F32)
    acc_ref[...] = jnp.zeros(acc_ref.shape, F32)
    tri = tri_ref[...]
    blk = functools.partial(_sb_block, run_ref=run_ref, acc_ref=acc_ref, heads=heads, group=group,
                            tq=tq)
    blk(q_ref, lambda h: (kd_ref[h], vd_ref[h]), tri, _strict_lower(tq))

    def cond(c):
        return (c[0] >= 0) & (c[1] > 0)

    def body(c):
        j = c[0]
        slot = lax.rem(qi - 1 - j, 2)
        for cp in copies(j, slot):
            cp.wait()

        @pl.when(j > 0)
        def _():
            for cp in copies(j - 1, 1 - slot):
                cp.start()

        blk(q_ref, lambda h: (kbuf[slot, h], vbuf[slot, h]), tri, None)
        go = jnp.where(_all_below(run_ref, SB_SKIP), 0, 1).astype(jnp.int32)
        return (j - 1, go)

    j_end, _ = lax.while_loop(cond, body, (qi - 1, jnp.int32(1)))

    @pl.when(j_end >= 0)
    def _():
        for cp in copies(j_end, lax.rem(qi - 1 - j_end, 2)):
            cp.wait()

    _sb_write(o_ref, acc_ref, heads, tq, hd)


def _sb_prompt(q, k, v, tri, *, batch, seq, name):
    heads, _, hd = q.shape
    tq = tri.shape[0]
    nq = seq // tq
    blk = lambda b, qi: (0, b * nq + qi, 0)
    return pl.pallas_call(
        functools.partial(_sb_prompt_kernel, heads=heads, group=_SB_GROUP, tq=tq, hd=hd, nq=nq),
        out_shape=jax.ShapeDtypeStruct((batch * seq, heads * hd), BF16),
        grid=(batch, nq),
        in_specs=[pl.BlockSpec((heads, tq, hd), blk), pl.BlockSpec((heads, tq, hd), blk),
                  pl.BlockSpec((heads, tq, hd), blk),
                  pl.BlockSpec((tq, tq), lambda b, qi: (0, 0)),
                  pl.BlockSpec(memory_space=pl.ANY), pl.BlockSpec(memory_space=pl.ANY)],
        out_specs=pl.BlockSpec((tq, heads * hd), lambda b, qi: (b * nq + qi, 0)),
        scratch_shapes=[pltpu.VMEM((2, heads, tq, hd), BF16), pltpu.VMEM((2, heads, tq, hd), BF16),
                        pltpu.SemaphoreType.DMA((2, 2)),
                        pltpu.VMEM((heads * tq, 1), F32), pltpu.VMEM((heads * tq, hd), F32)],
        compiler_params=_cparams(("arbitrary", "arbitrary"), 40),
        name=name,
    )(q, k, v, tri, k, v)


def _sb_sample_kernel(q_ref, nk_ref, nv_ref, tri_ref, ck_hbm, cv_hbm, o_ref, kbuf, vbuf, sem,
                      run_ref, acc_ref, *, heads, group, tq, tk, hd, nkc):
    b = pl.program_id(0)

    def copies(j, slot):
        p0 = pl.multiple_of(j * tk, tk)
        return (pltpu.make_async_copy(ck_hbm.at[b, pl.ds(p0, tk)], kbuf.at[slot], sem.at[0, slot]),
                pltpu.make_async_copy(cv_hbm.at[b, pl.ds(p0, tk)], vbuf.at[slot], sem.at[1, slot]))

    for cp in copies(nkc - 1, 0):
        cp.start()

    run_ref[...] = jnp.zeros(run_ref.shape, F32)
    acc_ref[...] = jnp.zeros(acc_ref.shape, F32)
    blk = functools.partial(_sb_block, run_ref=run_ref, acc_ref=acc_ref, heads=heads, group=group,
                            tq=tq)
    blk(q_ref, lambda h: (nk_ref[h], nv_ref[h]), tri_ref[:tq, :tq], _strict_lower(tq))
    tri = tri_ref[...]

    def cond(c):
        return (c[0] >= 0) & (c[1] > 0)

    def body(c):
        j = c[0]
        slot = lax.rem(nkc - 1 - j, 2)
        for cp in copies(j, slot):
            cp.wait()

        @pl.when(j > 0)
        def _():
            for cp in copies(j - 1, 1 - slot):
                cp.start()

        blk(q_ref, lambda h: (kbuf[slot, :, h, :].astype(BF16), vbuf[slot, :, h, :].astype(BF16)),
            tri, None)
        go = jnp.where(_all_below(run_ref, SB_SKIP), 0, 1).astype(jnp.int32)
        return (j - 1, go)

    j_end, _ = lax.while_loop(cond, body, (jnp.int32(nkc - 1), jnp.int32(1)))

    @pl.when(j_end >= 0)
    def _():
        for cp in copies(j_end, lax.rem(nkc - 1 - j_end, 2)):
            cp.wait()

    _sb_write(o_ref, acc_ref, heads, tq, hd)


def _sb_sample(q, k_new, v_new, cache_k, cache_v, tri, *, name):
    heads, rows, hd = q.shape
    batch, past = cache_k.shape[:2]
    tq = rows // batch
    tk = tri.shape[0]
    nkc = past // tk
    new = lambda b: (0, b, 0)
    return pl.pallas_call(
        functools.partial(_sb_sample_kernel, heads=heads, group=_SB_GROUP, tq=tq, tk=tk, hd=hd,
                          nkc=nkc),
        out_shape=jax.ShapeDtypeStruct((rows, heads * hd), BF16),
        grid=(batch,),
        in_specs=[pl.BlockSpec((heads, tq, hd), new), pl.BlockSpec((heads, tq, hd), new),
                  pl.BlockSpec((heads, tq, hd), new),
                  pl.BlockSpec((tk, tk), lambda b: (0, 0)),
                  pl.BlockSpec(memory_space=pl.ANY), pl.BlockSpec(memory_space=pl.ANY)],
        out_specs=pl.BlockSpec((tq, heads * hd), lambda b: (b, 0)),
        scratch_shapes=[pltpu.VMEM((2, tk, heads, hd), F32), pltpu.VMEM((2, tk, heads, hd), F32),
                        pltpu.SemaphoreType.DMA((2, 2)),
                        pltpu.VMEM((heads * tq, 1), F32), pltpu.VMEM((heads * tq, hd), F32)],
        compiler_params=_cparams(("arbitrary",), 40),
        name=name,
    )(q, k_new, v_new, tri, cache_k, cache_v)


def _mem_attn_kernel(x_ref, g_ref, wq_ref, mk_ref, mv_ref, wo_ref, gn_ref, xo_ref, hn_ref, o_scr,
                     *, nsub, sub, heads, hd, scale):
    x = x_ref[...]
    mq = (_dot(_rms(x, g_ref[...]).astype(BF16), wq_ref[...]) * scale).astype(BF16)
    for s in range(nsub):
        for h in range(heads):
            q = mq[s * sub:(s + 1) * sub, h * hd:(h + 1) * hd]
            k = mk_ref[s, :, h * hd:(h + 1) * hd]
            v = mv_ref[s, :, h * hd:(h + 1) * hd]
            sc = _dot_nt(q, k)
            p = jnp.exp(sc - jnp.max(sc, axis=-1, keepdims=True))
            p = p * (1.0 / jnp.sum(p, axis=-1, keepdims=True))
            o_scr[s * sub:(s + 1) * sub, h * hd:(h + 1) * hd] = _dot(p.astype(BF16), v).astype(BF16)
    xn = x + _dot(o_scr[...], wo_ref[...])
    xo_ref[...] = xn
    hn_ref[...] = _rms(xn, gn_ref[...]).astype(hn_ref.dtype)


def _mem_attn(x, g, w_mq, mem_k, mem_v, w_mo, g_next, *, sub, heads, name):
    m, d = x.shape
    nb, n_mem, width = mem_k.shape
    hd = width // heads
    tm = _tile(m, 256)
    if sub >= tm:
        nsub, rows = 1, tm
        per = sub // tm
        mmap = lambda i: (i // per, 0, 0)
    else:
        nsub, rows = tm // sub, sub
        mmap = lambda i: (i, 0, 0)
    row = lambda i: (i, 0)
    fix = lambda i: (0, 0)
    return pl.pallas_call(
        functools.partial(_mem_attn_kernel, nsub=nsub, sub=rows, heads=heads, hd=hd,
                          scale=hd ** -0.5),
        out_shape=[jax.ShapeDtypeStruct((m, d), F32), jax.ShapeDtypeStruct((m, d), BF16)],
        grid=(m // tm,),
        in_specs=[pl.BlockSpec((tm, d), row), pl.BlockSpec((1, d), fix),
                  pl.BlockSpec((d, width), fix),
                  pl.BlockSpec((nsub, n_mem, width), mmap),
                  pl.BlockSpec((nsub, n_mem, width), mmap),
                  pl.BlockSpec((width, d), fix), pl.BlockSpec((1, d), fix)],
        out_specs=[pl.BlockSpec((tm, d), row), pl.BlockSpec((tm, d), row)],
        scratch_shapes=[pltpu.VMEM((tm, width), BF16)],
        compiler_params=_cparams(("parallel",), 48),
        name=name,
    )(x, g.reshape(1, d), w_mq, mem_k, mem_v, w_mo, g_next.reshape(1, d))


def _rotate_half_rows(w):
    half = w.shape[0] // 2
    return jnp.concatenate([-w[half:], w[:half]], axis=0)


def _rotate_half_cols(w):
    half = w.shape[-1] // 2
    return jnp.concatenate([-w[..., half:], w[..., :half]], axis=-1)


def _rope_table(pos, half):
    inv = ROPE_THETA ** (-jnp.arange(half, dtype=F32) / half)
    ang = pos.astype(F32)[:, None] * inv[None, :]
    c, s = jnp.cos(ang), jnp.sin(ang)
    return jnp.concatenate([c, c, s, s], axis=-1)


def _prepare_weights(w_in, w_uq, w_uk, w_uv, w_branch_a, w_branch_b, w_out, w_mq, w_mk, w_mv, w_mo,
                     w_gate, w_up, w_down, dims):
    q_lora, kv_lora, rope, heads, nope = dims
    o_kr = q_lora + kv_lora
    o_sb = o_kr + rope
    w_in_t = jnp.swapaxes(w_in, 0, 1)
    w_lat = jnp.concatenate([w_in_t[:o_sb], _rotate_half_rows(w_in_t[o_kr:o_sb])], axis=0).astype(BF16)
    w_rest = w_in_t[o_sb:].astype(BF16)
    wq_cat = jnp.concatenate([w_uq, _rotate_half_cols(w_uq[..., nope:])], axis=-1)
    return dict(
        w_lat=w_lat,
        w_rest=w_rest,
        wq_cat=wq_cat.reshape(q_lora, -1).astype(BF16),
        wuk_t=jnp.transpose(w_uk, (1, 2, 0)).reshape(heads * nope, kv_lora).astype(BF16),
        wuv=jnp.transpose(w_uv, (1, 0, 2)).astype(BF16),
        wuk_flat=w_uk.reshape(kv_lora, -1).astype(BF16),
        wuv_flat=w_uv.reshape(kv_lora, -1).astype(BF16),
        w_ba=w_branch_a, w_bb=w_branch_b, w_out=w_out,
        w_mq=w_mq.astype(BF16), w_mk=w_mk.astype(BF16), w_mv=w_mv.astype(BF16),
        w_mo=w_mo.astype(BF16),
        w_gate=w_gate, w_up=w_up, w_down=w_down.astype(BF16),
    )


def _layer(x, pos, past, mem_k, mem_v, w, gains, b_gate, dims, *, batch, tag):
    g_mix, g_q_lat, g_kv_lat, g_xattn, g_ffn = gains
    q_lora, kv_lora, rope, heads, nope = dims
    m, d = x.shape
    t = m // batch
    sb_width = (w["w_rest"].shape[0] - 2 * d) // 3
    sb_heads = sb_width // LANE
    mla_scale = (nope + rope) ** -0.5 * LOG2E
    sb_scale = LANE ** -0.5 * LOG2E
    tm = _tile(m, 1024)
    tn = 512

    h = _rmsnorm(x, g_mix, BF16, f"{tag}_norm_mix")
    lat_w = w["w_lat"].shape[0]
    (p_lat,) = _fused_matmul(
        [h], [(0, w["w_lat"], 0)], [], [(F32, "tile")], lambda accs, ex: (accs[0],),
        n=lat_w, tm=_tile(m, 256), tn=lat_w, name=f"{tag}_proj_lat", w_rows=True)
    cs = jnp.tile(_rope_table(pos, rope // 2), (batch, 1))
    cqn, ckv, ckv_b, krope, krope_b = _lat_post(p_lat, g_q_lat, g_kv_lat, cs, q_lora=q_lora,
                                                kv_lora=kv_lora, rope=rope, name=f"{tag}_lat_post")
    nsb = sb_width // tn
    (sbq,) = _fused_matmul(
        [h], [(0, w["w_rest"], 0)], [], [(BF16, "heads")],
        lambda accs, ex: (accs[0] * sb_scale,), n=sb_width, tm=tm, tn=tn, name=f"{tag}_proj_sbq",
        w_rows=True)
    sbk, sbk_b = _fused_matmul(
        [h], [(0, w["w_rest"], nsb)], [], [(F32, "tile"), (BF16, "heads")],
        lambda accs, ex: (accs[0], accs[0]), n=sb_width, tm=tm, tn=tn, name=f"{tag}_proj_sbk",
        w_rows=True)
    sbv, sbv_b = _fused_matmul(
        [h], [(0, w["w_rest"], 2 * nsb)], [], [(F32, "tile"), (BF16, "heads")],
        lambda accs, ex: (accs[0], accs[0]), n=sb_width, tm=tm, tn=tn, name=f"{tag}_proj_sbv",
        w_rows=True)
    (gates,) = _fused_matmul(
        [h], [(0, w["w_rest"], 3 * nsb)], [(b_gate.reshape(1, -1), "row", 0)], [(BF16, "tile")],
        lambda accs, ex: (_sigmoid(accs[0] + ex[0]),), n=2 * d, tm=tm, tn=tn,
        name=f"{tag}_proj_gates", w_rows=True)

    tri_n = 256 if t % 256 == 0 else t
    tri = (jnp.arange(tri_n)[:, None] > jnp.arange(tri_n)[None, :]).astype(BF16)
    if past is None:
        qcat = _mla_qcat(cqn, w["wq_cat"], cs, heads=heads, nope=nope, rope=rope, scale=mla_scale,
                         name=f"{tag}_mla_q")
        kcat, vmla = _mla_kv_up(ckv_b, krope_b, w["wuk_flat"], w["wuv_flat"], heads=heads,
                                name=f"{tag}_mla_kv")
        o_a = _mha_prompt(qcat, kcat, vmla, batch=batch, seq=t, name=f"{tag}_mla_attn")
        o_b = _sb_prompt(sbq, sbk_b, sbv_b, tri, batch=batch, seq=t, name=f"{tag}_sb_attn")
    else:
        c_ckv, c_kr, c_k, c_v = past
        qlat, qrope = _mla_q(cqn, w["wq_cat"], w["wuk_t"], cs, heads=heads, nope=nope, rope=rope,
                             kv_lora=kv_lora, scale=mla_scale, name=f"{tag}_mla_q")
        o_a = _mla_sample(qlat, qrope, c_ckv, c_kr, ckv_b, krope_b, w["wuv"], name=f"{tag}_mla_attn")
        tri = (jnp.arange(256)[:, None] > jnp.arange(256)[None, :]).astype(BF16)
        o_b = _sb_sample(sbq, sbk_b, sbv_b, c_k, c_v, tri, name=f"{tag}_sb_attn")

    ng = d // tn
    (merged,) = _fused_matmul(
        [o_a, o_b], [(0, w["w_ba"], 0), (1, w["w_bb"], 0)],
        [(gates, "tile", 0), (gates, "tile", ng)], [(BF16, "tile")],
        lambda accs, ex: (ex[0].astype(F32) * accs[0] + ex[1].astype(F32) * accs[1],),
        n=d, tm=tm, tn=tn, name=f"{tag}_merge", vmem_mib=56)
    (x,) = _fused_matmul(
        [merged], [(0, w["w_out"], 0)], [(x, "tile", 0)], [(F32, "tile")],
        lambda accs, ex: (ex[0] + accs[0],), n=d, tm=tm, tn=tn, name=f"{tag}_out_proj", vmem_mib=56)

    mem_heads = mem_k.shape[2]
    mk = mem_k.reshape(mem_k.shape[0], mem_k.shape[1], -1).astype(BF16)
    mv = mem_v.reshape(mem_v.shape[0], mem_v.shape[1], -1).astype(BF16)
    x, hf = _mem_attn(x, g_xattn, w["w_mq"], mk, mv, w["w_mo"], g_ffn, sub=t, heads=mem_heads,
                      name=f"{tag}_mem_attn")

    d_ff = w["w_gate"].shape[1]
    tn_ff = _tile(d_ff, 256) if d_ff % 512 else 512
    (act,) = _fused_matmul(
        [hf], [(0, w["w_gate"], 0), (0, w["w_up"], 0)], [], [(BF16, "tile")],
        lambda accs, ex: (accs[0] * _sigmoid(accs[0]) * accs[1],), n=d_ff, tm=tm, tn=tn_ff,
        name=f"{tag}_ffn_up", vmem_mib=56)
    (x,) = _fused_matmul(
        [act], [(0, w["w_down"], 0)], [(x, "tile", 0)], [(F32, "tile")],
        lambda accs, ex: (ex[0] + accs[0],), n=d, tm=_tile(m, 512), tn=512, name=f"{tag}_ffn_down",
        vmem_mib=56)
    return x, (ckv, krope, sbk, sbv)


def kernel(x_prompt, x_sample, cache_mla_ckv, cache_mla_krope, cache_sb_k, cache_sb_v, cache_mem_k, cache_mem_v, mem_prompt, g_mix, w_in, b_gate, g_q_lat, w_uq, g_kv_lat, w_uk, w_uv, w_branch_a, w_branch_b, w_out, g_xattn, g_mem, w_mq, w_mk, w_mv, w_mo, g_ffn, w_gate, w_up, w_down, g_final):
    depth = w_in.shape[0]
    bp, seq, d = x_prompt.shape
    bs, dec, _ = x_sample.shape
    past_len = cache_mla_ckv.shape[2]
    q_lora, heads, qk = w_uq.shape[1:]
    kv_lora, _, nope = w_uk.shape[1:]
    rope = qk - nope
    dims = (q_lora, kv_lora, rope, heads, nope)
    sb_heads, sb_hd = cache_sb_k.shape[3:]
    n_mem, mem_heads, mem_hd = cache_mem_k.shape[2:]
    pos_p = jnp.arange(seq)
    pos_s = past_len + jnp.arange(dec)

    xp = x_prompt.reshape(bp * seq, d)
    xs = x_sample.reshape(bs * dec, d)
    outs = [[] for _ in range(10)]
    for l in range(depth):
        w = _prepare_weights(w_in[l], w_uq[l], w_uk[l], w_uv[l], w_branch_a[l], w_branch_b[l],
                             w_out[l], w_mq[l], w_mk[l], w_mv[l], w_mo[l], w_gate[l], w_up[l],
                             w_down[l], dims)
        gains = (g_mix[l], g_q_lat[l], g_kv_lat[l], g_xattn[l], g_ffn[l])
        mn = _rmsnorm(mem_prompt.reshape(bp * n_mem, d), g_mem[l], BF16, f"l{l}_norm_mem")
        mem_w = w["w_mk"].shape[1]
        mk, mv = _fused_matmul(
            [mn], [(0, w["w_mk"], 0), (0, w["w_mv"], 0)], [], [(F32, "tile"), (F32, "tile")],
            lambda accs, ex: (accs[0], accs[1]), n=mem_w, tm=_tile(bp * n_mem, 512),
            tn=_tile(mem_w, 512), name=f"l{l}_mem_kv")
        mk = mk.reshape(bp, n_mem, mem_heads, mem_hd)
        mv = mv.reshape(bp, n_mem, mem_heads, mem_hd)
        xp, (ckv, kr, k, v) = _layer(xp, pos_p, None, mk, mv, w, gains, b_gate[l], dims,
                                     batch=bp, tag=f"l{l}p")
        for lst, val in zip(outs[:6], (ckv.reshape(bp, seq, -1), kr.reshape(bp, seq, -1),
                                       k.reshape(bp, seq, sb_heads, sb_hd),
                                       v.reshape(bp, seq, sb_heads, sb_hd), mk, mv)):
            lst.append(val)
        past = (cache_mla_ckv[l], cache_mla_krope[l], cache_sb_k[l], cache_sb_v[l])
        xs, (ckv, kr, k, v) = _layer(xs, pos_s, past, cache_mem_k[l], cache_mem_v[l], w, gains,
                                     b_gate[l], dims, batch=bs, tag=f"l{l}s")
        for lst, val in zip(outs[6:], (ckv.reshape(bs, dec, -1), kr.reshape(bs, dec, -1),
                                       k.reshape(bs, dec, sb_heads, sb_hd),
                                       v.reshape(bs, dec, sb_heads, sb_hd))):
            lst.append(val)
    y_prompt = _rmsnorm(xp, g_final, F32, "final_norm_p").reshape(bp, seq, d)
    y_sample = _rmsnorm(xs, g_final, F32, "final_norm_s").reshape(bs, dec, d)
    return (y_prompt, y_sample) + tuple(jnp.stack(o) for o in outs)
```

```python
import functools
import math

import jax
import jax.numpy as jnp
from jax import lax
from jax.experimental import pallas as pl
from jax.experimental.pallas import tpu as pltpu

F32 = jnp.float32
BF16 = jnp.bfloat16

CHUNK = 64
EPS = 1e-6
ROPE_THETA = 10000.0
NEG_BIG = -1e30
LOG2E = math.log2(math.e)
MIB = 1024 * 1024
LANE = 128
_SB_GROUP = 4
_MLA_GROUP = 2
_MHA_GROUP = 2
_MLA_TQ = 256
_MLA_TK = 1024


def _cparams(sem, vmem_mib):
    return pltpu.CompilerParams(dimension_semantics=sem, vmem_limit_bytes=vmem_mib * MIB)


def _dot(a, b):
    return jnp.dot(a, b, preferred_element_type=F32)


def _dot_nt(a, b):
    return lax.dot_general(a, b, (((1,), (1,)), ((), ())), preferred_element_type=F32)


def _sigmoid(x):
    return 1.0 / (1.0 + jnp.exp(-x))


def _rms(x, g):
    return x * lax.rsqrt(jnp.mean(x * x, axis=-1, keepdims=True) + EPS) * g


def _tile(n, pref):
    if n <= pref:
        return n
    t = pref
    while n % t:
        t //= 2
    return t


def _norm_kernel(x_ref, g_ref, o_ref):
    o_ref[...] = _rms(x_ref[...], g_ref[...]).astype(o_ref.dtype)


def _rmsnorm(x, g, out_dtype, name):
    m, d = x.shape
    tm = _tile(m, 256)
    return pl.pallas_call(
        _norm_kernel,
        out_shape=jax.ShapeDtypeStruct((m, d), out_dtype),
        grid=(m // tm,),
        in_specs=[pl.BlockSpec((tm, d), lambda i: (i, 0)),
                  pl.BlockSpec((1, d), lambda i: (0, 0))],
        out_specs=pl.BlockSpec((tm, d), lambda i: (i, 0)),
        compiler_params=_cparams(("parallel",), 40),
        name=name,
    )(x, g.reshape(1, d))


def _fused_matmul(lhs, dots, extras, outs, epilogue, *, n, tm, tn, name, vmem_mib=48, w_rows=False):
    m = lhs[0].shape[0]
    na, nd, ne = len(lhs), len(dots), len(extras)
    hpt = tn // LANE

    def kernel(*refs):
        a_refs, w_refs = refs[:na], refs[na:na + nd]
        e_refs = refs[na + nd:na + nd + ne]
        o_refs = refs[na + nd + ne:]
        mm = _dot_nt if w_rows else _dot
        accs = [mm(a_refs[k][...], w[...].astype(BF16)) for (k, _, _), w in zip(dots, w_refs)]
        vals = epilogue(accs, [e[...] for e in e_refs])
        for o_ref, v, (_, kind) in zip(o_refs, vals, outs):
            if kind == "tile":
                o_ref[...] = v.astype(o_ref.dtype)
            else:
                for hh in range(hpt):
                    o_ref[hh] = v[:, hh * LANE:(hh + 1) * LANE].astype(o_ref.dtype)

    in_specs, args = [], []
    for a in lhs:
        in_specs.append(pl.BlockSpec((tm, a.shape[1]), lambda i, j: (i, 0)))
        args.append(a)
    for _, w, off in dots:
        if w_rows:
            in_specs.append(pl.BlockSpec((tn, w.shape[1]), lambda i, j, off=off: (j + off, 0)))
        else:
            in_specs.append(pl.BlockSpec((w.shape[0], tn), lambda i, j, off=off: (0, j + off)))
        args.append(w)
    for e, kind, off in extras:
        if kind == "row":
            in_specs.append(pl.BlockSpec((1, tn), lambda i, j, off=off: (0, j + off)))
        else:
            in_specs.append(pl.BlockSpec((tm, tn), lambda i, j, off=off: (i, j + off)))
        args.append(e)
    out_shape, out_specs = [], []
    for dt, kind in outs:
        if kind == "tile":
            out_shape.append(jax.ShapeDtypeStruct((m, n), dt))
            out_specs.append(pl.BlockSpec((tm, tn), lambda i, j: (i, j)))
        else:
            out_shape.append(jax.ShapeDtypeStruct((n // LANE, m, LANE), dt))
            out_specs.append(pl.BlockSpec((hpt, tm, LANE), lambda i, j: (j, i, 0)))
    return pl.pallas_call(
        kernel,
        out_shape=out_shape,
        grid=(m // tm, n // tn),
        in_specs=in_specs,
        out_specs=out_specs,
        compiler_params=_cparams(("parallel", "arbitrary"), vmem_mib),
        name=name,
    )(*args)


def _lat_proj_kernel(h_ref, w_ref, gq_ref, gkv_ref, cs_ref, cqn_ref, ckv_ref, ckvb_ref, kr_ref, krb_ref,
                     *, q_lora, kv_lora, rope):
    p = _dot_nt(h_ref[...], w_ref[...])
    cqn_ref[...] = _rms(p[:, :q_lora], gq_ref[...]).astype(cqn_ref.dtype)
    ckv = _rms(p[:, q_lora:q_lora + kv_lora], gkv_ref[...])
    ckv_ref[...] = ckv
    ckvb_ref[...] = ckv.astype(ckvb_ref.dtype)
    t = p[:, q_lora + kv_lora:] * cs_ref[...]
    kr = (t + pltpu.roll(t, rope, axis=1))[:, :rope]
    kr_ref[...] = kr
    krb_ref[...] = kr.astype(krb_ref.dtype)


def _lat_proj(h, w_lat_t, g_q, g_kv, cs, *, q_lora, kv_lora, rope, name):
    m, d = h.shape
    w = w_lat_t.shape[0]
    tm = _tile(m, 512)
    row = lambda i: (i, 0)
    fix = lambda i: (0, 0)
    return pl.pallas_call(
        functools.partial(_lat_proj_kernel, q_lora=q_lora, kv_lora=kv_lora, rope=rope),
        out_shape=[jax.ShapeDtypeStruct((m, q_lora), BF16),
                   jax.ShapeDtypeStruct((m, kv_lora), F32),
                   jax.ShapeDtypeStruct((m, kv_lora), BF16),
                   jax.ShapeDtypeStruct((m, rope), F32),
                   jax.ShapeDtypeStruct((m, rope), BF16)],
        grid=(m // tm,),
        in_specs=[pl.BlockSpec((tm, d), row), pl.BlockSpec((w, d), fix),
                  pl.BlockSpec((1, q_lora), fix),
                  pl.BlockSpec((1, kv_lora), fix), pl.BlockSpec((tm, 2 * rope), row)],
        out_specs=[pl.BlockSpec((tm, q_lora), row), pl.BlockSpec((tm, kv_lora), row),
                   pl.BlockSpec((tm, kv_lora), row), pl.BlockSpec((tm, rope), row),
                   pl.BlockSpec((tm, rope), row)],
        compiler_params=_cparams(("parallel",), 48),
        name=name,
    )(h, w_lat_t, g_q.reshape(1, -1), g_kv.reshape(1, -1), cs)


def _mla_q_kernel(cqn_ref, wq_ref, wuk_ref, cs_ref, qlat_ref, qrope_ref, *, nope, rope, scale):
    qh = _dot(cqn_ref[...], wq_ref[...])
    qn = qh[:, :nope].astype(BF16)
    qlat_ref[0] = (_dot(qn, wuk_ref[...]) * scale).astype(qlat_ref.dtype)
    t = qh[:, nope:] * cs_ref[...]
    qr = (t + pltpu.roll(t, rope, axis=1))[:, :rope]
    qrope_ref[0] = (qr * scale).astype(qrope_ref.dtype)


def _mla_q(cqn, wq_cat, wuk_t, cs, *, heads, nope, rope, kv_lora, scale, name):
    m, q_lora = cqn.shape
    tm = _tile(m, 1024)
    hw = nope + 2 * rope
    return pl.pallas_call(
        functools.partial(_mla_q_kernel, nope=nope, rope=rope, scale=scale),
        out_shape=[jax.ShapeDtypeStruct((heads, m, kv_lora), BF16),
                   jax.ShapeDtypeStruct((heads, m, rope), BF16)],
        grid=(m // tm, heads),
        in_specs=[pl.BlockSpec((tm, q_lora), lambda i, h: (i, 0)),
                  pl.BlockSpec((q_lora, hw), lambda i, h: (0, h)),
                  pl.BlockSpec((nope, kv_lora), lambda i, h: (h, 0)),
                  pl.BlockSpec((tm, 2 * rope), lambda i, h: (i, 0))],
        out_specs=[pl.BlockSpec((1, tm, kv_lora), lambda i, h: (h, i, 0)),
                   pl.BlockSpec((1, tm, rope), lambda i, h: (h, i, 0))],
        compiler_params=_cparams(("parallel", "arbitrary"), 32),
        name=name,
    )(cqn, wq_cat, wuk_t, cs)


def _flash_step(qlat_ref, qrope_ref, k, kr, mask, m_ref, l_ref, acc_ref, *, heads, group, tq):
    rows = group * tq
    tk = k.shape[0]

    def scores(g0):
        q = qlat_ref[g0:g0 + group].reshape(rows, qlat_ref.shape[-1])
        qr = qrope_ref[g0:g0 + group].reshape(rows, qrope_ref.shape[-1])
        return _dot_nt(q, k) + _dot_nt(qr, kr)

    s_next = scores(0)
    for g0 in range(0, heads, group):
        s = s_next
        if g0 + group < heads:
            s_next = scores(g0 + group)
        if mask is not None:
            s = jnp.where(mask[None], s.reshape(group, tq, tk), NEG_BIG).reshape(rows, tk)
        sl = slice(g0 * tq, g0 * tq + rows)
        m_prev = m_ref[sl]
        m_new = jnp.maximum(m_prev, jnp.max(s, axis=-1, keepdims=True))
        p = jnp.exp2(s - m_new)
        alpha = jnp.exp2(m_prev - m_new)
        l_ref[sl] = alpha * l_ref[sl] + jnp.sum(p, axis=-1, keepdims=True)
        acc_ref[sl] = alpha * acc_ref[sl] + _dot(p.astype(BF16), k)
        m_ref[sl] = m_new


def _flash_init(m_ref, l_ref, acc_ref):
    m_ref[...] = jnp.full(m_ref.shape, NEG_BIG, F32)
    l_ref[...] = jnp.zeros(l_ref.shape, F32)
    acc_ref[...] = jnp.zeros(acc_ref.shape, F32)


def _flash_finish(wuv_ref, o_ref, l_ref, acc_ref, heads, vh):
    tq = acc_ref.shape[0] // heads
    o = (acc_ref[...] * (1.0 / l_ref[...])).astype(BF16)
    for h in range(heads):
        o_ref[:, h * vh:(h + 1) * vh] = _dot(o[h * tq:(h + 1) * tq], wuv_ref[h]).astype(o_ref.dtype)


def _chunk_mask(q0, k0, tq, tk):
    qc = (q0 + lax.broadcasted_iota(jnp.int32, (tq, tk), 0)) // CHUNK
    kc = (k0 + lax.broadcasted_iota(jnp.int32, (tq, tk), 1)) // CHUNK
    return kc <= qc


def _mla_sample_kernel(qlat_ref, qrope_ref, cckv_ref, ckr_ref, nckv_ref, nkr_ref, wuv_ref, o_ref,
                       m_ref, l_ref, acc_ref, *, heads, group, tq, nkc, past, vh):
    ki = pl.program_id(1)

    @pl.when(ki == 0)
    def _():
        _flash_init(m_ref, l_ref, acc_ref)

    step = functools.partial(_flash_step, qlat_ref, qrope_ref, m_ref=m_ref, l_ref=l_ref,
                             acc_ref=acc_ref, heads=heads, group=group, tq=tq)

    @pl.when(ki < nkc)
    def _():
        step(cckv_ref[0].astype(BF16), ckr_ref[0].astype(BF16), None)

    @pl.when(ki == nkc)
    def _():
        step(nckv_ref[...], nkr_ref[...], _chunk_mask(past, past, tq, tq))
        _flash_finish(wuv_ref, o_ref, l_ref, acc_ref, heads, vh)


def _mla_sample(qlat, qrope, cache_ckv, cache_kr, ckv, krope, wuv, *, name):
    heads, _, c = qlat.shape
    rope = qrope.shape[-1]
    vh = wuv.shape[-1]
    batch, past, _ = cache_ckv.shape
    tq = ckv.shape[0] // batch
    tk = _tile(past, 1024)
    nkc = past // tk
    cmap = lambda b, ki: (b, jnp.minimum(ki, nkc - 1), 0)
    return pl.pallas_call(
        functools.partial(_mla_sample_kernel, heads=heads, group=_MLA_GROUP, tq=tq, nkc=nkc,
                          past=past, vh=vh),
        out_shape=jax.ShapeDtypeStruct((batch * tq, heads * vh), BF16),
        grid=(batch, nkc + 1),
        in_specs=[pl.BlockSpec((heads, tq, c), lambda b, ki: (0, b, 0)),
                  pl.BlockSpec((heads, tq, rope), lambda b, ki: (0, b, 0)),
                  pl.BlockSpec((1, tk, c), cmap),
                  pl.BlockSpec((1, tk, rope), cmap),
                  pl.BlockSpec((tq, c), lambda b, ki: (b, 0)),
                  pl.BlockSpec((tq, rope), lambda b, ki: (b, 0)),
                  pl.BlockSpec((heads, c, vh), lambda b, ki: (0, 0, 0))],
        out_specs=pl.BlockSpec((tq, heads * vh), lambda b, ki: (b, 0)),
        scratch_shapes=[pltpu.VMEM((heads * tq, 1), F32), pltpu.VMEM((heads * tq, 1), F32),
                        pltpu.VMEM((heads * tq, c), F32)],
        compiler_params=_cparams(("parallel", "arbitrary"), 48),
        name=name,
    )(qlat, qrope, cache_ckv, cache_kr, ckv, krope, wuv)


def _mla_kv_up_kernel(ckv_ref, kr_ref, wuk_ref, wuv_ref, kcat_ref, v_ref, *, hpt, nope, vh):
    c = ckv_ref[...]
    k = _dot(c, wuk_ref[...])
    v = _dot(c, wuv_ref[...])
    kr = kr_ref[...]
    for hh in range(hpt):
        kcat_ref[hh, :, :nope] = k[:, hh * nope:(hh + 1) * nope].astype(kcat_ref.dtype)
        kcat_ref[hh, :, nope:] = kr
        v_ref[hh] = v[:, hh * vh:(hh + 1) * vh].astype(v_ref.dtype)


def _mla_kv_up(ckv_b, krope_b, wuk_flat, wuv_flat, *, heads, name):
    m, c = ckv_b.shape
    rope = krope_b.shape[1]
    nope, vh = wuk_flat.shape[1] // heads, wuv_flat.shape[1] // heads
    hpt = _tile(heads, 4)
    tm = _tile(m, 1024)
    return pl.pallas_call(
        functools.partial(_mla_kv_up_kernel, hpt=hpt, nope=nope, vh=vh),
        out_shape=[jax.ShapeDtypeStruct((heads, m, nope + rope), BF16),
                   jax.ShapeDtypeStruct((heads, m, vh), BF16)],
        grid=(m // tm, heads // hpt),
        in_specs=[pl.BlockSpec((tm, c), lambda i, j: (i, 0)),
                  pl.BlockSpec((tm, rope), lambda i, j: (i, 0)),
                  pl.BlockSpec((c, hpt * nope), lambda i, j: (0, j)),
                  pl.BlockSpec((c, hpt * vh), lambda i, j: (0, j))],
        out_specs=[pl.BlockSpec((hpt, tm, nope + rope), lambda i, j: (j, i, 0)),
                   pl.BlockSpec((hpt, tm, vh), lambda i, j: (j, i, 0))],
        compiler_params=_cparams(("parallel", "arbitrary"), 32),
        name=name,
    )(ckv_b, krope_b, wuk_flat, wuv_flat)


def _mla_qcat_kernel(cqn_ref, wq_ref, cs_ref, qcat_ref, *, hpt, nope, rope, scale):
    qh = _dot(cqn_ref[...], wq_ref[...])
    hw = nope + 2 * rope
    cs = cs_ref[...]
    for hh in range(hpt):
        qcat_ref[hh, :, :nope] = (qh[:, hh * hw:hh * hw + nope] * scale).astype(qcat_ref.dtype)
        t = qh[:, hh * hw + nope:(hh + 1) * hw] * cs
        qr = (t + pltpu.roll(t, rope, axis=1))[:, :rope]
        qcat_ref[hh, :, nope:] = (qr * scale).astype(qcat_ref.dtype)


def _mla_qcat(cqn, wq_cat, cs, *, heads, nope, rope, scale, name):
    m, q_lora = cqn.shape
    hw = nope + 2 * rope
    hpt = _tile(heads, 4)
    tm = _tile(m, 1024)
    return pl.pallas_call(
        functools.partial(_mla_qcat_kernel, hpt=hpt, nope=nope, rope=rope, scale=scale),
        out_shape=jax.ShapeDtypeStruct((heads, m, nope + rope), BF16),
        grid=(m // tm, heads // hpt),
        in_specs=[pl.BlockSpec((tm, q_lora), lambda i, j: (i, 0)),
                  pl.BlockSpec((q_lora, hpt * hw), lambda i, j: (0, j)),
                  pl.BlockSpec((tm, 2 * rope), lambda i, j: (i, 0))],
        out_specs=pl.BlockSpec((hpt, tm, nope + rope), lambda i, j: (j, i, 0)),
        compiler_params=_cparams(("parallel", "arbitrary"), 32),
        name=name,
    )(cqn, wq_cat, cs)


def _mha_step(q_ref, k_ref, v_ref, mask, m_ref, acc_ref, *, heads, group, tq):
    rows = group * tq
    tk = k_ref.shape[1]
    ones = jnp.ones((tk, v_ref.shape[-1]), BF16)

    def scores(g0):
        return jnp.concatenate([_dot_nt(q_ref[g0 + i], k_ref[g0 + i]) for i in range(group)], axis=0)

    s_next = scores(0)
    for g0 in range(0, heads, group):
        s = s_next
        if g0 + group < heads:
            s_next = scores(g0 + group)
        if mask is not None:
            s = jnp.where(mask[None], s.reshape(group, tq, tk), NEG_BIG).reshape(rows, tk)
        sl = slice(g0 * tq, g0 * tq + rows)
        m_prev = m_ref[sl]
        m_new = jnp.maximum(m_prev, jnp.max(s, axis=-1, keepdims=True))
        p = jnp.exp2(s - m_new)
        alpha = jnp.exp2(m_prev - m_new)
        p = p.astype(BF16)
        pv = jnp.concatenate(
            [_dot(p[i * tq:(i + 1) * tq], jnp.concatenate([v_ref[g0 + i], ones], axis=1))
             for i in range(group)], axis=0)
        acc_ref[sl] = alpha * acc_ref[sl] + pv
        m_ref[sl] = m_new


def _mha_prompt_kernel(q_ref, k_ref, v_ref, o_ref, m_ref, acc_ref, *, heads, group, tq, tk, vh):
    qi, ki = pl.program_id(1), pl.program_id(2)
    k_last = ((qi + 1) * tq - 1) // tk
    partial = (ki + 1) * tk > qi * tq + CHUNK
    step = functools.partial(_mha_step, q_ref, k_ref, v_ref, m_ref=m_ref, acc_ref=acc_ref,
                             heads=heads, group=group, tq=tq)

    @pl.when(ki == 0)
    def _():
        m_ref[...] = jnp.full(m_ref.shape, NEG_BIG, F32)
        acc_ref[...] = jnp.zeros(acc_ref.shape, F32)

    @pl.when((ki <= k_last) & jnp.logical_not(partial))
    def _():
        step(None)

    @pl.when((ki <= k_last) & partial)
    def _():
        step(_chunk_mask(qi * tq, ki * tk, tq, tk))

    @pl.when(ki == k_last)
    def _():
        o = acc_ref[:, :vh] * (1.0 / acc_ref[:, vh:])
        for h in range(heads):
            o_ref[:, h * vh:(h + 1) * vh] = o[h * tq:(h + 1) * tq].astype(o_ref.dtype)


def _mha_prompt(qcat, kcat, v, *, batch, seq, name):
    heads, _, dk = qcat.shape
    vh = v.shape[-1]
    tq = _tile(seq, _MLA_TQ)
    tk = _tile(seq, _MLA_TK)
    nq, nk = seq // tq, seq // tk

    def kmap(b, qi, ki):
        return (0, b * nk + jnp.minimum(ki, ((qi + 1) * tq - 1) // tk), 0)

    return pl.pallas_call(
        functools.partial(_mha_prompt_kernel, heads=heads, group=_MHA_GROUP, tq=tq, tk=tk, vh=vh),
        out_shape=jax.ShapeDtypeStruct((batch * seq, heads * vh), BF16),
        grid=(batch, nq, nk),
        in_specs=[pl.BlockSpec((heads, tq, dk), lambda b, qi, ki: (0, b * nq + qi, 0)),
                  pl.BlockSpec((heads, tk, dk), kmap),
                  pl.BlockSpec((heads, tk, vh), kmap)],
        out_specs=pl.BlockSpec((tq, heads * vh), lambda b, qi, ki: (b * nq + qi, 0)),
        scratch_shapes=[pltpu.VMEM((heads * tq, 1), F32), pltpu.VMEM((heads * tq, 2 * vh), F32)],
        compiler_params=_cparams(("parallel", "parallel", "arbitrary"), 56),
        name=name,
    )(qcat, kcat, v)


SB_SKIP = -160.0


def _sb_block(q_ref, get_kv, tri, valid, run_ref, acc_ref, *, heads, group, tq):
    rows = group * tq
    for g0 in range(0, heads, group):
        kv = [get_kv(h) for h in range(g0, g0 + group)]
        z = jnp.concatenate([_dot_nt(q_ref[g0 + i], kv[i][0]) for i in range(group)], axis=0)
        tk = z.shape[-1]
        sp = jnp.log2(1.0 + jnp.exp2(-jnp.abs(z)))
        lk = -(jnp.maximum(z, 0.0) + sp)
        if valid is not None:
            lk = jnp.where(valid[None], lk.reshape(group, tq, tk), 0.0).reshape(rows, tk)
        hi = lk.astype(BF16)
        lo = (lk - hi.astype(F32)).astype(BF16)
        suffix = _dot(hi, tri) + _dot(lo, tri)
        run = run_ref[g0 * tq:g0 * tq + rows]
        w = jnp.exp2((jnp.minimum(z, 0.0) - sp) + suffix + run)
        if valid is not None:
            w = jnp.where(valid[None], w.reshape(group, tq, tk), 0.0).reshape(rows, tk)
        w = w.astype(BF16)
        run_ref[g0 * tq:g0 * tq + rows] = run + suffix[:, :1] + lk[:, :1]
        for i in range(group):
            r0 = (g0 + i) * tq
            acc_ref[r0:r0 + tq] = acc_ref[r0:r0 + tq] + _dot(w[i * tq:(i + 1) * tq], kv[i][1])


def _strict_lower(n):
    row = lax.broadcasted_iota(jnp.int32, (n, n), 0)
    col = lax.broadcasted_iota(jnp.int32, (n, n), 1)
    return col < row


def _all_below(run_ref, bound):
    m = jnp.max(run_ref[...], axis=0, keepdims=True)
    return m[0, 0] <= bound


def _sb_write(o_ref, acc_ref, heads, tq, hd):
    for h in range(heads):
        o_ref[:, h * hd:(h + 1) * hd] = acc_ref[h * tq:(h + 1) * tq].astype(o_ref.dtype)


def _sb_prompt_kernel(q_ref, kd_ref, vd_ref, tri_ref, k_hbm, v_hbm, o_ref, kbuf, vbuf, sem,
                      run_ref, acc_ref, *, heads, group, tq, hd, nq):
    b, qi = pl.program_id(0), pl.program_id(1)

    def copies(j, slot):
        row0 = pl.multiple_of((b * nq + j) * tq, tq)
        return (pltpu.make_async_copy(k_hbm.at[:, pl.ds(row0, tq), :], kbuf.at[slot], sem.at[0, slot]),
                pltpu.make_async_copy(v_hbm.at[:, pl.ds(row0, tq), :], vbuf.at[slot], sem.at[1, slot]))

    @pl.when(qi > 0)
    def _():
        for cp in copies(qi - 1, 0):
            cp.start()

    run_ref[...] = jnp.zeros(run_ref.shape, F32)
    acc_ref[...] = jnp.zeros(acc_ref.shape, F32)
    tri = tri_ref[...]
    blk = functools.partial(_sb_block, run_ref=run_ref, acc_ref=acc_ref, heads=heads, group=group,
                            tq=tq)
    blk(q_ref, lambda h: (kd_ref[h], vd_ref[h]), tri, _strict_lower(tq))

    def cond(c):
        return (c[0] >= 0) & (c[1] > 0)

    def body(c):
        j = c[0]
        slot = lax.rem(qi - 1 - j, 2)
        for cp in copies(j, slot):
            cp.wait()

        @pl.when(j > 0)
        def _():
            for cp in copies(j - 1, 1 - slot):
                cp.start()

        blk(q_ref, lambda h: (kbuf[slot, h], vbuf[slot, h]), tri, None)
        go = jnp.where(_all_below(run_ref, SB_SKIP), 0, 1).astype(jnp.int32)
        return (j - 1, go)

    j_end, _ = lax.while_loop(cond, body, (qi - 1, jnp.int32(1)))

    @pl.when(j_end >= 0)
    def _():
        for cp in copies(j_end, lax.rem(qi - 1 - j_end, 2)):
            cp.wait()

    _sb_write(o_ref, acc_ref, heads, tq, hd)


def _sb_prompt(q, k, v, tri, *, batch, seq, name):
    heads, _, hd = q.shape
    tq = tri.shape[0]
    nq = seq // tq
    blk = lambda b, qi: (0, b * nq + qi, 0)
    return pl.pallas_call(
        functools.partial(_sb_prompt_kernel, heads=heads, group=_SB_GROUP, tq=tq, hd=hd, nq=nq),
        out_shape=jax.ShapeDtypeStruct((batch * seq, heads * hd), BF16),
        grid=(batch, nq),
        in_specs=[pl.BlockSpec((heads, tq, hd), blk), pl.BlockSpec((heads, tq, hd), blk),
                  pl.BlockSpec((heads, tq, hd), blk),
                  pl.BlockSpec((tq, tq), lambda b, qi: (0, 0)),
                  pl.BlockSpec(memory_space=pl.ANY), pl.BlockSpec(memory_space=pl.ANY)],
        out_specs=pl.BlockSpec((tq, heads * hd), lambda b, qi: (b * nq + qi, 0)),
        scratch_shapes=[pltpu.VMEM((2, heads, tq, hd), BF16), pltpu.VMEM((2, heads, tq, hd), BF16),
                        pltpu.SemaphoreType.DMA((2, 2)),
                        pltpu.VMEM((heads * tq, 1), F32), pltpu.VMEM((heads * tq, hd), F32)],
        compiler_params=_cparams(("arbitrary", "arbitrary"), 40),
        name=name,
    )(q, k, v, tri, k, v)


def _sb_sample_kernel(q_ref, nk_ref, nv_ref, tri_ref, ck_hbm, cv_hbm, o_ref, kbuf, vbuf, sem,
                      run_ref, acc_ref, *, heads, group, tq, tk, hd, nkc):
    b = pl.program_id(0)

    def copies(j, slot):
        p0 = pl.multiple_of(j * tk, tk)
        return (pltpu.make_async_copy(ck_hbm.at[b, pl.ds(p0, tk)], kbuf.at[slot], sem.at[0, slot]),
                pltpu.make_async_copy(cv_hbm.at[b, pl.ds(p0, tk)], vbuf.at[slot], sem.at[1, slot]))

    for cp in copies(nkc - 1, 0):
        cp.start()

    run_ref[...] = jnp.zeros(run_ref.shape, F32)
    acc_ref[...] = jnp.zeros(acc_ref.shape, F32)
    blk = functools.partial(_sb_block, run_ref=run_ref, acc_ref=acc_ref, heads=heads, group=group,
                            tq=tq)
    blk(q_ref, lambda h: (nk_ref[h], nv_ref[h]), tri_ref[:tq, :tq], _strict_lower(tq))
    tri = tri_ref[...]

    def cond(c):
        return (c[0] >= 0) & (c[1] > 0)

    def body(c):
        j = c[0]
        slot = lax.rem(nkc - 1 - j, 2)
        for cp in copies(j, slot):
            cp.wait()

        @pl.when(j > 0)
        def _():
            for cp in copies(j - 1, 1 - slot):
                cp.start()

        blk(q_ref, lambda h: (kbuf[slot, :, h, :].astype(BF16), vbuf[slot, :, h, :].astype(BF16)),
            tri, None)
        go = jnp.where(_all_below(run_ref, SB_SKIP), 0, 1).astype(jnp.int32)
        return (j - 1, go)

    j_end, _ = lax.while_loop(cond, body, (jnp.int32(nkc - 1), jnp.int32(1)))

    @pl.when(j_end >= 0)
    def _():
        for cp in copies(j_end, lax.rem(nkc - 1 - j_end, 2)):
            cp.wait()

    _sb_write(o_ref, acc_ref, heads, tq, hd)


def _sb_sample(q, k_new, v_new, cache_k, cache_v, tri, *, name):
    heads, rows, hd = q.shape
    batch, past = cache_k.shape[:2]
    tq = rows // batch
    tk = tri.shape[0]
    nkc = past // tk
    new = lambda b: (0, b, 0)
    return pl.pallas_call(
        functools.partial(_sb_sample_kernel, heads=heads, group=_SB_GROUP, tq=tq, tk=tk, hd=hd,
                          nkc=nkc),
        out_shape=jax.ShapeDtypeStruct((rows, heads * hd), BF16),
        grid=(batch,),
        in_specs=[pl.BlockSpec((heads, tq, hd), new), pl.BlockSpec((heads, tq, hd), new),
                  pl.BlockSpec((heads, tq, hd), new),
                  pl.BlockSpec((tk, tk), lambda b: (0, 0)),
                  pl.BlockSpec(memory_space=pl.ANY), pl.BlockSpec(memory_space=pl.ANY)],
        out_specs=pl.BlockSpec((tq, heads * hd), lambda b: (b, 0)),
        scratch_shapes=[pltpu.VMEM((2, tk, heads, hd), F32), pltpu.VMEM((2, tk, heads, hd), F32),
                        pltpu.SemaphoreType.DMA((2, 2)),
                        pltpu.VMEM((heads * tq, 1), F32), pltpu.VMEM((heads * tq, hd), F32)],
        compiler_params=_cparams(("arbitrary",), 40),
        name=name,
    )(q, k_new, v_new, tri, cache_k, cache_v)


def _mem_attn_kernel(x_ref, g_ref, wq_ref, mk_ref, mv_ref, wo_ref, gn_ref, xo_ref, hn_ref, o_scr,
                     *, nsub, sub, heads, hd, scale):
    x = x_ref[...]
    mq = (_dot(_rms(x, g_ref[...]).astype(BF16), wq_ref[...]) * scale).astype(BF16)
    for s in range(nsub):
        for h in range(heads):
            q = mq[s * sub:(s + 1) * sub, h * hd:(h + 1) * hd]
            k = mk_ref[s, :, h * hd:(h + 1) * hd]
            v = mv_ref[s, :, h * hd:(h + 1) * hd]
            sc = _dot_nt(q, k)
            p = jnp.exp(sc - jnp.max(sc, axis=-1, keepdims=True))
            p = p * (1.0 / jnp.sum(p, axis=-1, keepdims=True))
            o_scr[s * sub:(s + 1) * sub, h * hd:(h + 1) * hd] = _dot(p.astype(BF16), v).astype(BF16)
    xn = x + _dot(o_scr[...], wo_ref[...])
    xo_ref[...] = xn
    hn_ref[...] = _rms(xn, gn_ref[...]).astype(hn_ref.dtype)


def _mem_attn(x, g, w_mq, mem_k, mem_v, w_mo, g_next, *, sub, heads, name):
    m, d = x.shape
    nb, n_mem, width = mem_k.shape
    hd = width // heads
    tm = _tile(m, 256)
    if sub >= tm:
        nsub, rows = 1, tm
        per = sub // tm
        mmap = lambda i: (i // per, 0, 0)
    else:
        nsub, rows = tm // sub, sub
        mmap = lambda i: (i, 0, 0)
    row = lambda i: (i, 0)
    fix = lambda i: (0, 0)
    return pl.pallas_call(
        functools.partial(_mem_attn_kernel, nsub=nsub, sub=rows, heads=heads, hd=hd,
                          scale=hd ** -0.5),
        out_shape=[jax.ShapeDtypeStruct((m, d), F32), jax.ShapeDtypeStruct((m, d), BF16)],
        grid=(m // tm,),
        in_specs=[pl.BlockSpec((tm, d), row), pl.BlockSpec((1, d), fix),
                  pl.BlockSpec((d, width), fix),
                  pl.BlockSpec((nsub, n_mem, width), mmap),
                  pl.BlockSpec((nsub, n_mem, width), mmap),
                  pl.BlockSpec((width, d), fix), pl.BlockSpec((1, d), fix)],
        out_specs=[pl.BlockSpec((tm, d), row), pl.BlockSpec((tm, d), row)],
        scratch_shapes=[pltpu.VMEM((tm, width), BF16)],
        compiler_params=_cparams(("parallel",), 48),
        name=name,
    )(x, g.reshape(1, d), w_mq, mem_k, mem_v, w_mo, g_next.reshape(1, d))


def _rotate_half_rows(w):
    half = w.shape[0] // 2
    return jnp.concatenate([-w[half:], w[:half]], axis=0)


def _rotate_half_cols(w):
    half = w.shape[-1] // 2
    return jnp.concatenate([-w[..., half:], w[..., :half]], axis=-1)


def _rope_table(pos, half):
    inv = ROPE_THETA ** (-jnp.arange(half, dtype=F32) / half)
    ang = pos.astype(F32)[:, None] * inv[None, :]
    c, s = jnp.cos(ang), jnp.sin(ang)
    return jnp.concatenate([c, c, s, s], axis=-1)


def _prepare_weights(w_in, w_uq, w_uk, w_uv, w_branch_a, w_branch_b, w_out, w_mq, w_mk, w_mv, w_mo,
                     w_gate, w_up, w_down, dims):
    q_lora, kv_lora, rope, heads, nope = dims
    o_kr = q_lora + kv_lora
    o_sb = o_kr + rope
    w_in_t = jnp.swapaxes(w_in, 0, 1)
    w_lat = jnp.concatenate([w_in_t[:o_sb], _rotate_half_rows(w_in_t[o_kr:o_sb])], axis=0).astype(BF16)
    w_rest = w_in_t[o_sb:].astype(BF16)
    wq_cat = jnp.concatenate([w_uq, _rotate_half_cols(w_uq[..., nope:])], axis=-1)
    return dict(
        w_lat=w_lat,
        w_rest=w_rest,
        wq_cat=wq_cat.reshape(q_lora, -1).astype(BF16),
        wuk_t=jnp.transpose(w_uk, (1, 2, 0)).reshape(heads * nope, kv_lora).astype(BF16),
        wuv=jnp.transpose(w_uv, (1, 0, 2)).astype(BF16),
        wuk_flat=w_uk.reshape(kv_lora, -1).astype(BF16),
        wuv_flat=w_uv.reshape(kv_lora, -1).astype(BF16),
        w_ba=w_branch_a, w_bb=w_branch_b, w_out=w_out,
        w_mq=w_mq.astype(BF16), w_mk=w_mk.astype(BF16), w_mv=w_mv.astype(BF16),
        w_mo=w_mo.astype(BF16),
        w_gate=w_gate, w_up=w_up, w_down=w_down.astype(BF16),
    )


def _layer(x, pos, past, mem_k, mem_v, w, gains, b_gate, dims, *, batch, tag):
    g_mix, g_q_lat, g_kv_lat, g_xattn, g_ffn = gains
    q_lora, kv_lora, rope, heads, nope = dims
    m, d = x.shape
    t = m // batch
    sb_width = (w["w_rest"].shape[0] - 2 * d) // 3
    sb_heads = sb_width // LANE
    mla_scale = (nope + rope) ** -0.5 * LOG2E
    sb_scale = LANE ** -0.5 * LOG2E
    tm = _tile(m, 1024)
    tn = 512

    h = _rmsnorm(x, g_mix, BF16, f"{tag}_norm_mix")
    cs = jnp.tile(_rope_table(pos, rope // 2), (batch, 1))
    cqn, ckv, ckv_b, krope, krope_b = _lat_proj(h, w["w_lat"], g_q_lat, g_kv_lat, cs, q_lora=q_lora,
                                                kv_lora=kv_lora, rope=rope, name=f"{tag}_proj_lat")
    tn_p = _tile(sb_width, 1024)
    nsb = sb_width // tn_p
    (sbq,) = _fused_matmul(
        [h], [(0, w["w_rest"], 0)], [], [(BF16, "heads")],
        lambda accs, ex: (accs[0] * sb_scale,), n=sb_width, tm=tm, tn=tn_p, name=f"{tag}_proj_sbq",
        w_rows=True, vmem_mib=56)
    sbk, sbk_b = _fused_matmul(
        [h], [(0, w["w_rest"], nsb)], [], [(F32, "tile"), (BF16, "heads")],
        lambda accs, ex: (accs[0], accs[0]), n=sb_width, tm=tm, tn=tn_p, name=f"{tag}_proj_sbk",
        w_rows=True, vmem_mib=56)
    sbv, sbv_b = _fused_matmul(
        [h], [(0, w["w_rest"], 2 * nsb)], [], [(F32, "tile"), (BF16, "heads")],
        lambda accs, ex: (accs[0], accs[0]), n=sb_width, tm=tm, tn=tn_p, name=f"{tag}_proj_sbv",
        w_rows=True, vmem_mib=56)
    (gates,) = _fused_matmul(
        [h], [(0, w["w_rest"], 3 * nsb)], [(b_gate.reshape(1, -1), "row", 0)], [(BF16, "tile")],
        lambda accs, ex: (_sigmoid(accs[0] + ex[0]),), n=2 * d, tm=tm, tn=tn_p,
        name=f"{tag}_proj_gates", w_rows=True, vmem_mib=56)

    tri_n = 256 if t % 256 == 0 else t
    tri = (jnp.arange(tri_n)[:, None] > jnp.arange(tri_n)[None, :]).astype(BF16)
    if past is None:
        qcat = _mla_qcat(cqn, w["wq_cat"], cs, heads=heads, nope=nope, rope=rope, scale=mla_scale,
                         name=f"{tag}_mla_q")
        kcat, vmla = _mla_kv_up(ckv_b, krope_b, w["wuk_flat"], w["wuv_flat"], heads=heads,
                                name=f"{tag}_mla_kv")
        o_a = _mha_prompt(qcat, kcat, vmla, batch=batch, seq=t, name=f"{tag}_mla_attn")
        o_b = _sb_prompt(sbq, sbk_b, sbv_b, tri, batch=batch, seq=t, name=f"{tag}_sb_attn")
    else:
        c_ckv, c_kr, c_k, c_v = past
        qlat, qrope = _mla_q(cqn, w["wq_cat"], w["wuk_t"], cs, heads=heads, nope=nope, rope=rope,
                             kv_lora=kv_lora, scale=mla_scale, name=f"{tag}_mla_q")
        o_a = _mla_sample(qlat, qrope, c_ckv, c_kr, ckv_b, krope_b, w["wuv"], name=f"{tag}_mla_attn")
        tri = (jnp.arange(256)[:, None] > jnp.arange(256)[None, :]).astype(BF16)
        o_b = _sb_sample(sbq, sbk_b, sbv_b, c_k, c_v, tri, name=f"{tag}_sb_attn")

    ng = d // tn
    (merged,) = _fused_matmul(
        [o_a, o_b], [(0, w["w_ba"], 0), (1, w["w_bb"], 0)],
        [(gates, "tile", 0), (gates, "tile", ng)], [(BF16, "tile")],
        lambda accs, ex: (ex[0].astype(F32) * accs[0] + ex[1].astype(F32) * accs[1],),
        n=d, tm=tm, tn=tn, name=f"{tag}_merge", vmem_mib=56)
    (x,) = _fused_matmul(
        [merged], [(0, w["w_out"], 0)], [(x, "tile", 0)], [(F32, "tile")],
        lambda accs, ex: (ex[0] + accs[0],), n=d, tm=tm, tn=tn, name=f"{tag}_out_proj", vmem_mib=56)

    mem_heads = mem_k.shape[2]
    mk = mem_k.reshape(mem_k.shape[0], mem_k.shape[1], -1).astype(BF16)
    mv = mem_v.reshape(mem_v.shape[0], mem_v.shape[1], -1).astype(BF16)
    x, hf = _mem_attn(x, g_xattn, w["w_mq"], mk, mv, w["w_mo"], g_ffn, sub=t, heads=mem_heads,
                      name=f"{tag}_mem_attn")

    d_ff = w["w_gate"].shape[1]
    tn_ff = _tile(d_ff, 256) if d_ff % 512 else 512
    (act,) = _fused_matmul(
        [hf], [(0, w["w_gate"], 0), (0, w["w_up"], 0)], [], [(BF16, "tile")],
        lambda accs, ex: (accs[0] * _sigmoid(accs[0]) * accs[1],), n=d_ff, tm=tm, tn=tn_ff,
        name=f"{tag}_ffn_up", vmem_mib=56)
    (x,) = _fused_matmul(
        [act], [(0, w["w_down"], 0)], [(x, "tile", 0)], [(F32, "tile")],
        lambda accs, ex: (ex[0] + accs[0],), n=d, tm=_tile(m, 512), tn=512, name=f"{tag}_ffn_down",
        vmem_mib=56)
    return x, (ckv, krope, sbk, sbv)


def kernel(x_prompt, x_sample, cache_mla_ckv, cache_mla_krope, cache_sb_k, cache_sb_v, cache_mem_k, cache_mem_v, mem_prompt, g_mix, w_in, b_gate, g_q_lat, w_uq, g_kv_lat, w_uk, w_uv, w_branch_a, w_branch_b, w_out, g_xattn, g_mem, w_mq, w_mk, w_mv, w_mo, g_ffn, w_gate, w_up, w_down, g_final):
    depth = w_in.shape[0]
    bp, seq, d = x_prompt.shape
    bs, dec, _ = x_sample.shape
    past_len = cache_mla_ckv.shape[2]
    q_lora, heads, qk = w_uq.shape[1:]
    kv_lora, _, nope = w_uk.shape[1:]
    rope = qk - nope
    dims = (q_lora, kv_lora, rope, heads, nope)
    sb_heads, sb_hd = cache_sb_k.shape[3:]
    n_mem, mem_heads, mem_hd = cache_mem_k.shape[2:]
    pos_p = jnp.arange(seq)
    pos_s = past_len + jnp.arange(dec)

    xp = x_prompt.reshape(bp * seq, d)
    xs = x_sample.reshape(bs * dec, d)
    outs = [[] for _ in range(10)]
    for l in range(depth):
        w = _prepare_weights(w_in[l], w_uq[l], w_uk[l], w_uv[l], w_branch_a[l], w_branch_b[l],
                             w_out[l], w_mq[l], w_mk[l], w_mv[l], w_mo[l], w_gate[l], w_up[l],
                             w_down[l], dims)
        gains = (g_mix[l], g_q_lat[l], g_kv_lat[l], g_xattn[l], g_ffn[l])
        mn = _rmsnorm(mem_prompt.reshape(bp * n_mem, d), g_mem[l], BF16, f"l{l}_norm_mem")
        mem_w = w["w_mk"].shape[1]
        mk, mv = _fused_matmul(
            [mn], [(0, w["w_mk"], 0), (0, w["w_mv"], 0)], [], [(F32, "tile"), (F32, "tile")],
            lambda accs, ex: (accs[0], accs[1]), n=mem_w, tm=_tile(bp * n_mem, 512),
            tn=_tile(mem_w, 512), name=f"l{l}_mem_kv")
        mk = mk.reshape(bp, n_mem, mem_heads, mem_hd)
        mv = mv.reshape(bp, n_mem, mem_heads, mem_hd)
        xp, (ckv, kr, k, v) = _layer(xp, pos_p, None, mk, mv, w, gains, b_gate[l], dims,
                                     batch=bp, tag=f"l{l}p")
        for lst, val in zip(outs[:6], (ckv.reshape(bp, seq, -1), kr.reshape(bp, seq, -1),
                                       k.reshape(bp, seq, sb_heads, sb_hd),
                                       v.reshape(bp, seq, sb_heads, sb_hd), mk, mv)):
            lst.append(val)
        past = (cache_mla_ckv[l], cache_mla_krope[l], cache_sb_k[l], cache_sb_v[l])
        xs, (ckv, kr, k, v) = _layer(xs, pos_s, past, cache_mem_k[l], cache_mem_v[l], w, gains,
                                     b_gate[l], dims, batch=bs, tag=f"l{l}s")
        for lst, val in zip(outs[6:], (ckv.reshape(bs, dec, -1), kr.reshape(bs, dec, -1),
                                       k.reshape(bs, dec, sb_heads, sb_hd),
                                       v.reshape(bs, dec, sb_heads, sb_hd))):
            lst.append(val)
    y_prompt = _rmsnorm(xp, g_final, F32, "final_norm_p").reshape(bp, seq, d)
    y_sample = _rmsnorm(xs, g_final, F32, "final_norm_s").reshape(bs, dec, d)
    return (y_prompt, y_sample) + tuple(jnp.stack(o) for o in outs)
```

```python
import functools
import math

import jax
import jax.numpy as jnp
from jax import lax
from jax.experimental import pallas as pl
from jax.experimental.pallas import tpu as pltpu

F32 = jnp.float32
BF16 = jnp.bfloat16

CHUNK = 64
EPS = 1e-6
ROPE_THETA = 10000.0
NEG_BIG = -1e30
LOG2E = math.log2(math.e)
MIB = 1024 * 1024
LANE = 128
_SB_GROUP = 4
_MLA_GROUP = 2
_MHA_GROUP = 2
_MLA_TQ = 256
_MLA_TK = 1024
_SB_TILE = 256
_ROW_TILE = 1024
_COL_TILE = 512
_PROJ_COL_TILE = 1024
_FFN_DOWN_ROW_TILE = 512
_VMEM_BIG = 56


def _cparams(sem, vmem_mib):
    return pltpu.CompilerParams(dimension_semantics=sem, vmem_limit_bytes=vmem_mib * MIB)


def _dot(a, b):
    return jnp.dot(a, b, preferred_element_type=F32)


def _dot_nt(a, b):
    return lax.dot_general(a, b, (((1,), (1,)), ((), ())), preferred_element_type=F32)


def _sigmoid(x):
    return 1.0 / (1.0 + jnp.exp(-x))


def _rms(x, g):
    return x * lax.rsqrt(jnp.mean(x * x, axis=-1, keepdims=True) + EPS) * g


def _tile(n, pref):
    if n <= pref:
        return n
    t = pref
    while n % t:
        t //= 2
    return t


def _norm_kernel(x_ref, g_ref, o_ref):
    o_ref[...] = _rms(x_ref[...], g_ref[...]).astype(o_ref.dtype)


def _rmsnorm(x, g, out_dtype, name):
    m, d = x.shape
    tm = _tile(m, 256)
    return pl.pallas_call(
        _norm_kernel,
        out_shape=jax.ShapeDtypeStruct((m, d), out_dtype),
        grid=(m // tm,),
        in_specs=[pl.BlockSpec((tm, d), lambda i: (i, 0)),
                  pl.BlockSpec((1, d), lambda i: (0, 0))],
        out_specs=pl.BlockSpec((tm, d), lambda i: (i, 0)),
        compiler_params=_cparams(("parallel",), 40),
        name=name,
    )(x, g.reshape(1, d))


def _fused_matmul(lhs, dots, extras, outs, epilogue, *, n, tm, tn, name, vmem_mib=48, w_rows=False):
    m = lhs[0].shape[0]
    na, nd, ne = len(lhs), len(dots), len(extras)
    hpt = tn // LANE

    def kernel(*refs):
        a_refs, w_refs = refs[:na], refs[na:na + nd]
        e_refs = refs[na + nd:na + nd + ne]
        o_refs = refs[na + nd + ne:]
        mm = _dot_nt if w_rows else _dot
        accs = [mm(a_refs[k][...], w[...].astype(BF16)) for (k, _, _), w in zip(dots, w_refs)]
        vals = epilogue(accs, [e[...] for e in e_refs])
        for o_ref, v, (_, kind) in zip(o_refs, vals, outs):
            if kind == "tile":
                o_ref[...] = v.astype(o_ref.dtype)
            else:
                for hh in range(hpt):
                    o_ref[hh] = v[:, hh * LANE:(hh + 1) * LANE].astype(o_ref.dtype)

    in_specs, args = [], []
    for a in lhs:
        in_specs.append(pl.BlockSpec((tm, a.shape[1]), lambda i, j: (i, 0)))
        args.append(a)
    for _, w, off in dots:
        if w_rows:
            in_specs.append(pl.BlockSpec((tn, w.shape[1]), lambda i, j, off=off: (j + off, 0)))
        else:
            in_specs.append(pl.BlockSpec((w.shape[0], tn), lambda i, j, off=off: (0, j + off)))
        args.append(w)
    for e, kind, off in extras:
        if kind == "row":
            in_specs.append(pl.BlockSpec((1, tn), lambda i, j, off=off: (0, j + off)))
        else:
            in_specs.append(pl.BlockSpec((tm, tn), lambda i, j, off=off: (i, j + off)))
        args.append(e)
    out_shape, out_specs = [], []
    for dt, kind in outs:
        if kind == "tile":
            out_shape.append(jax.ShapeDtypeStruct((m, n), dt))
            out_specs.append(pl.BlockSpec((tm, tn), lambda i, j: (i, j)))
        else:
            out_shape.append(jax.ShapeDtypeStruct((n // LANE, m, LANE), dt))
            out_specs.append(pl.BlockSpec((hpt, tm, LANE), lambda i, j: (j, i, 0)))
    return pl.pallas_call(
        kernel,
        out_shape=out_shape,
        grid=(m // tm, n // tn),
        in_specs=in_specs,
        out_specs=out_specs,
        compiler_params=_cparams(("parallel", "arbitrary"), vmem_mib),
        name=name,
    )(*args)


def _lat_proj_kernel(x_ref, gm_ref, w_ref, gq_ref, gkv_ref, cs_ref,
                     h_ref, cqn_ref, ckv_ref, ckvb_ref, kr_ref, krb_ref, *, q_lora, kv_lora, rope):
    h = _rms(x_ref[...], gm_ref[...]).astype(h_ref.dtype)
    h_ref[...] = h
    p = _dot_nt(h, w_ref[...])
    cqn_ref[...] = _rms(p[:, :q_lora], gq_ref[...]).astype(cqn_ref.dtype)
    ckv = _rms(p[:, q_lora:q_lora + kv_lora], gkv_ref[...])
    ckv_ref[...] = ckv
    ckvb_ref[...] = ckv.astype(ckvb_ref.dtype)
    t = p[:, q_lora + kv_lora:] * cs_ref[...]
    kr = (t + pltpu.roll(t, rope, axis=1))[:, :rope]
    kr_ref[...] = kr
    krb_ref[...] = kr.astype(krb_ref.dtype)


def _lat_proj(x, g_mix, w_lat_t, g_q, g_kv, cs, *, q_lora, kv_lora, rope, name):
    m, d = x.shape
    w = w_lat_t.shape[0]
    tm = _tile(m, 256)
    row = lambda i: (i, 0)
    fix = lambda i: (0, 0)
    return pl.pallas_call(
        functools.partial(_lat_proj_kernel, q_lora=q_lora, kv_lora=kv_lora, rope=rope),
        out_shape=[jax.ShapeDtypeStruct((m, d), BF16),
                   jax.ShapeDtypeStruct((m, q_lora), BF16),
                   jax.ShapeDtypeStruct((m, kv_lora), F32),
                   jax.ShapeDtypeStruct((m, kv_lora), BF16),
                   jax.ShapeDtypeStruct((m, rope), F32),
                   jax.ShapeDtypeStruct((m, rope), BF16)],
        grid=(m // tm,),
        in_specs=[pl.BlockSpec((tm, d), row), pl.BlockSpec((1, d), fix), pl.BlockSpec((w, d), fix),
                  pl.BlockSpec((1, q_lora), fix),
                  pl.BlockSpec((1, kv_lora), fix), pl.BlockSpec((tm, 2 * rope), row)],
        out_specs=[pl.BlockSpec((tm, d), row),
                   pl.BlockSpec((tm, q_lora), row), pl.BlockSpec((tm, kv_lora), row),
                   pl.BlockSpec((tm, kv_lora), row), pl.BlockSpec((tm, rope), row),
                   pl.BlockSpec((tm, rope), row)],
        compiler_params=_cparams(("parallel",), 56),
        name=name,
    )(x, g_mix.reshape(1, -1), w_lat_t, g_q.reshape(1, -1), g_kv.reshape(1, -1), cs)


def _mla_q_kernel(cqn_ref, wq_ref, wuk_ref, cs_ref, qlat_ref, qrope_ref, *, nope, rope, scale):
    qh = _dot(cqn_ref[...], wq_ref[...])
    qn = qh[:, :nope].astype(BF16)
    qlat_ref[0] = (_dot(qn, wuk_ref[...]) * scale).astype(qlat_ref.dtype)
    t = qh[:, nope:] * cs_ref[...]
    qr = (t + pltpu.roll(t, rope, axis=1))[:, :rope]
    qrope_ref[0] = (qr * scale).astype(qrope_ref.dtype)


def _mla_q(cqn, wq_cat, wuk_t, cs, *, heads, nope, rope, kv_lora, scale, name):
    m, q_lora = cqn.shape
    tm = _tile(m, 1024)
    hw = nope + 2 * rope
    return pl.pallas_call(
        functools.partial(_mla_q_kernel, nope=nope, rope=rope, scale=scale),
        out_shape=[jax.ShapeDtypeStruct((heads, m, kv_lora), BF16),
                   jax.ShapeDtypeStruct((heads, m, rope), BF16)],
        grid=(m // tm, heads),
        in_specs=[pl.BlockSpec((tm, q_lora), lambda i, h: (i, 0)),
                  pl.BlockSpec((q_lora, hw), lambda i, h: (0, h)),
                  pl.BlockSpec((nope, kv_lora), lambda i, h: (h, 0)),
                  pl.BlockSpec((tm, 2 * rope), lambda i, h: (i, 0))],
        out_specs=[pl.BlockSpec((1, tm, kv_lora), lambda i, h: (h, i, 0)),
                   pl.BlockSpec((1, tm, rope), lambda i, h: (h, i, 0))],
        compiler_params=_cparams(("parallel", "arbitrary"), 32),
        name=name,
    )(cqn, wq_cat, wuk_t, cs)


def _flash_step(qlat_ref, qrope_ref, k, kr, mask, m_ref, l_ref, acc_ref, *, heads, group, tq,
                kr_rows=True):
    rows = group * tq
    tk = k.shape[0]
    rope_dot = _dot_nt if kr_rows else _dot

    def scores(g0):
        q = qlat_ref[g0:g0 + group].reshape(rows, qlat_ref.shape[-1])
        qr = qrope_ref[g0:g0 + group].reshape(rows, qrope_ref.shape[-1])
        return _dot_nt(q, k) + rope_dot(qr, kr)

    s_next = scores(0)
    for g0 in range(0, heads, group):
        s = s_next
        if g0 + group < heads:
            s_next = scores(g0 + group)
        if mask is not None:
            s = jnp.where(mask[None], s.reshape(group, tq, tk), NEG_BIG).reshape(rows, tk)
        sl = slice(g0 * tq, g0 * tq + rows)
        m_prev = m_ref[sl]
        m_new = jnp.maximum(m_prev, jnp.max(s, axis=-1, keepdims=True))
        p = jnp.exp2(s - m_new)
        alpha = jnp.exp2(m_prev - m_new)
        l_ref[sl] = alpha * l_ref[sl] + jnp.sum(p, axis=-1, keepdims=True)
        acc_ref[sl] = alpha * acc_ref[sl] + _dot(p.astype(BF16), k)
        m_ref[sl] = m_new


def _flash_init(m_ref, l_ref, acc_ref):
    m_ref[...] = jnp.full(m_ref.shape, NEG_BIG, F32)
    l_ref[...] = jnp.zeros(l_ref.shape, F32)
    acc_ref[...] = jnp.zeros(acc_ref.shape, F32)


def _flash_finish(wuv_ref, o_ref, l_ref, acc_ref, heads, vh):
    tq = acc_ref.shape[0] // heads
    o = (acc_ref[...] * (1.0 / l_ref[...])).astype(BF16)
    for h in range(heads):
        o_ref[:, h * vh:(h + 1) * vh] = _dot(o[h * tq:(h + 1) * tq], wuv_ref[h]).astype(o_ref.dtype)


def _chunk_mask(q0, k0, tq, tk):
    qc = (q0 + lax.broadcasted_iota(jnp.int32, (tq, tk), 0)) // CHUNK
    kc = (k0 + lax.broadcasted_iota(jnp.int32, (tq, tk), 1)) // CHUNK
    return kc <= qc


def _mla_sample_kernel(qlat_ref, qrope_ref, cckv_ref, ckr_ref, nckv_ref, nkr_ref, wuv_ref, o_ref,
                       m_ref, l_ref, acc_ref, *, heads, group, tq, nkc, past, vh):
    ki = pl.program_id(1)

    @pl.when(ki == 0)
    def _():
        _flash_init(m_ref, l_ref, acc_ref)

    step = functools.partial(_flash_step, qlat_ref, qrope_ref, m_ref=m_ref, l_ref=l_ref,
                             acc_ref=acc_ref, heads=heads, group=group, tq=tq)

    @pl.when(ki < nkc)
    def _():
        step(cckv_ref[0].astype(BF16), ckr_ref[0].astype(BF16), None, kr_rows=False)

    @pl.when(ki == nkc)
    def _():
        step(nckv_ref[...], nkr_ref[...], _chunk_mask(past, past, tq, tq))
        _flash_finish(wuv_ref, o_ref, l_ref, acc_ref, heads, vh)


def _mla_sample(qlat, qrope, cache_ckv, cache_kr, ckv, krope, wuv, *, name):
    heads, _, c = qlat.shape
    rope = qrope.shape[-1]
    vh = wuv.shape[-1]
    batch, past, _ = cache_ckv.shape
    tq = ckv.shape[0] // batch
    tk = _tile(past, 1024)
    nkc = past // tk
    cmap = lambda b, ki: (b, jnp.minimum(ki, nkc - 1), 0)
    return pl.pallas_call(
        functools.partial(_mla_sample_kernel, heads=heads, group=_MLA_GROUP, tq=tq, nkc=nkc,
                          past=past, vh=vh),
        out_shape=jax.ShapeDtypeStruct((batch * tq, heads * vh), BF16),
        grid=(batch, nkc + 1),
        in_specs=[pl.BlockSpec((heads, tq, c), lambda b, ki: (0, b, 0)),
                  pl.BlockSpec((heads, tq, rope), lambda b, ki: (0, b, 0)),
                  pl.BlockSpec((1, tk, c), cmap),
                  pl.BlockSpec((1, rope, tk), lambda b, ki: (b, 0, jnp.minimum(ki, nkc - 1))),
                  pl.BlockSpec((tq, c), lambda b, ki: (b, 0)),
                  pl.BlockSpec((tq, rope), lambda b, ki: (b, 0)),
                  pl.BlockSpec((heads, c, vh), lambda b, ki: (0, 0, 0))],
        out_specs=pl.BlockSpec((tq, heads * vh), lambda b, ki: (b, 0)),
        scratch_shapes=[pltpu.VMEM((heads * tq, 1), F32), pltpu.VMEM((heads * tq, 1), F32),
                        pltpu.VMEM((heads * tq, c), F32)],
        compiler_params=_cparams(("parallel", "arbitrary"), 48),
        name=name,
    )(qlat, qrope, cache_ckv, cache_kr, ckv, krope, wuv)


def _mla_kv_up_kernel(ckv_ref, kr_ref, wuk_ref, wuv_ref, kcat_ref, v_ref, *, hpt, nope, vh):
    c = ckv_ref[...]
    k = _dot(c, wuk_ref[...])
    v = _dot(c, wuv_ref[...])
    kr = kr_ref[...]
    for hh in range(hpt):
        kcat_ref[hh, :, :nope] = k[:, hh * nope:(hh + 1) * nope].astype(kcat_ref.dtype)
        kcat_ref[hh, :, nope:] = kr
        v_ref[hh] = v[:, hh * vh:(hh + 1) * vh].astype(v_ref.dtype)


def _mla_kv_up(ckv_b, krope_b, wuk_flat, wuv_flat, *, heads, name):
    m, c = ckv_b.shape
    rope = krope_b.shape[1]
    nope, vh = wuk_flat.shape[1] // heads, wuv_flat.shape[1] // heads
    hpt = _tile(heads, 4)
    tm = _tile(m, 1024)
    return pl.pallas_call(
        functools.partial(_mla_kv_up_kernel, hpt=hpt, nope=nope, vh=vh),
        out_shape=[jax.ShapeDtypeStruct((heads, m, nope + rope), BF16),
                   jax.ShapeDtypeStruct((heads, m, vh), BF16)],
        grid=(m // tm, heads // hpt),
        in_specs=[pl.BlockSpec((tm, c), lambda i, j: (i, 0)),
                  pl.BlockSpec((tm, rope), lambda i, j: (i, 0)),
                  pl.BlockSpec((c, hpt * nope), lambda i, j: (0, j)),
                  pl.BlockSpec((c, hpt * vh), lambda i, j: (0, j))],
        out_specs=[pl.BlockSpec((hpt, tm, nope + rope), lambda i, j: (j, i, 0)),
                   pl.BlockSpec((hpt, tm, vh), lambda i, j: (j, i, 0))],
        compiler_params=_cparams(("parallel", "arbitrary"), 32),
        name=name,
    )(ckv_b, krope_b, wuk_flat, wuv_flat)


def _mla_qcat_kernel(cqn_ref, wq_ref, cs_ref, qcat_ref, *, hpt, nope, rope, scale):
    qh = _dot(cqn_ref[...], wq_ref[...])
    hw = nope + 2 * rope
    cs = cs_ref[...]
    for hh in range(hpt):
        qcat_ref[hh, :, :nope] = (qh[:, hh * hw:hh * hw + nope] * scale).astype(qcat_ref.dtype)
        t = qh[:, hh * hw + nope:(hh + 1) * hw] * cs
        qr = (t + pltpu.roll(t, rope, axis=1))[:, :rope]
        qcat_ref[hh, :, nope:] = (qr * scale).astype(qcat_ref.dtype)


def _mla_qcat(cqn, wq_cat, cs, *, heads, nope, rope, scale, name):
    m, q_lora = cqn.shape
    hw = nope + 2 * rope
    hpt = _tile(heads, 4)
    tm = _tile(m, 1024)
    return pl.pallas_call(
        functools.partial(_mla_qcat_kernel, hpt=hpt, nope=nope, rope=rope, scale=scale),
        out_shape=jax.ShapeDtypeStruct((heads, m, nope + rope), BF16),
        grid=(m // tm, heads // hpt),
        in_specs=[pl.BlockSpec((tm, q_lora), lambda i, j: (i, 0)),
                  pl.BlockSpec((q_lora, hpt * hw), lambda i, j: (0, j)),
                  pl.BlockSpec((tm, 2 * rope), lambda i, j: (i, 0))],
        out_specs=pl.BlockSpec((hpt, tm, nope + rope), lambda i, j: (j, i, 0)),
        compiler_params=_cparams(("parallel", "arbitrary"), 32),
        name=name,
    )(cqn, wq_cat, cs)


def _mha_step(q_ref, k_ref, v_ref, mask, m_ref, acc_ref, *, heads, group, tq):
    rows = group * tq
    tk = k_ref.shape[1]
    ones = jnp.ones((tk, v_ref.shape[-1]), BF16)

    def scores(g0):
        return jnp.concatenate([_dot_nt(q_ref[g0 + i], k_ref[g0 + i]) for i in range(group)], axis=0)

    s_next = scores(0)
    for g0 in range(0, heads, group):
        s = s_next
        if g0 + group < heads:
            s_next = scores(g0 + group)
        if mask is not None:
            s = jnp.where(mask[None], s.reshape(group, tq, tk), NEG_BIG).reshape(rows, tk)
        sl = slice(g0 * tq, g0 * tq + rows)
        m_prev = m_ref[sl]
        m_new = jnp.maximum(m_prev, jnp.max(s, axis=-1, keepdims=True))
        p = jnp.exp2(s - m_new)
        alpha = jnp.exp2(m_prev - m_new)
        p = p.astype(BF16)
        pv = jnp.concatenate(
            [_dot(p[i * tq:(i + 1) * tq], jnp.concatenate([v_ref[g0 + i], ones], axis=1))
             for i in range(group)], axis=0)
        acc_ref[sl] = alpha * acc_ref[sl] + pv
        m_ref[sl] = m_new


def _mha_prompt_kernel(q_ref, k_ref, v_ref, o_ref, m_ref, acc_ref, *, heads, group, tq, tk, vh):
    qi, ki = pl.program_id(1), pl.program_id(2)
    k_last = ((qi + 1) * tq - 1) // tk
    partial = (ki + 1) * tk > qi * tq + CHUNK
    step = functools.partial(_mha_step, q_ref, k_ref, v_ref, m_ref=m_ref, acc_ref=acc_ref,
                             heads=heads, group=group, tq=tq)

    @pl.when(ki == 0)
    def _():
        m_ref[...] = jnp.full(m_ref.shape, NEG_BIG, F32)
        acc_ref[...] = jnp.zeros(acc_ref.shape, F32)

    @pl.when((ki <= k_last) & jnp.logical_not(partial))
    def _():
        step(None)

    @pl.when((ki <= k_last) & partial)
    def _():
        step(_chunk_mask(qi * tq, ki * tk, tq, tk))

    @pl.when(ki == k_last)
    def _():
        o = acc_ref[:, :vh] * (1.0 / acc_ref[:, vh:])
        for h in range(heads):
            o_ref[:, h * vh:(h + 1) * vh] = o[h * tq:(h + 1) * tq].astype(o_ref.dtype)


def _mha_prompt(qcat, kcat, v, *, batch, seq, name):
    heads, _, dk = qcat.shape
    vh = v.shape[-1]
    tq = _tile(seq, _MLA_TQ)
    tk = _tile(seq, _MLA_TK)
    nq, nk = seq // tq, seq // tk

    def kmap(b, qi, ki):
        return (0, b * nk + jnp.minimum(ki, ((qi + 1) * tq - 1) // tk), 0)

    return pl.pallas_call(
        functools.partial(_mha_prompt_kernel, heads=heads, group=_MHA_GROUP, tq=tq, tk=tk, vh=vh),
        out_shape=jax.ShapeDtypeStruct((batch * seq, heads * vh), BF16),
        grid=(batch, nq, nk),
        in_specs=[pl.BlockSpec((heads, tq, dk), lambda b, qi, ki: (0, b * nq + qi, 0)),
                  pl.BlockSpec((heads, tk, dk), kmap),
                  pl.BlockSpec((heads, tk, vh), kmap)],
        out_specs=pl.BlockSpec((tq, heads * vh), lambda b, qi, ki: (b * nq + qi, 0)),
        scratch_shapes=[pltpu.VMEM((heads * tq, 1), F32), pltpu.VMEM((heads * tq, 2 * vh), F32)],
        compiler_params=_cparams(("parallel", "parallel", "arbitrary"), 56),
        name=name,
    )(qcat, kcat, v)


SB_SKIP = -160.0


def _sb_block(q_ref, get_kv, tri, valid, run_ref, acc_ref, *, heads, group, tq):
    rows = group * tq
    for g0 in range(0, heads, group):
        kv = [get_kv(h) for h in range(g0, g0 + group)]
        z = jnp.concatenate([_dot_nt(q_ref[g0 + i], kv[i][0]) for i in range(group)], axis=0)
        tk = z.shape[-1]
        sp = jnp.log2(1.0 + jnp.exp2(-jnp.abs(z)))
        lk = -(jnp.maximum(z, 0.0) + sp)
        if valid is not None:
            lk = jnp.where(valid[None], lk.reshape(group, tq, tk), 0.0).reshape(rows, tk)
        hi = lk.astype(BF16)
        lo = (lk - hi.astype(F32)).astype(BF16)
        suffix = _dot(hi, tri) + _dot(lo, tri)
        run = run_ref[g0 * tq:g0 * tq + rows]
        w = jnp.exp2((jnp.minimum(z, 0.0) - sp) + suffix + run)
        if valid is not None:
            w = jnp.where(valid[None], w.reshape(group, tq, tk), 0.0).reshape(rows, tk)
        w = w.astype(BF16)
        run_ref[g0 * tq:g0 * tq + rows] = run + suffix[:, :1] + lk[:, :1]
        for i in range(group):
            r0 = (g0 + i) * tq
            acc_ref[r0:r0 + tq] = acc_ref[r0:r0 + tq] + _dot(w[i * tq:(i + 1) * tq], kv[i][1])


def _strict_lower(n):
    row = lax.broadcasted_iota(jnp.int32, (n, n), 0)
    col = lax.broadcasted_iota(jnp.int32, (n, n), 1)
    return col < row


def _all_below(run_ref, bound):
    m = jnp.max(run_ref[...], axis=0, keepdims=True)
    return m[0, 0] <= bound


def _sb_write(o_ref, acc_ref, heads, tq, hd):
    for h in range(heads):
        o_ref[:, h * hd:(h + 1) * hd] = acc_ref[h * tq:(h + 1) * tq].astype(o_ref.dtype)


def _sb_prompt_kernel(q_ref, kd_ref, vd_ref, tri_ref, k_hbm, v_hbm, o_ref, kbuf, vbuf, sem,
                      run_ref, acc_ref, *, heads, group, tq, hd, nq):
    b, qi = pl.program_id(0), pl.program_id(1)

    def copies(j, slot):
        row0 = pl.multiple_of((b * nq + j) * tq, tq)
        return (pltpu.make_async_copy(k_hbm.at[:, pl.ds(row0, tq), :], kbuf.at[slot], sem.at[0, slot]),
                pltpu.make_async_copy(v_hbm.at[:, pl.ds(row0, tq), :], vbuf.at[slot], sem.at[1, slot]))

    @pl.when(qi > 0)
    def _():
        for cp in copies(qi - 1, 0):
            cp.start()

    run_ref[...] = jnp.zeros(run_ref.shape, F32)
    acc_ref[...] = jnp.zeros(acc_ref.shape, F32)
    tri = tri_ref[...]
    blk = functools.partial(_sb_block, run_ref=run_ref, acc_ref=acc_ref, heads=heads, group=group,
                            tq=tq)
    blk(q_ref, lambda h: (kd_ref[h], vd_ref[h]), tri, _strict_lower(tq))

    def cond(c):
        return (c[0] >= 0) & (c[1] > 0)

    def body(c):
        j = c[0]
        slot = lax.rem(qi - 1 - j, 2)
        for cp in copies(j, slot):
            cp.wait()

        @pl.when(j > 0)
        def _():
            for cp in copies(j - 1, 1 - slot):
                cp.start()

        blk(q_ref, lambda h: (kbuf[slot, h], vbuf[slot, h]), tri, None)
        go = jnp.where(_all_below(run_ref, SB_SKIP), 0, 1).astype(jnp.int32)
        return (j - 1, go)

    j_end, _ = lax.while_loop(cond, body, (qi - 1, jnp.int32(1)))

    @pl.when(j_end >= 0)
    def _():
        for cp in copies(j_end, lax.rem(qi - 1 - j_end, 2)):
            cp.wait()

    _sb_write(o_ref, acc_ref, heads, tq, hd)


def _sb_prompt(q, k, v, tri, *, batch, seq, name):
    heads, _, hd = q.shape
    tq = tri.shape[0]
    nq = seq // tq
    blk = lambda b, qi: (0, b * nq + qi, 0)
    return pl.pallas_call(
        functools.partial(_sb_prompt_kernel, heads=heads, group=_SB_GROUP, tq=tq, hd=hd, nq=nq),
        out_shape=jax.ShapeDtypeStruct((batch * seq, heads * hd), BF16),
        grid=(batch, nq),
        in_specs=[pl.BlockSpec((heads, tq, hd), blk), pl.BlockSpec((heads, tq, hd), blk),
                  pl.BlockSpec((heads, tq, hd), blk),
                  pl.BlockSpec((tq, tq), lambda b, qi: (0, 0)),
                  pl.BlockSpec(memory_space=pl.ANY), pl.BlockSpec(memory_space=pl.ANY)],
        out_specs=pl.BlockSpec((tq, heads * hd), lambda b, qi: (b * nq + qi, 0)),
        scratch_shapes=[pltpu.VMEM((2, heads, tq, hd), BF16), pltpu.VMEM((2, heads, tq, hd), BF16),
                        pltpu.SemaphoreType.DMA((2, 2)),
                        pltpu.VMEM((heads * tq, 1), F32), pltpu.VMEM((heads * tq, hd), F32)],
        compiler_params=_cparams(("arbitrary", "arbitrary"), 40),
        name=name,
    )(q, k, v, tri, k, v)


def _sb_sample_kernel(q_ref, nk_ref, nv_ref, tri_ref, ck_hbm, cv_hbm, o_ref, kbuf, vbuf, sem,
                      run_ref, acc_ref, *, heads, group, tq, tk, hd, nkc):
    b = pl.program_id(0)

    def copies(j, slot):
        p0 = pl.multiple_of(j * tk, tk)
        return (pltpu.make_async_copy(ck_hbm.at[b, pl.ds(p0, tk)], kbuf.at[slot], sem.at[0, slot]),
                pltpu.make_async_copy(cv_hbm.at[b, pl.ds(p0, tk)], vbuf.at[slot], sem.at[1, slot]))

    for cp in copies(nkc - 1, 0):
        cp.start()

    run_ref[...] = jnp.zeros(run_ref.shape, F32)
    acc_ref[...] = jnp.zeros(acc_ref.shape, F32)
    blk = functools.partial(_sb_block, run_ref=run_ref, acc_ref=acc_ref, heads=heads, group=group,
                            tq=tq)
    blk(q_ref, lambda h: (nk_ref[h], nv_ref[h]), tri_ref[:tq, :tq], _strict_lower(tq))
    tri = tri_ref[...]

    def cond(c):
        return (c[0] >= 0) & (c[1] > 0)

    def body(c):
        j = c[0]
        slot = lax.rem(nkc - 1 - j, 2)
        for cp in copies(j, slot):
            cp.wait()

        @pl.when(j > 0)
        def _():
            for cp in copies(j - 1, 1 - slot):
                cp.start()

        blk(q_ref, lambda h: (kbuf[slot, :, h, :].astype(BF16), vbuf[slot, :, h, :].astype(BF16)),
            tri, None)
        go = jnp.where(_all_below(run_ref, SB_SKIP), 0, 1).astype(jnp.int32)
        return (j - 1, go)

    j_end, _ = lax.while_loop(cond, body, (jnp.int32(nkc - 1), jnp.int32(1)))

    @pl.when(j_end >= 0)
    def _():
        for cp in copies(j_end, lax.rem(nkc - 1 - j_end, 2)):
            cp.wait()

    _sb_write(o_ref, acc_ref, heads, tq, hd)


def _sb_sample(q, k_new, v_new, cache_k, cache_v, tri, *, name):
    heads, rows, hd = q.shape
    batch, past = cache_k.shape[:2]
    tq = rows // batch
    tk = tri.shape[0]
    nkc = past // tk
    new = lambda b: (0, b, 0)
    return pl.pallas_call(
        functools.partial(_sb_sample_kernel, heads=heads, group=_SB_GROUP, tq=tq, tk=tk, hd=hd,
                          nkc=nkc),
        out_shape=jax.ShapeDtypeStruct((rows, heads * hd), BF16),
        grid=(batch,),
        in_specs=[pl.BlockSpec((heads, tq, hd), new), pl.BlockSpec((heads, tq, hd), new),
                  pl.BlockSpec((heads, tq, hd), new),
                  pl.BlockSpec((tk, tk), lambda b: (0, 0)),
                  pl.BlockSpec(memory_space=pl.ANY), pl.BlockSpec(memory_space=pl.ANY)],
        out_specs=pl.BlockSpec((tq, heads * hd), lambda b: (b, 0)),
        scratch_shapes=[pltpu.VMEM((2, tk, heads, hd), F32), pltpu.VMEM((2, tk, heads, hd), F32),
                        pltpu.SemaphoreType.DMA((2, 2)),
                        pltpu.VMEM((heads * tq, 1), F32), pltpu.VMEM((heads * tq, hd), F32)],
        compiler_params=_cparams(("arbitrary",), 40),
        name=name,
    )(q, k_new, v_new, tri, cache_k, cache_v)


def _mem_attn_kernel(x_ref, g_ref, wq_ref, mk_ref, mv_ref, wo_ref, gn_ref, xo_ref, hn_ref, o_scr,
                     *, nsub, sub, heads, hd, scale):
    x = x_ref[...]
    mq = (_dot(_rms(x, g_ref[...]).astype(BF16), wq_ref[...]) * scale).astype(BF16)
    for s in range(nsub):
        for h in range(heads):
            q = mq[s * sub:(s + 1) * sub, h * hd:(h + 1) * hd]
            k = mk_ref[s, :, h * hd:(h + 1) * hd]
            v = mv_ref[s, :, h * hd:(h + 1) * hd]
            sc = _dot_nt(q, k)
            p = jnp.exp(sc - jnp.max(sc, axis=-1, keepdims=True))
            p = p * (1.0 / jnp.sum(p, axis=-1, keepdims=True))
            o_scr[s * sub:(s + 1) * sub, h * hd:(h + 1) * hd] = _dot(p.astype(BF16), v).astype(BF16)
    xn = x + _dot(o_scr[...], wo_ref[...])
    xo_ref[...] = xn
    hn_ref[...] = _rms(xn, gn_ref[...]).astype(hn_ref.dtype)


def _mem_attn(x, g, w_mq, mem_k, mem_v, w_mo, g_next, *, sub, heads, name):
    m, d = x.shape
    nb, n_mem, width = mem_k.shape
    hd = width // heads
    tm = _tile(m, 256)
    if sub >= tm:
        nsub, rows = 1, tm
        per = sub // tm
        mmap = lambda i: (i // per, 0, 0)
    else:
        nsub, rows = tm // sub, sub
        mmap = lambda i: (i, 0, 0)
    row = lambda i: (i, 0)
    fix = lambda i: (0, 0)
    return pl.pallas_call(
        functools.partial(_mem_attn_kernel, nsub=nsub, sub=rows, heads=heads, hd=hd,
                          scale=hd ** -0.5),
        out_shape=[jax.ShapeDtypeStruct((m, d), F32), jax.ShapeDtypeStruct((m, d), BF16)],
        grid=(m // tm,),
        in_specs=[pl.BlockSpec((tm, d), row), pl.BlockSpec((1, d), fix),
                  pl.BlockSpec((d, width), fix),
                  pl.BlockSpec((nsub, n_mem, width), mmap),
                  pl.BlockSpec((nsub, n_mem, width), mmap),
                  pl.BlockSpec((width, d), fix), pl.BlockSpec((1, d), fix)],
        out_specs=[pl.BlockSpec((tm, d), row), pl.BlockSpec((tm, d), row)],
        scratch_shapes=[pltpu.VMEM((tm, width), BF16)],
        compiler_params=_cparams(("parallel",), 48),
        name=name,
    )(x, g.reshape(1, d), w_mq, mem_k, mem_v, w_mo, g_next.reshape(1, d))


def _rotate_half_rows(w):
    half = w.shape[0] // 2
    return jnp.concatenate([-w[half:], w[:half]], axis=0)


def _rotate_half_cols(w):
    half = w.shape[-1] // 2
    return jnp.concatenate([-w[..., half:], w[..., :half]], axis=-1)


def _rope_table(pos, half):
    inv = ROPE_THETA ** (-jnp.arange(half, dtype=F32) / half)
    ang = pos.astype(F32)[:, None] * inv[None, :]
    c, s = jnp.cos(ang), jnp.sin(ang)
    return jnp.concatenate([c, c, s, s], axis=-1)


def _prepare_weights(w_in, w_uq, w_uk, w_uv, w_branch_a, w_branch_b, w_out, w_mq, w_mk, w_mv, w_mo,
                     w_gate, w_up, w_down, dims):
    q_lora, kv_lora, rope, heads, nope = dims
    o_kr = q_lora + kv_lora
    o_sb = o_kr + rope
    w_in_t = jnp.swapaxes(w_in, 0, 1)
    w_lat = jnp.concatenate([w_in_t[:o_sb], _rotate_half_rows(w_in_t[o_kr:o_sb])], axis=0).astype(BF16)
    w_rest = w_in_t[o_sb:].astype(BF16)
    wq_cat = jnp.concatenate([w_uq, _rotate_half_cols(w_uq[..., nope:])], axis=-1)
    return dict(
        w_lat=w_lat,
        w_rest=w_rest,
        wq_cat=wq_cat.reshape(q_lora, -1).astype(BF16),
        wuk_t=jnp.transpose(w_uk, (1, 2, 0)).reshape(heads * nope, kv_lora).astype(BF16),
        wuv=jnp.transpose(w_uv, (1, 0, 2)).astype(BF16),
        wuk_flat=w_uk.reshape(kv_lora, -1).astype(BF16),
        wuv_flat=w_uv.reshape(kv_lora, -1).astype(BF16),
        w_ba=w_branch_a, w_bb=w_branch_b, w_out=w_out,
        w_mq=w_mq.astype(BF16), w_mk=w_mk.astype(BF16), w_mv=w_mv.astype(BF16),
        w_mo=w_mo.astype(BF16),
        w_gate=w_gate, w_up=w_up, w_down=w_down.astype(BF16),
    )


def _layer(x, pos, past, mem_k, mem_v, w, gains, b_gate, dims, *, batch, tag):
    g_mix, g_q_lat, g_kv_lat, g_xattn, g_ffn = gains
    q_lora, kv_lora, rope, heads, nope = dims
    m, d = x.shape
    t = m // batch
    sb_width = (w["w_rest"].shape[0] - 2 * d) // 3
    mla_scale = (nope + rope) ** -0.5 * LOG2E
    sb_scale = LANE ** -0.5 * LOG2E
    tm = _tile(m, _ROW_TILE)
    tn = _COL_TILE

    cs = jnp.tile(_rope_table(pos, rope // 2), (batch, 1))
    h, cqn, ckv, ckv_b, krope, krope_b = _lat_proj(
        x, g_mix, w["w_lat"], g_q_lat, g_kv_lat, cs, q_lora=q_lora, kv_lora=kv_lora, rope=rope,
        name=f"{tag}_proj_lat")
    tn_p = _tile(sb_width, _PROJ_COL_TILE)
    nsb = sb_width // tn_p
    (sbq,) = _fused_matmul(
        [h], [(0, w["w_rest"], 0)], [], [(BF16, "heads")],
        lambda accs, ex: (accs[0] * sb_scale,), n=sb_width, tm=tm, tn=tn_p, name=f"{tag}_proj_sbq",
        w_rows=True, vmem_mib=_VMEM_BIG)
    sbk, sbk_b = _fused_matmul(
        [h], [(0, w["w_rest"], nsb)], [], [(F32, "tile"), (BF16, "heads")],
        lambda accs, ex: (accs[0], accs[0]), n=sb_width, tm=tm, tn=tn_p, name=f"{tag}_proj_sbk",
        w_rows=True, vmem_mib=_VMEM_BIG)
    sbv, sbv_b = _fused_matmul(
        [h], [(0, w["w_rest"], 2 * nsb)], [], [(F32, "tile"), (BF16, "heads")],
        lambda accs, ex: (accs[0], accs[0]), n=sb_width, tm=tm, tn=tn_p, name=f"{tag}_proj_sbv",
        w_rows=True, vmem_mib=_VMEM_BIG)
    (gates,) = _fused_matmul(
        [h], [(0, w["w_rest"], 3 * nsb)], [(b_gate.reshape(1, -1), "row", 0)], [(BF16, "tile")],
        lambda accs, ex: (_sigmoid(accs[0] + ex[0]),), n=2 * d, tm=tm, tn=tn_p,
        name=f"{tag}_proj_gates", w_rows=True, vmem_mib=_VMEM_BIG)

    tri_n = _SB_TILE if t % _SB_TILE == 0 else t
    tri = (jnp.arange(tri_n)[:, None] > jnp.arange(tri_n)[None, :]).astype(BF16)
    if past is None:
        qcat = _mla_qcat(cqn, w["wq_cat"], cs, heads=heads, nope=nope, rope=rope, scale=mla_scale,
                         name=f"{tag}_mla_q")
        kcat, vmla = _mla_kv_up(ckv_b, krope_b, w["wuk_flat"], w["wuv_flat"], heads=heads,
                                name=f"{tag}_mla_kv")
        o_a = _mha_prompt(qcat, kcat, vmla, batch=batch, seq=t, name=f"{tag}_mla_attn")
        o_b = _sb_prompt(sbq, sbk_b, sbv_b, tri, batch=batch, seq=t, name=f"{tag}_sb_attn")
    else:
        c_ckv, c_kr, c_k, c_v = past
        qlat, qrope = _mla_q(cqn, w["wq_cat"], w["wuk_t"], cs, heads=heads, nope=nope, rope=rope,
                             kv_lora=kv_lora, scale=mla_scale, name=f"{tag}_mla_q")
        o_a = _mla_sample(qlat, qrope, c_ckv, jnp.swapaxes(c_kr, 1, 2), ckv_b, krope_b, w["wuv"],
                          name=f"{tag}_mla_attn")
        tri = (jnp.arange(_SB_TILE)[:, None] > jnp.arange(_SB_TILE)[None, :]).astype(BF16)
        o_b = _sb_sample(sbq, sbk_b, sbv_b, c_k, c_v, tri, name=f"{tag}_sb_attn")

    ng = d // tn
    (merged,) = _fused_matmul(
        [o_a, o_b], [(0, w["w_ba"], 0), (1, w["w_bb"], 0)],
        [(gates, "tile", 0), (gates, "tile", ng)], [(BF16, "tile")],
        lambda accs, ex: (ex[0].astype(F32) * accs[0] + ex[1].astype(F32) * accs[1],),
        n=d, tm=tm, tn=tn, name=f"{tag}_merge", vmem_mib=_VMEM_BIG)
    (x,) = _fused_matmul(
        [merged], [(0, w["w_out"], 0)], [(x, "tile", 0)], [(F32, "tile")],
        lambda accs, ex: (ex[0] + accs[0],), n=d, tm=tm, tn=tn, name=f"{tag}_out_proj", vmem_mib=_VMEM_BIG)

    mem_heads = mem_k.shape[2]
    mk = mem_k.reshape(mem_k.shape[0], mem_k.shape[1], -1).astype(BF16)
    mv = mem_v.reshape(mem_v.shape[0], mem_v.shape[1], -1).astype(BF16)
    x, hf = _mem_attn(x, g_xattn, w["w_mq"], mk, mv, w["w_mo"], g_ffn, sub=t, heads=mem_heads,
                      name=f"{tag}_mem_attn")

    d_ff = w["w_gate"].shape[1]
    tn_ff = _tile(d_ff, 256) if d_ff % 512 else 512
    (act,) = _fused_matmul(
        [hf], [(0, w["w_gate"], 0), (0, w["w_up"], 0)], [], [(BF16, "tile")],
        lambda accs, ex: (accs[0] * _sigmoid(accs[0]) * accs[1],), n=d_ff, tm=tm, tn=tn_ff,
        name=f"{tag}_ffn_up", vmem_mib=_VMEM_BIG)
    (x,) = _fused_matmul(
        [act], [(0, w["w_down"], 0)], [(x, "tile", 0)], [(F32, "tile")],
        lambda accs, ex: (ex[0] + accs[0],), n=d, tm=_tile(m, _FFN_DOWN_ROW_TILE), tn=tn,
        name=f"{tag}_ffn_down", vmem_mib=_VMEM_BIG)
    return x, (ckv, krope, sbk, sbv)


def kernel(x_prompt, x_sample, cache_mla_ckv, cache_mla_krope, cache_sb_k, cache_sb_v, cache_mem_k, cache_mem_v, mem_prompt, g_mix, w_in, b_gate, g_q_lat, w_uq, g_kv_lat, w_uk, w_uv, w_branch_a, w_branch_b, w_out, g_xattn, g_mem, w_mq, w_mk, w_mv, w_mo, g_ffn, w_gate, w_up, w_down, g_final):
    depth = w_in.shape[0]
    bp, seq, d = x_prompt.shape
    bs, dec, _ = x_sample.shape
    past_len = cache_mla_ckv.shape[2]
    q_lora, heads, qk = w_uq.shape[1:]
    kv_lora, _, nope = w_uk.shape[1:]
    rope = qk - nope
    dims = (q_lora, kv_lora, rope, heads, nope)
    sb_heads, sb_hd = cache_sb_k.shape[3:]
    n_mem, mem_heads, mem_hd = cache_mem_k.shape[2:]
    pos_p = jnp.arange(seq)
    pos_s = past_len + jnp.arange(dec)

    xp = x_prompt.reshape(bp * seq, d)
    xs = x_sample.reshape(bs * dec, d)
    outs = [[] for _ in range(10)]
    for l in range(depth):
        w = _prepare_weights(w_in[l], w_uq[l], w_uk[l], w_uv[l], w_branch_a[l], w_branch_b[l],
                             w_out[l], w_mq[l], w_mk[l], w_mv[l], w_mo[l], w_gate[l], w_up[l],
                             w_down[l], dims)
        gains = (g_mix[l], g_q_lat[l], g_kv_lat[l], g_xattn[l], g_ffn[l])
        mn = _rmsnorm(mem_prompt.reshape(bp * n_mem, d), g_mem[l], BF16, f"l{l}_norm_mem")
        mem_w = w["w_mk"].shape[1]
        mk, mv = _fused_matmul(
            [mn], [(0, w["w_mk"], 0), (0, w["w_mv"], 0)], [], [(F32, "tile"), (F32, "tile")],
            lambda accs, ex: (accs[0], accs[1]), n=mem_w, tm=_tile(bp * n_mem, 512),
            tn=_tile(mem_w, 512), name=f"l{l}_mem_kv")
        mk = mk.reshape(bp, n_mem, mem_heads, mem_hd)
        mv = mv.reshape(bp, n_mem, mem_heads, mem_hd)
        xp, (ckv, kr, k, v) = _layer(xp, pos_p, None, mk, mv, w, gains, b_gate[l], dims,
                                     batch=bp, tag=f"l{l}p")
        for lst, val in zip(outs[:6], (ckv.reshape(bp, seq, -1), kr.reshape(bp, seq, -1),
                                       k.reshape(bp, seq, sb_heads, sb_hd),
                                       v.reshape(bp, seq, sb_heads, sb_hd), mk, mv)):
            lst.append(val)
        past = (cache_mla_ckv[l], cache_mla_krope[l], cache_sb_k[l], cache_sb_v[l])
        xs, (ckv, kr, k, v) = _layer(xs, pos_s, past, cache_mem_k[l], cache_mem_v[l], w, gains,
                                     b_gate[l], dims, batch=bs, tag=f"l{l}s")
        for lst, val in zip(outs[6:], (ckv.reshape(bs, dec, -1), kr.reshape(bs, dec, -1),
                                       k.reshape(bs, dec, sb_heads, sb_hd),
                                       v.reshape(bs, dec, sb_heads, sb_hd))):
            lst.append(val)
    y_prompt = _rmsnorm(xp, g_final, F32, "final_norm_p").reshape(bp, seq, d)
    y_sample = _rmsnorm(xs, g_final, F32, "final_norm_s").reshape(bs, dec, d)
    return (y_prompt, y_sample) + tuple(jnp.stack(o) for o in outs)
```

```python
import functools
import math

import jax
import jax.numpy as jnp
from jax import lax
from jax.experimental import pallas as pl
from jax.experimental.pallas import tpu as pltpu

F32 = jnp.float32
BF16 = jnp.bfloat16

CHUNK = 64
EPS = 1e-6
ROPE_THETA = 10000.0
NEG_BIG = -1e30
LOG2E = math.log2(math.e)
MIB = 1024 * 1024
LANE = 128
_SB_GROUP = 8
_MLA_GROUP = 4
_MHA_GROUP = 2
_MLA_TQ = 256
_MLA_TK = 1024
_SB_TILE = 256
_ROW_TILE = 1024
_COL_TILE = 512
_PROJ_COL_TILE = 1024
_FFN_DOWN_ROW_TILE = 512
_VMEM_BIG = 56


def _cparams(sem, vmem_mib):
    return pltpu.CompilerParams(dimension_semantics=sem, vmem_limit_bytes=vmem_mib * MIB)


def _dot(a, b):
    return jnp.dot(a, b, preferred_element_type=F32)


def _dot_nt(a, b):
    return lax.dot_general(a, b, (((1,), (1,)), ((), ())), preferred_element_type=F32)


def _sigmoid(x):
    return 1.0 / (1.0 + jnp.exp(-x))


def _rms(x, g):
    return x * lax.rsqrt(jnp.mean(x * x, axis=-1, keepdims=True) + EPS) * g


def _tile(n, pref):
    if n <= pref:
        return n
    t = pref
    while n % t:
        t //= 2
    return t


def _norm_kernel(x_ref, g_ref, o_ref):
    o_ref[...] = _rms(x_ref[...], g_ref[...]).astype(o_ref.dtype)


def _rmsnorm(x, g, out_dtype, name):
    m, d = x.shape
    tm = _tile(m, 256)
    return pl.pallas_call(
        _norm_kernel,
        out_shape=jax.ShapeDtypeStruct((m, d), out_dtype),
        grid=(m // tm,),
        in_specs=[pl.BlockSpec((tm, d), lambda i: (i, 0)),
                  pl.BlockSpec((1, d), lambda i: (0, 0))],
        out_specs=pl.BlockSpec((tm, d), lambda i: (i, 0)),
        compiler_params=_cparams(("parallel",), 40),
        name=name,
    )(x, g.reshape(1, d))


def _fused_matmul(lhs, dots, extras, outs, epilogue, *, n, tm, tn, name, vmem_mib=48, w_rows=False):
    m = lhs[0].shape[0]
    na, nd, ne = len(lhs), len(dots), len(extras)
    hpt = tn // LANE

    def kernel(*refs):
        a_refs, w_refs = refs[:na], refs[na:na + nd]
        e_refs = refs[na + nd:na + nd + ne]
        o_refs = refs[na + nd + ne:]
        mm = _dot_nt if w_rows else _dot
        accs = [mm(a_refs[k][...], w[...].astype(BF16)) for (k, _, _), w in zip(dots, w_refs)]
        vals = epilogue(accs, [e[...] for e in e_refs])
        for o_ref, v, (_, kind) in zip(o_refs, vals, outs):
            if kind == "tile":
                o_ref[...] = v.astype(o_ref.dtype)
            else:
                for hh in range(hpt):
                    o_ref[hh] = v[:, hh * LANE:(hh + 1) * LANE].astype(o_ref.dtype)

    in_specs, args = [], []
    for a in lhs:
        in_specs.append(pl.BlockSpec((tm, a.shape[1]), lambda i, j: (i, 0)))
        args.append(a)
    for _, w, off in dots:
        if w_rows:
            in_specs.append(pl.BlockSpec((tn, w.shape[1]), lambda i, j, off=off: (j + off, 0)))
        else:
            in_specs.append(pl.BlockSpec((w.shape[0], tn), lambda i, j, off=off: (0, j + off)))
        args.append(w)
    for e, kind, off in extras:
        if kind == "row":
            in_specs.append(pl.BlockSpec((1, tn), lambda i, j, off=off: (0, j + off)))
        else:
            in_specs.append(pl.BlockSpec((tm, tn), lambda i, j, off=off: (i, j + off)))
        args.append(e)
    out_shape, out_specs = [], []
    for dt, kind in outs:
        if kind == "tile":
            out_shape.append(jax.ShapeDtypeStruct((m, n), dt))
            out_specs.append(pl.BlockSpec((tm, tn), lambda i, j: (i, j)))
        else:
            out_shape.append(jax.ShapeDtypeStruct((n // LANE, m, LANE), dt))
            out_specs.append(pl.BlockSpec((hpt, tm, LANE), lambda i, j: (j, i, 0)))
    return pl.pallas_call(
        kernel,
        out_shape=out_shape,
        grid=(m // tm, n // tn),
        in_specs=in_specs,
        out_specs=out_specs,
        compiler_params=_cparams(("parallel", "arbitrary"), vmem_mib),
        name=name,
    )(*args)


def _lat_proj_kernel(x_ref, gm_ref, w_ref, gq_ref, gkv_ref, cs_ref,
                     h_ref, cqn_ref, ckv_ref, ckvb_ref, kr_ref, krb_ref, *, q_lora, kv_lora, rope):
    h = _rms(x_ref[...], gm_ref[...]).astype(h_ref.dtype)
    h_ref[...] = h
    p = _dot_nt(h, w_ref[...])
    cqn_ref[...] = _rms(p[:, :q_lora], gq_ref[...]).astype(cqn_ref.dtype)
    ckv = _rms(p[:, q_lora:q_lora + kv_lora], gkv_ref[...])
    ckv_ref[...] = ckv
    ckvb_ref[...] = ckv.astype(ckvb_ref.dtype)
    t = p[:, q_lora + kv_lora:] * cs_ref[...]
    kr = (t + pltpu.roll(t, rope, axis=1))[:, :rope]
    kr_ref[...] = kr
    krb_ref[...] = kr.astype(krb_ref.dtype)


def _lat_proj(x, g_mix, w_lat_t, g_q, g_kv, cs, *, q_lora, kv_lora, rope, name):
    m, d = x.shape
    w = w_lat_t.shape[0]
    tm = _tile(m, 256)
    row = lambda i: (i, 0)
    fix = lambda i: (0, 0)
    return pl.pallas_call(
        functools.partial(_lat_proj_kernel, q_lora=q_lora, kv_lora=kv_lora, rope=rope),
        out_shape=[jax.ShapeDtypeStruct((m, d), BF16),
                   jax.ShapeDtypeStruct((m, q_lora), BF16),
                   jax.ShapeDtypeStruct((m, kv_lora), F32),
                   jax.ShapeDtypeStruct((m, kv_lora), BF16),
                   jax.ShapeDtypeStruct((m, rope), F32),
                   jax.ShapeDtypeStruct((m, rope), BF16)],
        grid=(m // tm,),
        in_specs=[pl.BlockSpec((tm, d), row), pl.BlockSpec((1, d), fix), pl.BlockSpec((w, d), fix),
                  pl.BlockSpec((1, q_lora), fix),
                  pl.BlockSpec((1, kv_lora), fix), pl.BlockSpec((tm, 2 * rope), row)],
        out_specs=[pl.BlockSpec((tm, d), row),
                   pl.BlockSpec((tm, q_lora), row), pl.BlockSpec((tm, kv_lora), row),
                   pl.BlockSpec((tm, kv_lora), row), pl.BlockSpec((tm, rope), row),
                   pl.BlockSpec((tm, rope), row)],
        compiler_params=_cparams(("parallel",), 56),
        name=name,
    )(x, g_mix.reshape(1, -1), w_lat_t, g_q.reshape(1, -1), g_kv.reshape(1, -1), cs)


def _mla_q_kernel(cqn_ref, wq_ref, wuk_ref, cs_ref, qlat_ref, qrope_ref, *, nope, rope, scale):
    qh = _dot(cqn_ref[...], wq_ref[...])
    qn = qh[:, :nope].astype(BF16)
    qlat_ref[0] = (_dot(qn, wuk_ref[...]) * scale).astype(qlat_ref.dtype)
    t = qh[:, nope:] * cs_ref[...]
    qr = (t + pltpu.roll(t, rope, axis=1))[:, :rope]
    qrope_ref[0] = (qr * scale).astype(qrope_ref.dtype)


def _mla_q(cqn, wq_cat, wuk_t, cs, *, heads, nope, rope, kv_lora, scale, name):
    m, q_lora = cqn.shape
    tm = _tile(m, 1024)
    hw = nope + 2 * rope
    return pl.pallas_call(
        functools.partial(_mla_q_kernel, nope=nope, rope=rope, scale=scale),
        out_shape=[jax.ShapeDtypeStruct((heads, m, kv_lora), BF16),
                   jax.ShapeDtypeStruct((heads, m, rope), BF16)],
        grid=(m // tm, heads),
        in_specs=[pl.BlockSpec((tm, q_lora), lambda i, h: (i, 0)),
                  pl.BlockSpec((q_lora, hw), lambda i, h: (0, h)),
                  pl.BlockSpec((nope, kv_lora), lambda i, h: (h, 0)),
                  pl.BlockSpec((tm, 2 * rope), lambda i, h: (i, 0))],
        out_specs=[pl.BlockSpec((1, tm, kv_lora), lambda i, h: (h, i, 0)),
                   pl.BlockSpec((1, tm, rope), lambda i, h: (h, i, 0))],
        compiler_params=_cparams(("parallel", "arbitrary"), 32),
        name=name,
    )(cqn, wq_cat, wuk_t, cs)


def _flash_step(qlat_ref, qrope_ref, k, kr, mask, m_ref, l_ref, acc_ref, *, heads, group, tq,
                kr_rows=True):
    rows = group * tq
    tk = k.shape[0]
    rope_dot = _dot_nt if kr_rows else _dot

    def scores(g0):
        q = qlat_ref[g0:g0 + group].reshape(rows, qlat_ref.shape[-1])
        qr = qrope_ref[g0:g0 + group].reshape(rows, qrope_ref.shape[-1])
        return _dot_nt(q, k) + rope_dot(qr, kr)

    s_next = scores(0)
    for g0 in range(0, heads, group):
        s = s_next
        if g0 + group < heads:
            s_next = scores(g0 + group)
        if mask is not None:
            s = jnp.where(mask[None], s.reshape(group, tq, tk), NEG_BIG).reshape(rows, tk)
        sl = slice(g0 * tq, g0 * tq + rows)
        m_prev = m_ref[sl]
        m_new = jnp.maximum(m_prev, jnp.max(s, axis=-1, keepdims=True))
        p = jnp.exp2(s - m_new)
        alpha = jnp.exp2(m_prev - m_new)
        l_ref[sl] = alpha * l_ref[sl] + jnp.sum(p, axis=-1, keepdims=True)
        acc_ref[sl] = alpha * acc_ref[sl] + _dot(p.astype(BF16), k)
        m_ref[sl] = m_new


def _flash_init(m_ref, l_ref, acc_ref):
    m_ref[...] = jnp.full(m_ref.shape, NEG_BIG, F32)
    l_ref[...] = jnp.zeros(l_ref.shape, F32)
    acc_ref[...] = jnp.zeros(acc_ref.shape, F32)


def _flash_finish(wuv_ref, o_ref, l_ref, acc_ref, heads, vh):
    tq = acc_ref.shape[0] // heads
    o = (acc_ref[...] * (1.0 / l_ref[...])).astype(BF16)
    for h in range(heads):
        o_ref[:, h * vh:(h + 1) * vh] = _dot(o[h * tq:(h + 1) * tq], wuv_ref[h]).astype(o_ref.dtype)


def _chunk_mask(q0, k0, tq, tk):
    qc = (q0 + lax.broadcasted_iota(jnp.int32, (tq, tk), 0)) // CHUNK
    kc = (k0 + lax.broadcasted_iota(jnp.int32, (tq, tk), 1)) // CHUNK
    return kc <= qc


def _mla_sample_kernel(qlat_ref, qrope_ref, cckv_ref, ckr_ref, nckv_ref, nkr_ref, wuv_ref, o_ref,
                       m_ref, l_ref, acc_ref, *, heads, group, tq, nkc, past, vh):
    ki = pl.program_id(1)

    @pl.when(ki == 0)
    def _():
        _flash_init(m_ref, l_ref, acc_ref)

    step = functools.partial(_flash_step, qlat_ref, qrope_ref, m_ref=m_ref, l_ref=l_ref,
                             acc_ref=acc_ref, heads=heads, group=group, tq=tq)

    @pl.when(ki < nkc)
    def _():
        step(cckv_ref[0].astype(BF16), ckr_ref[0].astype(BF16), None, kr_rows=False)

    @pl.when(ki == nkc)
    def _():
        step(nckv_ref[...], nkr_ref[...], _chunk_mask(past, past, tq, tq))
        _flash_finish(wuv_ref, o_ref, l_ref, acc_ref, heads, vh)


def _mla_sample(qlat, qrope, cache_ckv, cache_kr, ckv, krope, wuv, *, name):
    heads, _, c = qlat.shape
    rope = qrope.shape[-1]
    vh = wuv.shape[-1]
    batch, past, _ = cache_ckv.shape
    tq = ckv.shape[0] // batch
    tk = _tile(past, 1024)
    nkc = past // tk
    cmap = lambda b, ki: (b, jnp.minimum(ki, nkc - 1), 0)
    return pl.pallas_call(
        functools.partial(_mla_sample_kernel, heads=heads, group=_tile(heads, _MLA_GROUP), tq=tq, nkc=nkc,
                          past=past, vh=vh),
        out_shape=jax.ShapeDtypeStruct((batch * tq, heads * vh), BF16),
        grid=(batch, nkc + 1),
        in_specs=[pl.BlockSpec((heads, tq, c), lambda b, ki: (0, b, 0)),
                  pl.BlockSpec((heads, tq, rope), lambda b, ki: (0, b, 0)),
                  pl.BlockSpec((1, tk, c), cmap),
                  pl.BlockSpec((1, rope, tk), lambda b, ki: (b, 0, jnp.minimum(ki, nkc - 1))),
                  pl.BlockSpec((tq, c), lambda b, ki: (b, 0)),
                  pl.BlockSpec((tq, rope), lambda b, ki: (b, 0)),
                  pl.BlockSpec((heads, c, vh), lambda b, ki: (0, 0, 0))],
        out_specs=pl.BlockSpec((tq, heads * vh), lambda b, ki: (b, 0)),
        scratch_shapes=[pltpu.VMEM((heads * tq, 1), F32), pltpu.VMEM((heads * tq, 1), F32),
                        pltpu.VMEM((heads * tq, c), F32)],
        compiler_params=_cparams(("parallel", "arbitrary"), 48),
        name=name,
    )(qlat, qrope, cache_ckv, cache_kr, ckv, krope, wuv)


def _mla_kv_up_kernel(ckv_ref, kr_ref, wuk_ref, wuv_ref, kcat_ref, v_ref, *, hpt, nope, vh):
    c = ckv_ref[...]
    k = _dot(c, wuk_ref[...])
    v = _dot(c, wuv_ref[...])
    kr = kr_ref[...]
    for hh in range(hpt):
        kcat_ref[hh, :, :nope] = k[:, hh * nope:(hh + 1) * nope].astype(kcat_ref.dtype)
        kcat_ref[hh, :, nope:] = kr
        v_ref[hh] = v[:, hh * vh:(hh + 1) * vh].astype(v_ref.dtype)


def _mla_kv_up(ckv_b, krope_b, wuk_flat, wuv_flat, *, heads, name):
    m, c = ckv_b.shape
    rope = krope_b.shape[1]
    nope, vh = wuk_flat.shape[1] // heads, wuv_flat.shape[1] // heads
    hpt = _tile(heads, 4)
    tm = _tile(m, 1024)
    return pl.pallas_call(
        functools.partial(_mla_kv_up_kernel, hpt=hpt, nope=nope, vh=vh),
        out_shape=[jax.ShapeDtypeStruct((heads, m, nope + rope), BF16),
                   jax.ShapeDtypeStruct((heads, m, vh), BF16)],
        grid=(m // tm, heads // hpt),
        in_specs=[pl.BlockSpec((tm, c), lambda i, j: (i, 0)),
                  pl.BlockSpec((tm, rope), lambda i, j: (i, 0)),
                  pl.BlockSpec((c, hpt * nope), lambda i, j: (0, j)),
                  pl.BlockSpec((c, hpt * vh), lambda i, j: (0, j))],
        out_specs=[pl.BlockSpec((hpt, tm, nope + rope), lambda i, j: (j, i, 0)),
                   pl.BlockSpec((hpt, tm, vh), lambda i, j: (j, i, 0))],
        compiler_params=_cparams(("parallel", "arbitrary"), 32),
        name=name,
    )(ckv_b, krope_b, wuk_flat, wuv_flat)


def _mla_qcat_kernel(cqn_ref, wq_ref, cs_ref, qcat_ref, *, hpt, nope, rope, scale):
    qh = _dot(cqn_ref[...], wq_ref[...])
    hw = nope + 2 * rope
    cs = cs_ref[...]
    for hh in range(hpt):
        qcat_ref[hh, :, :nope] = (qh[:, hh * hw:hh * hw + nope] * scale).astype(qcat_ref.dtype)
        t = qh[:, hh * hw + nope:(hh + 1) * hw] * cs
        qr = (t + pltpu.roll(t, rope, axis=1))[:, :rope]
        qcat_ref[hh, :, nope:] = (qr * scale).astype(qcat_ref.dtype)


def _mla_qcat(cqn, wq_cat, cs, *, heads, nope, rope, scale, name):
    m, q_lora = cqn.shape
    hw = nope + 2 * rope
    hpt = _tile(heads, 4)
    tm = _tile(m, 1024)
    return pl.pallas_call(
        functools.partial(_mla_qcat_kernel, hpt=hpt, nope=nope, rope=rope, scale=scale),
        out_shape=jax.ShapeDtypeStruct((heads, m, nope + rope), BF16),
        grid=(m // tm, heads // hpt),
        in_specs=[pl.BlockSpec((tm, q_lora), lambda i, j: (i, 0)),
                  pl.BlockSpec((q_lora, hpt * hw), lambda i, j: (0, j)),
                  pl.BlockSpec((tm, 2 * rope), lambda i, j: (i, 0))],
        out_specs=pl.BlockSpec((hpt, tm, nope + rope), lambda i, j: (j, i, 0)),
        compiler_params=_cparams(("parallel", "arbitrary"), 32),
        name=name,
    )(cqn, wq_cat, cs)


def _mha_step(q_ref, k_ref, v_ref, mask, m_ref, acc_ref, *, heads, group, tq):
    rows = group * tq
    tk = k_ref.shape[1]
    ones = jnp.ones((tk, v_ref.shape[-1]), BF16)

    def scores(g0):
        return jnp.concatenate([_dot_nt(q_ref[g0 + i], k_ref[g0 + i]) for i in range(group)], axis=0)

    s_next = scores(0)
    for g0 in range(0, heads, group):
        s = s_next
        if g0 + group < heads:
            s_next = scores(g0 + group)
        if mask is not None:
            s = jnp.where(mask[None], s.reshape(group, tq, tk), NEG_BIG).reshape(rows, tk)
        sl = slice(g0 * tq, g0 * tq + rows)
        m_prev = m_ref[sl]
        m_new = jnp.maximum(m_prev, jnp.max(s, axis=-1, keepdims=True))
        p = jnp.exp2(s - m_new)
        alpha = jnp.exp2(m_prev - m_new)
        p = p.astype(BF16)
        pv = jnp.concatenate(
            [_dot(p[i * tq:(i + 1) * tq], jnp.concatenate([v_ref[g0 + i], ones], axis=1))
             for i in range(group)], axis=0)
        acc_ref[sl] = alpha * acc_ref[sl] + pv
        m_ref[sl] = m_new


def _mha_prompt_kernel(q_ref, k_ref, v_ref, o_ref, m_ref, acc_ref, *, heads, group, tq, tk, vh):
    qi, ki = pl.program_id(1), pl.program_id(2)
    k_last = ((qi + 1) * tq - 1) // tk
    partial = (ki + 1) * tk > qi * tq + CHUNK
    step = functools.partial(_mha_step, q_ref, k_ref, v_ref, m_ref=m_ref, acc_ref=acc_ref,
                             heads=heads, group=group, tq=tq)

    @pl.when(ki == 0)
    def _():
        m_ref[...] = jnp.full(m_ref.shape, NEG_BIG, F32)
        acc_ref[...] = jnp.zeros(acc_ref.shape, F32)

    @pl.when((ki <= k_last) & jnp.logical_not(partial))
    def _():
        step(None)

    @pl.when((ki <= k_last) & partial)
    def _():
        step(_chunk_mask(qi * tq, ki * tk, tq, tk))

    @pl.when(ki == k_last)
    def _():
        o = acc_ref[:, :vh] * (1.0 / acc_ref[:, vh:])
        for h in range(heads):
            o_ref[:, h * vh:(h + 1) * vh] = o[h * tq:(h + 1) * tq].astype(o_ref.dtype)


def _mha_prompt(qcat, kcat, v, *, batch, seq, name):
    heads, _, dk = qcat.shape
    vh = v.shape[-1]
    tq = _tile(seq, _MLA_TQ)
    tk = _tile(seq, _MLA_TK)
    nq, nk = seq // tq, seq // tk

    def kmap(b, qi, ki):
        return (0, b * nk + jnp.minimum(ki, ((qi + 1) * tq - 1) // tk), 0)

    return pl.pallas_call(
        functools.partial(_mha_prompt_kernel, heads=heads, group=_tile(heads, _MHA_GROUP), tq=tq, tk=tk,
                          vh=vh),
        out_shape=jax.ShapeDtypeStruct((batch * seq, heads * vh), BF16),
        grid=(batch, nq, nk),
        in_specs=[pl.BlockSpec((heads, tq, dk), lambda b, qi, ki: (0, b * nq + qi, 0)),
                  pl.BlockSpec((heads, tk, dk), kmap),
                  pl.BlockSpec((heads, tk, vh), kmap)],
        out_specs=pl.BlockSpec((tq, heads * vh), lambda b, qi, ki: (b * nq + qi, 0)),
        scratch_shapes=[pltpu.VMEM((heads * tq, 1), F32), pltpu.VMEM((heads * tq, 2 * vh), F32)],
        compiler_params=_cparams(("parallel", "parallel", "arbitrary"), 56),
        name=name,
    )(qcat, kcat, v)


SB_SKIP = -160.0


def _sb_block(q_ref, get_kv, tri, valid, run_ref, acc_ref, *, heads, group, tq):
    rows = group * tq
    for g0 in range(0, heads, group):
        kv = [get_kv(h) for h in range(g0, g0 + group)]
        z = jnp.concatenate([_dot_nt(q_ref[g0 + i], kv[i][0]) for i in range(group)], axis=0)
        tk = z.shape[-1]
        sp = jnp.log2(1.0 + jnp.exp2(-jnp.abs(z)))
        lk = -(jnp.maximum(z, 0.0) + sp)
        if valid is not None:
            lk = jnp.where(valid[None], lk.reshape(group, tq, tk), 0.0).reshape(rows, tk)
        hi = lk.astype(BF16)
        lo = (lk - hi.astype(F32)).astype(BF16)
        suffix = _dot(hi, tri) + _dot(lo, tri)
        run = run_ref[g0 * tq:g0 * tq + rows]
        w = jnp.exp2((jnp.minimum(z, 0.0) - sp) + suffix + run)
        if valid is not None:
            w = jnp.where(valid[None], w.reshape(group, tq, tk), 0.0).reshape(rows, tk)
        w = w.astype(BF16)
        run_ref[g0 * tq:g0 * tq + rows] = run + suffix[:, :1] + lk[:, :1]
        for i in range(group):
            r0 = (g0 + i) * tq
            acc_ref[r0:r0 + tq] = acc_ref[r0:r0 + tq] + _dot(w[i * tq:(i + 1) * tq], kv[i][1])


def _strict_lower(n):
    row = lax.broadcasted_iota(jnp.int32, (n, n), 0)
    col = lax.broadcasted_iota(jnp.int32, (n, n), 1)
    return col < row


def _all_below(run_ref, bound):
    m = jnp.max(run_ref[...], axis=0, keepdims=True)
    return m[0, 0] <= bound


def _sb_write(o_ref, acc_ref, heads, tq, hd):
    for h in range(heads):
        o_ref[:, h * hd:(h + 1) * hd] = acc_ref[h * tq:(h + 1) * tq].astype(o_ref.dtype)


def _sb_prompt_kernel(q_ref, kd_ref, vd_ref, tri_ref, k_hbm, v_hbm, o_ref, kbuf, vbuf, sem,
                      run_ref, acc_ref, *, heads, group, tq, hd, nq):
    b, qi = pl.program_id(0), pl.program_id(1)

    def copies(j, slot):
        row0 = pl.multiple_of((b * nq + j) * tq, tq)
        return (pltpu.make_async_copy(k_hbm.at[:, pl.ds(row0, tq), :], kbuf.at[slot], sem.at[0, slot]),
                pltpu.make_async_copy(v_hbm.at[:, pl.ds(row0, tq), :], vbuf.at[slot], sem.at[1, slot]))

    @pl.when(qi > 0)
    def _():
        for cp in copies(qi - 1, 0):
            cp.start()

    run_ref[...] = jnp.zeros(run_ref.shape, F32)
    acc_ref[...] = jnp.zeros(acc_ref.shape, F32)
    tri = tri_ref[...]
    blk = functools.partial(_sb_block, run_ref=run_ref, acc_ref=acc_ref, heads=heads, group=group,
                            tq=tq)
    blk(q_ref, lambda h: (kd_ref[h], vd_ref[h]), tri, _strict_lower(tq))

    def cond(c):
        return (c[0] >= 0) & (c[1] > 0)

    def body(c):
        j = c[0]
        slot = lax.rem(qi - 1 - j, 2)
        for cp in copies(j, slot):
            cp.wait()

        @pl.when(j > 0)
        def _():
            for cp in copies(j - 1, 1 - slot):
                cp.start()

        blk(q_ref, lambda h: (kbuf[slot, h], vbuf[slot, h]), tri, None)
        go = jnp.where(_all_below(run_ref, SB_SKIP), 0, 1).astype(jnp.int32)
        return (j - 1, go)

    j_end, _ = lax.while_loop(cond, body, (qi - 1, jnp.int32(1)))

    @pl.when(j_end >= 0)
    def _():
        for cp in copies(j_end, lax.rem(qi - 1 - j_end, 2)):
            cp.wait()

    _sb_write(o_ref, acc_ref, heads, tq, hd)


def _sb_prompt(q, k, v, tri, *, batch, seq, name):
    heads, _, hd = q.shape
    tq = tri.shape[0]
    nq = seq // tq
    blk = lambda b, qi: (0, b * nq + qi, 0)
    return pl.pallas_call(
        functools.partial(_sb_prompt_kernel, heads=heads, group=_tile(heads, _SB_GROUP), tq=tq, hd=hd,
                          nq=nq),
        out_shape=jax.ShapeDtypeStruct((batch * seq, heads * hd), BF16),
        grid=(batch, nq),
        in_specs=[pl.BlockSpec((heads, tq, hd), blk), pl.BlockSpec((heads, tq, hd), blk),
                  pl.BlockSpec((heads, tq, hd), blk),
                  pl.BlockSpec((tq, tq), lambda b, qi: (0, 0)),
                  pl.BlockSpec(memory_space=pl.ANY), pl.BlockSpec(memory_space=pl.ANY)],
        out_specs=pl.BlockSpec((tq, heads * hd), lambda b, qi: (b * nq + qi, 0)),
        scratch_shapes=[pltpu.VMEM((2, heads, tq, hd), BF16), pltpu.VMEM((2, heads, tq, hd), BF16),
                        pltpu.SemaphoreType.DMA((2, 2)),
                        pltpu.VMEM((heads * tq, 1), F32), pltpu.VMEM((heads * tq, hd), F32)],
        compiler_params=_cparams(("arbitrary", "arbitrary"), 40),
        name=name,
    )(q, k, v, tri, k, v)


def _sb_sample_kernel(q_ref, nk_ref, nv_ref, tri_ref, ck_hbm, cv_hbm, o_ref, kbuf, vbuf, sem,
                      run_ref, acc_ref, *, heads, group, tq, tk, hd, nkc):
    b = pl.program_id(0)

    def copies(j, slot):
        p0 = pl.multiple_of(j * tk, tk)
        return (pltpu.make_async_copy(ck_hbm.at[b, pl.ds(p0, tk)], kbuf.at[slot], sem.at[0, slot]),
                pltpu.make_async_copy(cv_hbm.at[b, pl.ds(p0, tk)], vbuf.at[slot], sem.at[1, slot]))

    for cp in copies(nkc - 1, 0):
        cp.start()

    run_ref[...] = jnp.zeros(run_ref.shape, F32)
    acc_ref[...] = jnp.zeros(acc_ref.shape, F32)
    blk = functools.partial(_sb_block, run_ref=run_ref, acc_ref=acc_ref, heads=heads, group=group,
                            tq=tq)
    blk(q_ref, lambda h: (nk_ref[h], nv_ref[h]), tri_ref[:tq, :tq], _strict_lower(tq))
    tri = tri_ref[...]

    def cond(c):
        return (c[0] >= 0) & (c[1] > 0)

    def body(c):
        j = c[0]
        slot = lax.rem(nkc - 1 - j, 2)
        for cp in copies(j, slot):
            cp.wait()

        @pl.when(j > 0)
        def _():
            for cp in copies(j - 1, 1 - slot):
                cp.start()

        blk(q_ref, lambda h: (kbuf[slot, :, h, :].astype(BF16), vbuf[slot, :, h, :].astype(BF16)),
            tri, None)
        go = jnp.where(_all_below(run_ref, SB_SKIP), 0, 1).astype(jnp.int32)
        return (j - 1, go)

    j_end, _ = lax.while_loop(cond, body, (jnp.int32(nkc - 1), jnp.int32(1)))

    @pl.when(j_end >= 0)
    def _():
        for cp in copies(j_end, lax.rem(nkc - 1 - j_end, 2)):
            cp.wait()

    _sb_write(o_ref, acc_ref, heads, tq, hd)


def _sb_sample(q, k_new, v_new, cache_k, cache_v, tri, *, name):
    heads, rows, hd = q.shape
    batch, past = cache_k.shape[:2]
    tq = rows // batch
    tk = tri.shape[0]
    nkc = past // tk
    new = lambda b: (0, b, 0)
    return pl.pallas_call(
        functools.partial(_sb_sample_kernel, heads=heads, group=_tile(heads, _SB_GROUP), tq=tq, tk=tk, hd=hd,
                          nkc=nkc),
        out_shape=jax.ShapeDtypeStruct((rows, heads * hd), BF16),
        grid=(batch,),
        in_specs=[pl.BlockSpec((heads, tq, hd), new), pl.BlockSpec((heads, tq, hd), new),
                  pl.BlockSpec((heads, tq, hd), new),
                  pl.BlockSpec((tk, tk), lambda b: (0, 0)),
                  pl.BlockSpec(memory_space=pl.ANY), pl.BlockSpec(memory_space=pl.ANY)],
        out_specs=pl.BlockSpec((tq, heads * hd), lambda b: (b, 0)),
        scratch_shapes=[pltpu.VMEM((2, tk, heads, hd), F32), pltpu.VMEM((2, tk, heads, hd), F32),
                        pltpu.SemaphoreType.DMA((2, 2)),
                        pltpu.VMEM((heads * tq, 1), F32), pltpu.VMEM((heads * tq, hd), F32)],
        compiler_params=_cparams(("arbitrary",), 40),
        name=name,
    )(q, k_new, v_new, tri, cache_k, cache_v)


def _mem_attn_kernel(x_ref, g_ref, wq_ref, mk_ref, mv_ref, wo_ref, gn_ref, xo_ref, hn_ref, o_scr,
                     *, nsub, sub, heads, hd, scale):
    x = x_ref[...]
    mq = (_dot(_rms(x, g_ref[...]).astype(BF16), wq_ref[...]) * scale).astype(BF16)
    for s in range(nsub):
        for h in range(heads):
            q = mq[s * sub:(s + 1) * sub, h * hd:(h + 1) * hd]
            k = mk_ref[s, :, h * hd:(h + 1) * hd]
            v = mv_ref[s, :, h * hd:(h + 1) * hd]
            sc = _dot_nt(q, k)
            p = jnp.exp(sc - jnp.max(sc, axis=-1, keepdims=True))
            p = p * (1.0 / jnp.sum(p, axis=-1, keepdims=True))
            o_scr[s * sub:(s + 1) * sub, h * hd:(h + 1) * hd] = _dot(p.astype(BF16), v).astype(BF16)
    xn = x + _dot(o_scr[...], wo_ref[...])
    xo_ref[...] = xn
    hn_ref[...] = _rms(xn, gn_ref[...]).astype(hn_ref.dtype)


def _mem_attn(x, g, w_mq, mem_k, mem_v, w_mo, g_next, *, sub, heads, name):
    m, d = x.shape
    nb, n_mem, width = mem_k.shape
    hd = width // heads
    tm = _tile(m, 256)
    if sub >= tm:
        nsub, rows = 1, tm
        per = sub // tm
        mmap = lambda i: (i // per, 0, 0)
    else:
        nsub, rows = tm // sub, sub
        mmap = lambda i: (i, 0, 0)
    row = lambda i: (i, 0)
    fix = lambda i: (0, 0)
    return pl.pallas_call(
        functools.partial(_mem_attn_kernel, nsub=nsub, sub=rows, heads=heads, hd=hd,
                          scale=hd ** -0.5),
        out_shape=[jax.ShapeDtypeStruct((m, d), F32), jax.ShapeDtypeStruct((m, d), BF16)],
        grid=(m // tm,),
        in_specs=[pl.BlockSpec((tm, d), row), pl.BlockSpec((1, d), fix),
                  pl.BlockSpec((d, width), fix),
                  pl.BlockSpec((nsub, n_mem, width), mmap),
                  pl.BlockSpec((nsub, n_mem, width), mmap),
                  pl.BlockSpec((width, d), fix), pl.BlockSpec((1, d), fix)],
        out_specs=[pl.BlockSpec((tm, d), row), pl.BlockSpec((tm, d), row)],
        scratch_shapes=[pltpu.VMEM((tm, width), BF16)],
        compiler_params=_cparams(("parallel",), 48),
        name=name,
    )(x, g.reshape(1, d), w_mq, mem_k, mem_v, w_mo, g_next.reshape(1, d))


def _rotate_half_rows(w):
    half = w.shape[0] // 2
    return jnp.concatenate([-w[half:], w[:half]], axis=0)


def _rotate_half_cols(w):
    half = w.shape[-1] // 2
    return jnp.concatenate([-w[..., half:], w[..., :half]], axis=-1)


def _rope_table(pos, half):
    inv = ROPE_THETA ** (-jnp.arange(half, dtype=F32) / half)
    ang = pos.astype(F32)[:, None] * inv[None, :]
    c, s = jnp.cos(ang), jnp.sin(ang)
    return jnp.concatenate([c, c, s, s], axis=-1)


def _prepare_weights(w_in, w_uq, w_uk, w_uv, w_branch_a, w_branch_b, w_out, w_mq, w_mk, w_mv, w_mo,
                     w_gate, w_up, w_down, dims):
    q_lora, kv_lora, rope, heads, nope = dims
    o_kr = q_lora + kv_lora
    o_sb = o_kr + rope
    w_in_t = jnp.swapaxes(w_in, 0, 1)
    w_lat = jnp.concatenate([w_in_t[:o_sb], _rotate_half_rows(w_in_t[o_kr:o_sb])], axis=0).astype(BF16)
    w_rest = w_in_t[o_sb:].astype(BF16)
    wq_cat = jnp.concatenate([w_uq, _rotate_half_cols(w_uq[..., nope:])], axis=-1)
    return dict(
        w_lat=w_lat,
        w_rest=w_rest,
        wq_cat=wq_cat.reshape(q_lora, -1).astype(BF16),
        wuk_t=jnp.transpose(w_uk, (1, 2, 0)).reshape(heads * nope, kv_lora).astype(BF16),
        wuv=jnp.transpose(w_uv, (1, 0, 2)).astype(BF16),
        wuk_flat=w_uk.reshape(kv_lora, -1).astype(BF16),
        wuv_flat=w_uv.reshape(kv_lora, -1).astype(BF16),
        w_ba=w_branch_a, w_bb=w_branch_b, w_out=w_out,
        w_mq=w_mq.astype(BF16), w_mk=w_mk.astype(BF16), w_mv=w_mv.astype(BF16),
        w_mo=w_mo.astype(BF16),
        w_gate=w_gate, w_up=w_up, w_down=w_down.astype(BF16),
    )


def _layer(x, pos, past, mem_k, mem_v, w, gains, b_gate, dims, *, batch, tag):
    g_mix, g_q_lat, g_kv_lat, g_xattn, g_ffn = gains
    q_lora, kv_lora, rope, heads, nope = dims
    m, d = x.shape
    t = m // batch
    sb_width = (w["w_rest"].shape[0] - 2 * d) // 3
    mla_scale = (nope + rope) ** -0.5 * LOG2E
    sb_scale = LANE ** -0.5 * LOG2E
    tm = _tile(m, _ROW_TILE)
    tn = _COL_TILE

    cs = jnp.tile(_rope_table(pos, rope // 2), (batch, 1))
    h, cqn, ckv, ckv_b, krope, krope_b = _lat_proj(
        x, g_mix, w["w_lat"], g_q_lat, g_kv_lat, cs, q_lora=q_lora, kv_lora=kv_lora, rope=rope,
        name=f"{tag}_proj_lat")
    tn_p = _tile(sb_width, _PROJ_COL_TILE)
    nsb = sb_width // tn_p
    (sbq,) = _fused_matmul(
        [h], [(0, w["w_rest"], 0)], [], [(BF16, "heads")],
        lambda accs, ex: (accs[0] * sb_scale,), n=sb_width, tm=tm, tn=tn_p, name=f"{tag}_proj_sbq",
        w_rows=True, vmem_mib=_VMEM_BIG)
    sbk, sbk_b = _fused_matmul(
        [h], [(0, w["w_rest"], nsb)], [], [(F32, "tile"), (BF16, "heads")],
        lambda accs, ex: (accs[0], accs[0]), n=sb_width, tm=tm, tn=tn_p, name=f"{tag}_proj_sbk",
        w_rows=True, vmem_mib=_VMEM_BIG)
    sbv, sbv_b = _fused_matmul(
        [h], [(0, w["w_rest"], 2 * nsb)], [], [(F32, "tile"), (BF16, "heads")],
        lambda accs, ex: (accs[0], accs[0]), n=sb_width, tm=tm, tn=tn_p, name=f"{tag}_proj_sbv",
        w_rows=True, vmem_mib=_VMEM_BIG)
    (gates,) = _fused_matmul(
        [h], [(0, w["w_rest"], 3 * nsb)], [(b_gate.reshape(1, -1), "row", 0)], [(BF16, "tile")],
        lambda accs, ex: (_sigmoid(accs[0] + ex[0]),), n=2 * d, tm=tm, tn=tn_p,
        name=f"{tag}_proj_gates", w_rows=True, vmem_mib=_VMEM_BIG)

    tri_n = _SB_TILE if t % _SB_TILE == 0 else t
    tri = (jnp.arange(tri_n)[:, None] > jnp.arange(tri_n)[None, :]).astype(BF16)
    if past is None:
        qcat = _mla_qcat(cqn, w["wq_cat"], cs, heads=heads, nope=nope, rope=rope, scale=mla_scale,
                         name=f"{tag}_mla_q")
        kcat, vmla = _mla_kv_up(ckv_b, krope_b, w["wuk_flat"], w["wuv_flat"], heads=heads,
                                name=f"{tag}_mla_kv")
        o_a = _mha_prompt(qcat, kcat, vmla, batch=batch, seq=t, name=f"{tag}_mla_attn")
        o_b = _sb_prompt(sbq, sbk_b, sbv_b, tri, batch=batch, seq=t, name=f"{tag}_sb_attn")
    else:
        c_ckv, c_kr, c_k, c_v = past
        qlat, qrope = _mla_q(cqn, w["wq_cat"], w["wuk_t"], cs, heads=heads, nope=nope, rope=rope,
                             kv_lora=kv_lora, scale=mla_scale, name=f"{tag}_mla_q")
        o_a = _mla_sample(qlat, qrope, c_ckv, jnp.swapaxes(c_kr, 1, 2), ckv_b, krope_b, w["wuv"],
                          name=f"{tag}_mla_attn")
        tri = (jnp.arange(_SB_TILE)[:, None] > jnp.arange(_SB_TILE)[None, :]).astype(BF16)
        o_b = _sb_sample(sbq, sbk_b, sbv_b, c_k, c_v, tri, name=f"{tag}_sb_attn")

    ng = d // tn
    (merged,) = _fused_matmul(
        [o_a, o_b], [(0, w["w_ba"], 0), (1, w["w_bb"], 0)],
        [(gates, "tile", 0), (gates, "tile", ng)], [(BF16, "tile")],
        lambda accs, ex: (ex[0].astype(F32) * accs[0] + ex[1].astype(F32) * accs[1],),
        n=d, tm=tm, tn=tn, name=f"{tag}_merge", vmem_mib=_VMEM_BIG)
    (x,) = _fused_matmul(
        [merged], [(0, w["w_out"], 0)], [(x, "tile", 0)], [(F32, "tile")],
        lambda accs, ex: (ex[0] + accs[0],), n=d, tm=tm, tn=tn, name=f"{tag}_out_proj", vmem_mib=_VMEM_BIG)

    mem_heads = mem_k.shape[2]
    mk = mem_k.reshape(mem_k.shape[0], mem_k.shape[1], -1).astype(BF16)
    mv = mem_v.reshape(mem_v.shape[0], mem_v.shape[1], -1).astype(BF16)
    x, hf = _mem_attn(x, g_xattn, w["w_mq"], mk, mv, w["w_mo"], g_ffn, sub=t, heads=mem_heads,
                      name=f"{tag}_mem_attn")

    d_ff = w["w_gate"].shape[1]
    tn_ff = _tile(d_ff, 256) if d_ff % 512 else 512
    (act,) = _fused_matmul(
        [hf], [(0, w["w_gate"], 0), (0, w["w_up"], 0)], [], [(BF16, "tile")],
        lambda accs, ex: (accs[0] * _sigmoid(accs[0]) * accs[1],), n=d_ff, tm=tm, tn=tn_ff,
        name=f"{tag}_ffn_up", vmem_mib=_VMEM_BIG)
    (x,) = _fused_matmul(
        [act], [(0, w["w_down"], 0)], [(x, "tile", 0)], [(F32, "tile")],
        lambda accs, ex: (ex[0] + accs[0],), n=d, tm=_tile(m, _FFN_DOWN_ROW_TILE), tn=tn,
        name=f"{tag}_ffn_down", vmem_mib=_VMEM_BIG)
    return x, (ckv, krope, sbk, sbv)


def kernel(x_prompt, x_sample, cache_mla_ckv, cache_mla_krope, cache_sb_k, cache_sb_v, cache_mem_k, cache_mem_v, mem_prompt, g_mix, w_in, b_gate, g_q_lat, w_uq, g_kv_lat, w_uk, w_uv, w_branch_a, w_branch_b, w_out, g_xattn, g_mem, w_mq, w_mk, w_mv, w_mo, g_ffn, w_gate, w_up, w_down, g_final):
    depth = w_in.shape[0]
    bp, seq, d = x_prompt.shape
    bs, dec, _ = x_sample.shape
    past_len = cache_mla_ckv.shape[2]
    q_lora, heads, qk = w_uq.shape[1:]
    kv_lora, _, nope = w_uk.shape[1:]
    rope = qk - nope
    dims = (q_lora, kv_lora, rope, heads, nope)
    sb_heads, sb_hd = cache_sb_k.shape[3:]
    n_mem, mem_heads, mem_hd = cache_mem_k.shape[2:]
    pos_p = jnp.arange(seq)
    pos_s = past_len + jnp.arange(dec)

    xp = x_prompt.reshape(bp * seq, d)
    xs = x_sample.reshape(bs * dec, d)
    outs = [[] for _ in range(10)]
    for l in range(depth):
        w = _prepare_weights(w_in[l], w_uq[l], w_uk[l], w_uv[l], w_branch_a[l], w_branch_b[l],
                             w_out[l], w_mq[l], w_mk[l], w_mv[l], w_mo[l], w_gate[l], w_up[l],
                             w_down[l], dims)
        gains = (g_mix[l], g_q_lat[l], g_kv_lat[l], g_xattn[l], g_ffn[l])
        mn = _rmsnorm(mem_prompt.reshape(bp * n_mem, d), g_mem[l], BF16, f"l{l}_norm_mem")
        mem_w = w["w_mk"].shape[1]
        mk, mv = _fused_matmul(
            [mn], [(0, w["w_mk"], 0), (0, w["w_mv"], 0)], [], [(F32, "tile"), (F32, "tile")],
            lambda accs, ex: (accs[0], accs[1]), n=mem_w, tm=_tile(bp * n_mem, 512),
            tn=_tile(mem_w, 512), name=f"l{l}_mem_kv")
        mk = mk.reshape(bp, n_mem, mem_heads, mem_hd)
        mv = mv.reshape(bp, n_mem, mem_heads, mem_hd)
        xp, (ckv, kr, k, v) = _layer(xp, pos_p, None, mk, mv, w, gains, b_gate[l], dims,
                                     batch=bp, tag=f"l{l}p")
        for lst, val in zip(outs[:6], (ckv.reshape(bp, seq, -1), kr.reshape(bp, seq, -1),
                                       k.reshape(bp, seq, sb_heads, sb_hd),
                                       v.reshape(bp, seq, sb_heads, sb_hd), mk, mv)):
            lst.append(val)
        past = (cache_mla_ckv[l], cache_mla_krope[l], cache_sb_k[l], cache_sb_v[l])
        xs, (ckv, kr, k, v) = _layer(xs, pos_s, past, cache_mem_k[l], cache_mem_v[l], w, gains,
                                     b_gate[l], dims, batch=bs, tag=f"l{l}s")
        for lst, val in zip(outs[6:], (ckv.reshape(bs, dec, -1), kr.reshape(bs, dec, -1),
                                       k.reshape(bs, dec, sb_heads, sb_hd),
                                       v.reshape(bs, dec, sb_heads, sb_hd))):
            lst.append(val)
    y_prompt = _rmsnorm(xp, g_final, F32, "final_norm_p").reshape(bp, seq, d)
    y_sample = _rmsnorm(xs, g_final, F32, "final_norm_s").reshape(bs, dec, d)
    return (y_prompt, y_sample) + tuple(jnp.stack(o) for o in outs)
```

```python
import functools
import math

import jax
import jax.numpy as jnp
from jax import lax
from jax.experimental import pallas as pl
from jax.experimental.pallas import tpu as pltpu

F32 = jnp.float32
BF16 = jnp.bfloat16

CHUNK = 64
EPS = 1e-6
ROPE_THETA = 10000.0
NEG_BIG = -1e30
LOG2E = math.log2(math.e)
MIB = 1024 * 1024
LANE = 128
_SB_GROUP = 8
_MLA_GROUP = 4
_MHA_GROUP = 2
_MLA_TQ = 256
_MLA_TK = 1024
_SB_TILE = 256
_ROW_TILE = 1024
_COL_TILE = 512
_PROJ_COL_TILE = 1024
_FFN_DOWN_ROW_TILE = 512
_VMEM_BIG = 56


def _cparams(sem, vmem_mib):
    return pltpu.CompilerParams(dimension_semantics=sem, vmem_limit_bytes=vmem_mib * MIB)


def _dot(a, b):
    return jnp.dot(a, b, preferred_element_type=F32)


def _dot_nt(a, b):
    return lax.dot_general(a, b, (((1,), (1,)), ((), ())), preferred_element_type=F32)


def _sigmoid(x):
    return 1.0 / (1.0 + jnp.exp(-x))


def _rms(x, g):
    return x * lax.rsqrt(jnp.mean(x * x, axis=-1, keepdims=True) + EPS) * g


def _tile(n, pref):
    if n <= pref:
        return n
    t = pref
    while n % t:
        t //= 2
    return t


def _norm_kernel(x_ref, g_ref, o_ref):
    o_ref[...] = _rms(x_ref[...], g_ref[...]).astype(o_ref.dtype)


def _rmsnorm(x, g, out_dtype, name):
    m, d = x.shape
    tm = _tile(m, 256)
    return pl.pallas_call(
        _norm_kernel,
        out_shape=jax.ShapeDtypeStruct((m, d), out_dtype),
        grid=(m // tm,),
        in_specs=[pl.BlockSpec((tm, d), lambda i: (i, 0)),
                  pl.BlockSpec((1, d), lambda i: (0, 0))],
        out_specs=pl.BlockSpec((tm, d), lambda i: (i, 0)),
        compiler_params=_cparams(("parallel",), 40),
        name=name,
    )(x, g.reshape(1, d))


def _fused_matmul(lhs, dots, extras, outs, epilogue, *, n, tm, tn, name, vmem_mib=48, w_rows=False):
    m = lhs[0].shape[0]
    na, nd, ne = len(lhs), len(dots), len(extras)
    hpt = tn // LANE

    def kernel(*refs):
        a_refs, w_refs = refs[:na], refs[na:na + nd]
        e_refs = refs[na + nd:na + nd + ne]
        o_refs = refs[na + nd + ne:]
        mm = _dot_nt if w_rows else _dot
        accs = [mm(a_refs[k][...], w[...].astype(BF16)) for (k, _, _), w in zip(dots, w_refs)]
        vals = epilogue(accs, [e[...] for e in e_refs])
        for o_ref, v, (_, kind) in zip(o_refs, vals, outs):
            if kind == "tile":
                o_ref[...] = v.astype(o_ref.dtype)
            else:
                for hh in range(hpt):
                    o_ref[hh] = v[:, hh * LANE:(hh + 1) * LANE].astype(o_ref.dtype)

    in_specs, args = [], []
    for a in lhs:
        in_specs.append(pl.BlockSpec((tm, a.shape[1]), lambda i, j: (i, 0)))
        args.append(a)
    for _, w, off in dots:
        if w_rows:
            in_specs.append(pl.BlockSpec((tn, w.shape[1]), lambda i, j, off=off: (j + off, 0)))
        else:
            in_specs.append(pl.BlockSpec((w.shape[0], tn), lambda i, j, off=off: (0, j + off)))
        args.append(w)
    for e, kind, off in extras:
        if kind == "row":
            in_specs.append(pl.BlockSpec((1, tn), lambda i, j, off=off: (0, j + off)))
        else:
            in_specs.append(pl.BlockSpec((tm, tn), lambda i, j, off=off: (i, j + off)))
        args.append(e)
    out_shape, out_specs = [], []
    for dt, kind in outs:
        if kind == "tile":
            out_shape.append(jax.ShapeDtypeStruct((m, n), dt))
            out_specs.append(pl.BlockSpec((tm, tn), lambda i, j: (i, j)))
        else:
            out_shape.append(jax.ShapeDtypeStruct((n // LANE, m, LANE), dt))
            out_specs.append(pl.BlockSpec((hpt, tm, LANE), lambda i, j: (j, i, 0)))
    return pl.pallas_call(
        kernel,
        out_shape=out_shape,
        grid=(m // tm, n // tn),
        in_specs=in_specs,
        out_specs=out_specs,
        compiler_params=_cparams(("parallel", "arbitrary"), vmem_mib),
        name=name,
    )(*args)


def _lat_proj_kernel(x_ref, gm_ref, w_ref, gq_ref, gkv_ref, cs_ref,
                     h_ref, cqn_ref, ckv_ref, ckvb_ref, kr_ref, krb_ref, *, q_lora, kv_lora, rope):
    h = _rms(x_ref[...], gm_ref[...]).astype(h_ref.dtype)
    h_ref[...] = h
    p = _dot_nt(h, w_ref[...])
    cqn_ref[...] = _rms(p[:, :q_lora], gq_ref[...]).astype(cqn_ref.dtype)
    ckv = _rms(p[:, q_lora:q_lora + kv_lora], gkv_ref[...])
    ckv_ref[...] = ckv
    ckvb_ref[...] = ckv.astype(ckvb_ref.dtype)
    t = p[:, q_lora + kv_lora:] * cs_ref[...]
    kr = (t + pltpu.roll(t, rope, axis=1))[:, :rope]
    kr_ref[...] = kr
    krb_ref[...] = kr.astype(krb_ref.dtype)


def _lat_proj(x, g_mix, w_lat_t, g_q, g_kv, cs, *, q_lora, kv_lora, rope, name):
    m, d = x.shape
    w = w_lat_t.shape[0]
    tm = _tile(m, 256)
    row = lambda i: (i, 0)
    fix = lambda i: (0, 0)
    return pl.pallas_call(
        functools.partial(_lat_proj_kernel, q_lora=q_lora, kv_lora=kv_lora, rope=rope),
        out_shape=[jax.ShapeDtypeStruct((m, d), BF16),
                   jax.ShapeDtypeStruct((m, q_lora), BF16),
                   jax.ShapeDtypeStruct((m, kv_lora), F32),
                   jax.ShapeDtypeStruct((m, kv_lora), BF16),
                   jax.ShapeDtypeStruct((m, rope), F32),
                   jax.ShapeDtypeStruct((m, rope), BF16)],
        grid=(m // tm,),
        in_specs=[pl.BlockSpec((tm, d), row), pl.BlockSpec((1, d), fix), pl.BlockSpec((w, d), fix),
                  pl.BlockSpec((1, q_lora), fix),
                  pl.BlockSpec((1, kv_lora), fix), pl.BlockSpec((tm, 2 * rope), row)],
        out_specs=[pl.BlockSpec((tm, d), row),
                   pl.BlockSpec((tm, q_lora), row), pl.BlockSpec((tm, kv_lora), row),
                   pl.BlockSpec((tm, kv_lora), row), pl.BlockSpec((tm, rope), row),
                   pl.BlockSpec((tm, rope), row)],
        compiler_params=_cparams(("parallel",), 56),
        name=name,
    )(x, g_mix.reshape(1, -1), w_lat_t, g_q.reshape(1, -1), g_kv.reshape(1, -1), cs)


def _mla_q_kernel(cqn_ref, wq_ref, wuk_ref, cs_ref, qlat_ref, qrope_ref, *, nope, rope, scale):
    qh = _dot(cqn_ref[...], wq_ref[...])
    qn = qh[:, :nope].astype(BF16)
    qlat_ref[0] = (_dot(qn, wuk_ref[...]) * scale).astype(qlat_ref.dtype)
    t = qh[:, nope:] * cs_ref[...]
    qr = (t + pltpu.roll(t, rope, axis=1))[:, :rope]
    qrope_ref[0] = (qr * scale).astype(qrope_ref.dtype)


def _mla_q(cqn, wq_cat, wuk_t, cs, *, heads, nope, rope, kv_lora, scale, name):
    m, q_lora = cqn.shape
    tm = _tile(m, 1024)
    hw = nope + 2 * rope
    return pl.pallas_call(
        functools.partial(_mla_q_kernel, nope=nope, rope=rope, scale=scale),
        out_shape=[jax.ShapeDtypeStruct((heads, m, kv_lora), BF16),
                   jax.ShapeDtypeStruct((heads, m, rope), BF16)],
        grid=(m // tm, heads),
        in_specs=[pl.BlockSpec((tm, q_lora), lambda i, h: (i, 0)),
                  pl.BlockSpec((q_lora, hw), lambda i, h: (0, h)),
                  pl.BlockSpec((nope, kv_lora), lambda i, h: (h, 0)),
                  pl.BlockSpec((tm, 2 * rope), lambda i, h: (i, 0))],
        out_specs=[pl.BlockSpec((1, tm, kv_lora), lambda i, h: (h, i, 0)),
                   pl.BlockSpec((1, tm, rope), lambda i, h: (h, i, 0))],
        compiler_params=_cparams(("parallel", "arbitrary"), 32),
        name=name,
    )(cqn, wq_cat, wuk_t, cs)


def _flash_step(qlat_ref, qrope_ref, k, kr, mask, m_ref, l_ref, acc_ref, *, heads, group, tq,
                kr_rows=True):
    rows = group * tq
    tk = k.shape[0]
    rope_dot = _dot_nt if kr_rows else _dot

    def scores(g0):
        q = qlat_ref[g0:g0 + group].reshape(rows, qlat_ref.shape[-1])
        qr = qrope_ref[g0:g0 + group].reshape(rows, qrope_ref.shape[-1])
        return _dot_nt(q, k) + rope_dot(qr, kr)

    s_next = scores(0)
    for g0 in range(0, heads, group):
        s = s_next
        if g0 + group < heads:
            s_next = scores(g0 + group)
        if mask is not None:
            s = jnp.where(mask[None], s.reshape(group, tq, tk), NEG_BIG).reshape(rows, tk)
        sl = slice(g0 * tq, g0 * tq + rows)
        m_prev = m_ref[sl]
        m_new = jnp.maximum(m_prev, jnp.max(s, axis=-1, keepdims=True))
        p = jnp.exp2(s - m_new)
        alpha = jnp.exp2(m_prev - m_new)
        l_ref[sl] = alpha * l_ref[sl] + jnp.sum(p, axis=-1, keepdims=True)
        acc_ref[sl] = alpha * acc_ref[sl] + _dot(p.astype(BF16), k)
        m_ref[sl] = m_new


def _flash_init(m_ref, l_ref, acc_ref):
    m_ref[...] = jnp.full(m_ref.shape, NEG_BIG, F32)
    l_ref[...] = jnp.zeros(l_ref.shape, F32)
    acc_ref[...] = jnp.zeros(acc_ref.shape, F32)


def _flash_finish(wuv_ref, o_ref, l_ref, acc_ref, heads, vh):
    tq = acc_ref.shape[0] // heads
    o = (acc_ref[...] * (1.0 / l_ref[...])).astype(BF16)
    for h in range(heads):
        o_ref[:, h * vh:(h + 1) * vh] = _dot(o[h * tq:(h + 1) * tq], wuv_ref[h]).astype(o_ref.dtype)


def _chunk_mask(q0, k0, tq, tk):
    qc = (q0 + lax.broadcasted_iota(jnp.int32, (tq, tk), 0)) // CHUNK
    kc = (k0 + lax.broadcasted_iota(jnp.int32, (tq, tk), 1)) // CHUNK
    return kc <= qc


def _mla_sample_kernel(qlat_ref, qrope_ref, cckv_ref, ckr_ref, nckv_ref, nkr_ref, wuv_ref, o_ref,
                       m_ref, l_ref, acc_ref, *, heads, group, tq, nkc, past, vh):
    ki = pl.program_id(1)

    @pl.when(ki == 0)
    def _():
        _flash_init(m_ref, l_ref, acc_ref)

    step = functools.partial(_flash_step, qlat_ref, qrope_ref, m_ref=m_ref, l_ref=l_ref,
                             acc_ref=acc_ref, heads=heads, group=group, tq=tq)

    @pl.when(ki < nkc)
    def _():
        step(cckv_ref[0].astype(BF16), ckr_ref[0].astype(BF16), None, kr_rows=False)

    @pl.when(ki == nkc)
    def _():
        step(nckv_ref[...], nkr_ref[...], _chunk_mask(past, past, tq, tq))
        _flash_finish(wuv_ref, o_ref, l_ref, acc_ref, heads, vh)


def _mla_sample(qlat, qrope, cache_ckv, cache_kr, ckv, krope, wuv, *, name):
    heads, _, c = qlat.shape
    rope = qrope.shape[-1]
    vh = wuv.shape[-1]
    batch, past, _ = cache_ckv.shape
    tq = ckv.shape[0] // batch
    tk = _tile(past, 1024)
    nkc = past // tk
    cmap = lambda b, ki: (b, jnp.minimum(ki, nkc - 1), 0)
    return pl.pallas_call(
        functools.partial(_mla_sample_kernel, heads=heads, group=_tile(heads, _MLA_GROUP), tq=tq, nkc=nkc,
                          past=past, vh=vh),
        out_shape=jax.ShapeDtypeStruct((batch * tq, heads * vh), BF16),
        grid=(batch, nkc + 1),
        in_specs=[pl.BlockSpec((heads, tq, c), lambda b, ki: (0, b, 0)),
                  pl.BlockSpec((heads, tq, rope), lambda b, ki: (0, b, 0)),
                  pl.BlockSpec((1, tk, c), cmap),
                  pl.BlockSpec((1, rope, tk), lambda b, ki: (b, 0, jnp.minimum(ki, nkc - 1))),
                  pl.BlockSpec((tq, c), lambda b, ki: (b, 0)),
                  pl.BlockSpec((tq, rope), lambda b, ki: (b, 0)),
                  pl.BlockSpec((heads, c, vh), lambda b, ki: (0, 0, 0))],
        out_specs=pl.BlockSpec((tq, heads * vh), lambda b, ki: (b, 0)),
        scratch_shapes=[pltpu.VMEM((heads * tq, 1), F32), pltpu.VMEM((heads * tq, 1), F32),
                        pltpu.VMEM((heads * tq, c), F32)],
        compiler_params=_cparams(("parallel", "arbitrary"), 48),
        name=name,
    )(qlat, qrope, cache_ckv, cache_kr, ckv, krope, wuv)


def _mla_kv_up_kernel(ckv_ref, kr_ref, wuk_ref, wuv_ref, kcat_ref, v_ref, *, hpt, nope, vh):
    c = ckv_ref[...]
    k = _dot(c, wuk_ref[...])
    v = _dot(c, wuv_ref[...])
    kr = kr_ref[...]
    for hh in range(hpt):
        kcat_ref[hh, :, :nope] = k[:, hh * nope:(hh + 1) * nope].astype(kcat_ref.dtype)
        kcat_ref[hh, :, nope:] = kr
        v_ref[hh] = v[:, hh * vh:(hh + 1) * vh].astype(v_ref.dtype)


def _mla_kv_up(ckv_b, krope_b, wuk_flat, wuv_flat, *, heads, name):
    m, c = ckv_b.shape
    rope = krope_b.shape[1]
    nope, vh = wuk_flat.shape[1] // heads, wuv_flat.shape[1] // heads
    hpt = _tile(heads, 4)
    tm = _tile(m, 1024)
    return pl.pallas_call(
        functools.partial(_mla_kv_up_kernel, hpt=hpt, nope=nope, vh=vh),
        out_shape=[jax.ShapeDtypeStruct((heads, m, nope + rope), BF16),
                   jax.ShapeDtypeStruct((heads, m, vh), BF16)],
        grid=(m // tm, heads // hpt),
        in_specs=[pl.BlockSpec((tm, c), lambda i, j: (i, 0)),
                  pl.BlockSpec((tm, rope), lambda i, j: (i, 0)),
                  pl.BlockSpec((c, hpt * nope), lambda i, j: (0, j)),
                  pl.BlockSpec((c, hpt * vh), lambda i, j: (0, j))],
        out_specs=[pl.BlockSpec((hpt, tm, nope + rope), lambda i, j: (j, i, 0)),
                   pl.BlockSpec((hpt, tm, vh), lambda i, j: (j, i, 0))],
        compiler_params=_cparams(("parallel", "arbitrary"), 32),
        name=name,
    )(ckv_b, krope_b, wuk_flat, wuv_flat)


def _mla_qcat_kernel(cqn_ref, wq_ref, cs_ref, qcat_ref, *, hpt, nope, rope, scale):
    qh = _dot(cqn_ref[...], wq_ref[...])
    hw = nope + 2 * rope
    cs = cs_ref[...]
    for hh in range(hpt):
        qcat_ref[hh, :, :nope] = (qh[:, hh * hw:hh * hw + nope] * scale).astype(qcat_ref.dtype)
        t = qh[:, hh * hw + nope:(hh + 1) * hw] * cs
        qr = (t + pltpu.roll(t, rope, axis=1))[:, :rope]
        qcat_ref[hh, :, nope:] = (qr * scale).astype(qcat_ref.dtype)


def _mla_qcat(cqn, wq_cat, cs, *, heads, nope, rope, scale, name):
    m, q_lora = cqn.shape
    hw = nope + 2 * rope
    hpt = _tile(heads, 4)
    tm = _tile(m, 1024)
    return pl.pallas_call(
        functools.partial(_mla_qcat_kernel, hpt=hpt, nope=nope, rope=rope, scale=scale),
        out_shape=jax.ShapeDtypeStruct((heads, m, nope + rope), BF16),
        grid=(m // tm, heads // hpt),
        in_specs=[pl.BlockSpec((tm, q_lora), lambda i, j: (i, 0)),
                  pl.BlockSpec((q_lora, hpt * hw), lambda i, j: (0, j)),
                  pl.BlockSpec((tm, 2 * rope), lambda i, j: (i, 0))],
        out_specs=pl.BlockSpec((hpt, tm, nope + rope), lambda i, j: (j, i, 0)),
        compiler_params=_cparams(("parallel", "arbitrary"), 32),
        name=name,
    )(cqn, wq_cat, cs)


def _mha_step(q_ref, k_ref, v_ref, mask, m_ref, acc_ref, *, heads, group, tq):
    rows = group * tq
    tk = k_ref.shape[1]
    ones = jnp.ones((tk, v_ref.shape[-1]), BF16)

    def scores(g0):
        return jnp.concatenate([_dot_nt(q_ref[g0 + i], k_ref[g0 + i]) for i in range(group)], axis=0)

    s_next = scores(0)
    for g0 in range(0, heads, group):
        s = s_next
        if g0 + group < heads:
            s_next = scores(g0 + group)
        if mask is not None:
            s = jnp.where(mask[None], s.reshape(group, tq, tk), NEG_BIG).reshape(rows, tk)
        sl = slice(g0 * tq, g0 * tq + rows)
        m_prev = m_ref[sl]
        m_new = jnp.maximum(m_prev, jnp.max(s, axis=-1, keepdims=True))
        p = jnp.exp2(s - m_new)
        alpha = jnp.exp2(m_prev - m_new)
        p = p.astype(BF16)
        pv = jnp.concatenate(
            [_dot(p[i * tq:(i + 1) * tq], jnp.concatenate([v_ref[g0 + i], ones], axis=1))
             for i in range(group)], axis=0)
        acc_ref[sl] = alpha * acc_ref[sl] + pv
        m_ref[sl] = m_new


def _mha_prompt_kernel(qi_tab, ki_tab, q_ref, k_ref, v_ref, o_ref, m_ref, acc_ref,
                       *, heads, group, tq, tk, vh):
    step_id = pl.program_id(1)
    qi, ki = qi_tab[step_id], ki_tab[step_id]
    k_last = ((qi + 1) * tq - 1) // tk
    partial = (ki + 1) * tk > qi * tq + CHUNK
    step = functools.partial(_mha_step, q_ref, k_ref, v_ref, m_ref=m_ref, acc_ref=acc_ref,
                             heads=heads, group=group, tq=tq)

    @pl.when(ki == 0)
    def _():
        m_ref[...] = jnp.full(m_ref.shape, NEG_BIG, F32)
        acc_ref[...] = jnp.zeros(acc_ref.shape, F32)

    @pl.when(jnp.logical_not(partial))
    def _():
        step(None)

    @pl.when(partial)
    def _():
        step(_chunk_mask(qi * tq, ki * tk, tq, tk))

    @pl.when(ki == k_last)
    def _():
        o = acc_ref[:, :vh] * (1.0 / acc_ref[:, vh:])
        for h in range(heads):
            o_ref[:, h * vh:(h + 1) * vh] = o[h * tq:(h + 1) * tq].astype(o_ref.dtype)


def _mha_prompt(qcat, kcat, v, *, batch, seq, name):
    heads, _, dk = qcat.shape
    vh = v.shape[-1]
    tq = _tile(seq, _MLA_TQ)
    tk = _tile(seq, _MLA_TK)
    nq, nk = seq // tq, seq // tk
    pairs = [(qi, ki) for qi in range(nq) for ki in range(((qi + 1) * tq - 1) // tk + 1)]
    qi_tab = jnp.asarray([p[0] for p in pairs], jnp.int32)
    ki_tab = jnp.asarray([p[1] for p in pairs], jnp.int32)
    qmap = lambda b, s, qt, kt: (0, b * nq + qt[s], 0)
    kmap = lambda b, s, qt, kt: (0, b * nk + kt[s], 0)
    return pl.pallas_call(
        functools.partial(_mha_prompt_kernel, heads=heads, group=_tile(heads, _MHA_GROUP), tq=tq, tk=tk,
                          vh=vh),
        out_shape=jax.ShapeDtypeStruct((batch * seq, heads * vh), BF16),
        grid_spec=pltpu.PrefetchScalarGridSpec(
            num_scalar_prefetch=2,
            grid=(batch, len(pairs)),
            in_specs=[pl.BlockSpec((heads, tq, dk), qmap), pl.BlockSpec((heads, tk, dk), kmap),
                      pl.BlockSpec((heads, tk, vh), kmap)],
            out_specs=pl.BlockSpec((tq, heads * vh), lambda b, s, qt, kt: (b * nq + qt[s], 0)),
            scratch_shapes=[pltpu.VMEM((heads * tq, 1), F32), pltpu.VMEM((heads * tq, 2 * vh), F32)]),
        compiler_params=_cparams(("parallel", "arbitrary"), 56),
        name=name,
    )(qi_tab, ki_tab, qcat, kcat, v)


SB_SKIP = -160.0


def _sb_block(q_ref, get_kv, tri, valid, run_ref, acc_ref, *, heads, group, tq):
    rows = group * tq
    for g0 in range(0, heads, group):
        kv = [get_kv(h) for h in range(g0, g0 + group)]
        z = jnp.concatenate([_dot_nt(q_ref[g0 + i], kv[i][0]) for i in range(group)], axis=0)
        tk = z.shape[-1]
        sp = jnp.log2(1.0 + jnp.exp2(-jnp.abs(z)))
        lk = -(jnp.maximum(z, 0.0) + sp)
        if valid is not None:
            lk = jnp.where(valid[None], lk.reshape(group, tq, tk), 0.0).reshape(rows, tk)
        hi = lk.astype(BF16)
        lo = (lk - hi.astype(F32)).astype(BF16)
        suffix = _dot(hi, tri) + _dot(lo, tri)
        run = run_ref[g0 * tq:g0 * tq + rows]
        w = jnp.exp2((jnp.minimum(z, 0.0) - sp) + suffix + run)
        if valid is not None:
            w = jnp.where(valid[None], w.reshape(group, tq, tk), 0.0).reshape(rows, tk)
        w = w.astype(BF16)
        run_ref[g0 * tq:g0 * tq + rows] = run + suffix[:, :1] + lk[:, :1]
        for i in range(group):
            r0 = (g0 + i) * tq
            acc_ref[r0:r0 + tq] = acc_ref[r0:r0 + tq] + _dot(w[i * tq:(i + 1) * tq], kv[i][1])


def _strict_lower(n):
    row = lax.broadcasted_iota(jnp.int32, (n, n), 0)
    col = lax.broadcasted_iota(jnp.int32, (n, n), 1)
    return col < row


def _all_below(run_ref, bound):
    m = jnp.max(run_ref[...], axis=0, keepdims=True)
    return m[0, 0] <= bound


def _sb_write(o_ref, acc_ref, heads, tq, hd):
    for h in range(heads):
        o_ref[:, h * hd:(h + 1) * hd] = acc_ref[h * tq:(h + 1) * tq].astype(o_ref.dtype)


def _sb_prompt_kernel(q_ref, kd_ref, vd_ref, tri_ref, k_hbm, v_hbm, o_ref, kbuf, vbuf, sem,
                      run_ref, acc_ref, *, heads, group, tq, hd, nq):
    b, qi = pl.program_id(0), pl.program_id(1)

    def copies(j, slot):
        row0 = pl.multiple_of((b * nq + j) * tq, tq)
        return (pltpu.make_async_copy(k_hbm.at[:, pl.ds(row0, tq), :], kbuf.at[slot], sem.at[0, slot]),
                pltpu.make_async_copy(v_hbm.at[:, pl.ds(row0, tq), :], vbuf.at[slot], sem.at[1, slot]))

    @pl.when(qi > 0)
    def _():
        for cp in copies(qi - 1, 0):
            cp.start()

    run_ref[...] = jnp.zeros(run_ref.shape, F32)
    acc_ref[...] = jnp.zeros(acc_ref.shape, F32)
    tri = tri_ref[...]
    blk = functools.partial(_sb_block, run_ref=run_ref, acc_ref=acc_ref, heads=heads, group=group,
                            tq=tq)
    blk(q_ref, lambda h: (kd_ref[h], vd_ref[h]), tri, _strict_lower(tq))

    def cond(c):
        return (c[0] >= 0) & (c[1] > 0)

    def body(c):
        j = c[0]
        slot = lax.rem(qi - 1 - j, 2)
        for cp in copies(j, slot):
            cp.wait()

        @pl.when(j > 0)
        def _():
            for cp in copies(j - 1, 1 - slot):
                cp.start()

        blk(q_ref, lambda h: (kbuf[slot, h], vbuf[slot, h]), tri, None)
        go = jnp.where(_all_below(run_ref, SB_SKIP), 0, 1).astype(jnp.int32)
        return (j - 1, go)

    j_end, _ = lax.while_loop(cond, body, (qi - 1, jnp.int32(1)))

    @pl.when(j_end >= 0)
    def _():
        for cp in copies(j_end, lax.rem(qi - 1 - j_end, 2)):
            cp.wait()

    _sb_write(o_ref, acc_ref, heads, tq, hd)


def _sb_prompt(q, k, v, tri, *, batch, seq, name):
    heads, _, hd = q.shape
    tq = tri.shape[0]
    nq = seq // tq
    blk = lambda b, qi: (0, b * nq + qi, 0)
    return pl.pallas_call(
        functools.partial(_sb_prompt_kernel, heads=heads, group=_tile(heads, _SB_GROUP), tq=tq, hd=hd,
                          nq=nq),
        out_shape=jax.ShapeDtypeStruct((batch * seq, heads * hd), BF16),
        grid=(batch, nq),
        in_specs=[pl.BlockSpec((heads, tq, hd), blk), pl.BlockSpec((heads, tq, hd), blk),
                  pl.BlockSpec((heads, tq, hd), blk),
                  pl.BlockSpec((tq, tq), lambda b, qi: (0, 0)),
                  pl.BlockSpec(memory_space=pl.ANY), pl.BlockSpec(memory_space=pl.ANY)],
        out_specs=pl.BlockSpec((tq, heads * hd), lambda b, qi: (b * nq + qi, 0)),
        scratch_shapes=[pltpu.VMEM((2, heads, tq, hd), BF16), pltpu.VMEM((2, heads, tq, hd), BF16),
                        pltpu.SemaphoreType.DMA((2, 2)),
                        pltpu.VMEM((heads * tq, 1), F32), pltpu.VMEM((heads * tq, hd), F32)],
        compiler_params=_cparams(("arbitrary", "arbitrary"), 40),
        name=name,
    )(q, k, v, tri, k, v)


def _sb_sample_kernel(q_ref, nk_ref, nv_ref, tri_ref, ck_hbm, cv_hbm, o_ref, kbuf, vbuf, sem,
                      run_ref, acc_ref, *, heads, group, tq, tk, hd, nkc):
    b = pl.program_id(0)

    def copies(j, slot):
        p0 = pl.multiple_of(j * tk, tk)
        return (pltpu.make_async_copy(ck_hbm.at[b, pl.ds(p0, tk)], kbuf.at[slot], sem.at[0, slot]),
                pltpu.make_async_copy(cv_hbm.at[b, pl.ds(p0, tk)], vbuf.at[slot], sem.at[1, slot]))

    for cp in copies(nkc - 1, 0):
        cp.start()

    run_ref[...] = jnp.zeros(run_ref.shape, F32)
    acc_ref[...] = jnp.zeros(acc_ref.shape, F32)
    blk = functools.partial(_sb_block, run_ref=run_ref, acc_ref=acc_ref, heads=heads, group=group,
                            tq=tq)
    blk(q_ref, lambda h: (nk_ref[h], nv_ref[h]), tri_ref[:tq, :tq], _strict_lower(tq))
    tri = tri_ref[...]

    def cond(c):
        return (c[0] >= 0) & (c[1] > 0)

    def body(c):
        j = c[0]
        slot = lax.rem(nkc - 1 - j, 2)
        for cp in copies(j, slot):
            cp.wait()

        @pl.when(j > 0)
        def _():
            for cp in copies(j - 1, 1 - slot):
                cp.start()

        blk(q_ref, lambda h: (kbuf[slot, :, h, :].astype(BF16), vbuf[slot, :, h, :].astype(BF16)),
            tri, None)
        go = jnp.where(_all_below(run_ref, SB_SKIP), 0, 1).astype(jnp.int32)
        return (j - 1, go)

    j_end, _ = lax.while_loop(cond, body, (jnp.int32(nkc - 1), jnp.int32(1)))

    @pl.when(j_end >= 0)
    def _():
        for cp in copies(j_end, lax.rem(nkc - 1 - j_end, 2)):
            cp.wait()

    _sb_write(o_ref, acc_ref, heads, tq, hd)


def _sb_sample(q, k_new, v_new, cache_k, cache_v, tri, *, name):
    heads, rows, hd = q.shape
    batch, past = cache_k.shape[:2]
    tq = rows // batch
    tk = tri.shape[0]
    nkc = past // tk
    new = lambda b: (0, b, 0)
    return pl.pallas_call(
        functools.partial(_sb_sample_kernel, heads=heads, group=_tile(heads, _SB_GROUP), tq=tq, tk=tk, hd=hd,
                          nkc=nkc),
        out_shape=jax.ShapeDtypeStruct((rows, heads * hd), BF16),
        grid=(batch,),
        in_specs=[pl.BlockSpec((heads, tq, hd), new), pl.BlockSpec((heads, tq, hd), new),
                  pl.BlockSpec((heads, tq, hd), new),
                  pl.BlockSpec((tk, tk), lambda b: (0, 0)),
                  pl.BlockSpec(memory_space=pl.ANY), pl.BlockSpec(memory_space=pl.ANY)],
        out_specs=pl.BlockSpec((tq, heads * hd), lambda b: (b, 0)),
        scratch_shapes=[pltpu.VMEM((2, tk, heads, hd), F32), pltpu.VMEM((2, tk, heads, hd), F32),
                        pltpu.SemaphoreType.DMA((2, 2)),
                        pltpu.VMEM((heads * tq, 1), F32), pltpu.VMEM((heads * tq, hd), F32)],
        compiler_params=_cparams(("arbitrary",), 40),
        name=name,
    )(q, k_new, v_new, tri, cache_k, cache_v)


def _mem_attn_kernel(x_ref, g_ref, wq_ref, mk_ref, mv_ref, wo_ref, gn_ref, xo_ref, hn_ref, o_scr,
                     *, nsub, sub, heads, hd, scale):
    x = x_ref[...]
    mq = (_dot(_rms(x, g_ref[...]).astype(BF16), wq_ref[...]) * scale).astype(BF16)
    for s in range(nsub):
        for h in range(heads):
            q = mq[s * sub:(s + 1) * sub, h * hd:(h + 1) * hd]
            k = mk_ref[s, :, h * hd:(h + 1) * hd]
            v = mv_ref[s, :, h * hd:(h + 1) * hd]
            sc = _dot_nt(q, k)
            p = jnp.exp(sc - jnp.max(sc, axis=-1, keepdims=True))
            p = p * (1.0 / jnp.sum(p, axis=-1, keepdims=True))
            o_scr[s * sub:(s + 1) * sub, h * hd:(h + 1) * hd] = _dot(p.astype(BF16), v).astype(BF16)
    xn = x + _dot(o_scr[...], wo_ref[...])
    xo_ref[...] = xn
    hn_ref[...] = _rms(xn, gn_ref[...]).astype(hn_ref.dtype)


def _mem_attn(x, g, w_mq, mem_k, mem_v, w_mo, g_next, *, sub, heads, name):
    m, d = x.shape
    nb, n_mem, width = mem_k.shape
    hd = width // heads
    tm = _tile(m, 256)
    if sub >= tm:
        nsub, rows = 1, tm
        per = sub // tm
        mmap = lambda i: (i // per, 0, 0)
    else:
        nsub, rows = tm // sub, sub
        mmap = lambda i: (i, 0, 0)
    row = lambda i: (i, 0)
    fix = lambda i: (0, 0)
    return pl.pallas_call(
        functools.partial(_mem_attn_kernel, nsub=nsub, sub=rows, heads=heads, hd=hd,
                          scale=hd ** -0.5),
        out_shape=[jax.ShapeDtypeStruct((m, d), F32), jax.ShapeDtypeStruct((m, d), BF16)],
        grid=(m // tm,),
        in_specs=[pl.BlockSpec((tm, d), row), pl.BlockSpec((1, d), fix),
                  pl.BlockSpec((d, width), fix),
                  pl.BlockSpec((nsub, n_mem, width), mmap),
                  pl.BlockSpec((nsub, n_mem, width), mmap),
                  pl.BlockSpec((width, d), fix), pl.BlockSpec((1, d), fix)],
        out_specs=[pl.BlockSpec((tm, d), row), pl.BlockSpec((tm, d), row)],
        scratch_shapes=[pltpu.VMEM((tm, width), BF16)],
        compiler_params=_cparams(("parallel",), 48),
        name=name,
    )(x, g.reshape(1, d), w_mq, mem_k, mem_v, w_mo, g_next.reshape(1, d))


def _rotate_half_rows(w):
    half = w.shape[0] // 2
    return jnp.concatenate([-w[half:], w[:half]], axis=0)


def _rotate_half_cols(w):
    half = w.shape[-1] // 2
    return jnp.concatenate([-w[..., half:], w[..., :half]], axis=-1)


def _rope_table(pos, half):
    inv = ROPE_THETA ** (-jnp.arange(half, dtype=F32) / half)
    ang = pos.astype(F32)[:, None] * inv[None, :]
    c, s = jnp.cos(ang), jnp.sin(ang)
    return jnp.concatenate([c, c, s, s], axis=-1)


def _prepare_weights(w_in, w_uq, w_uk, w_uv, w_branch_a, w_branch_b, w_out, w_mq, w_mk, w_mv, w_mo,
                     w_gate, w_up, w_down, dims):
    q_lora, kv_lora, rope, heads, nope = dims
    o_kr = q_lora + kv_lora
    o_sb = o_kr + rope
    w_in_t = jnp.swapaxes(w_in, 0, 1)
    w_lat = jnp.concatenate([w_in_t[:o_sb], _rotate_half_rows(w_in_t[o_kr:o_sb])], axis=0).astype(BF16)
    w_rest = w_in_t[o_sb:].astype(BF16)
    wq_cat = jnp.concatenate([w_uq, _rotate_half_cols(w_uq[..., nope:])], axis=-1)
    return dict(
        w_lat=w_lat,
        w_rest=w_rest,
        wq_cat=wq_cat.reshape(q_lora, -1).astype(BF16),
        wuk_t=jnp.transpose(w_uk, (1, 2, 0)).reshape(heads * nope, kv_lora).astype(BF16),
        wuv=jnp.transpose(w_uv, (1, 0, 2)).astype(BF16),
        wuk_flat=w_uk.reshape(kv_lora, -1).astype(BF16),
        wuv_flat=w_uv.reshape(kv_lora, -1).astype(BF16),
        w_ba=w_branch_a, w_bb=w_branch_b, w_out=w_out,
        w_mq=w_mq.astype(BF16), w_mk=w_mk.astype(BF16), w_mv=w_mv.astype(BF16),
        w_mo=w_mo.astype(BF16),
        w_gate=w_gate, w_up=w_up, w_down=w_down.astype(BF16),
    )


def _layer(x, pos, past, mem_k, mem_v, w, gains, b_gate, dims, *, batch, tag):
    g_mix, g_q_lat, g_kv_lat, g_xattn, g_ffn = gains
    q_lora, kv_lora, rope, heads, nope = dims
    m, d = x.shape
    t = m // batch
    sb_width = (w["w_rest"].shape[0] - 2 * d) // 3
    mla_scale = (nope + rope) ** -0.5 * LOG2E
    sb_scale = LANE ** -0.5 * LOG2E
    tm = _tile(m, _ROW_TILE)
    tn = _COL_TILE

    cs = jnp.tile(_rope_table(pos, rope // 2), (batch, 1))
    h, cqn, ckv, ckv_b, krope, krope_b = _lat_proj(
        x, g_mix, w["w_lat"], g_q_lat, g_kv_lat, cs, q_lora=q_lora, kv_lora=kv_lora, rope=rope,
        name=f"{tag}_proj_lat")
    tn_p = _tile(sb_width, _PROJ_COL_TILE)
    nsb = sb_width // tn_p
    (sbq,) = _fused_matmul(
        [h], [(0, w["w_rest"], 0)], [], [(BF16, "heads")],
        lambda accs, ex: (accs[0] * sb_scale,), n=sb_width, tm=tm, tn=tn_p, name=f"{tag}_proj_sbq",
        w_rows=True, vmem_mib=_VMEM_BIG)
    sbk, sbk_b = _fused_matmul(
        [h], [(0, w["w_rest"], nsb)], [], [(F32, "tile"), (BF16, "heads")],
        lambda accs, ex: (accs[0], accs[0]), n=sb_width, tm=tm, tn=tn_p, name=f"{tag}_proj_sbk",
        w_rows=True, vmem_mib=_VMEM_BIG)
    sbv, sbv_b = _fused_matmul(
        [h], [(0, w["w_rest"], 2 * nsb)], [], [(F32, "tile"), (BF16, "heads")],
        lambda accs, ex: (accs[0], accs[0]), n=sb_width, tm=tm, tn=tn_p, name=f"{tag}_proj_sbv",
        w_rows=True, vmem_mib=_VMEM_BIG)
    (gates,) = _fused_matmul(
        [h], [(0, w["w_rest"], 3 * nsb)], [(b_gate.reshape(1, -1), "row", 0)], [(BF16, "tile")],
        lambda accs, ex: (_sigmoid(accs[0] + ex[0]),), n=2 * d, tm=tm, tn=tn_p,
        name=f"{tag}_proj_gates", w_rows=True, vmem_mib=_VMEM_BIG)

    tri_n = _SB_TILE if t % _SB_TILE == 0 else t
    tri = (jnp.arange(tri_n)[:, None] > jnp.arange(tri_n)[None, :]).astype(BF16)
    if past is None:
        qcat = _mla_qcat(cqn, w["wq_cat"], cs, heads=heads, nope=nope, rope=rope, scale=mla_scale,
                         name=f"{tag}_mla_q")
        kcat, vmla = _mla_kv_up(ckv_b, krope_b, w["wuk_flat"], w["wuv_flat"], heads=heads,
                                name=f"{tag}_mla_kv")
        o_a = _mha_prompt(qcat, kcat, vmla, batch=batch, seq=t, name=f"{tag}_mla_attn")
        o_b = _sb_prompt(sbq, sbk_b, sbv_b, tri, batch=batch, seq=t, name=f"{tag}_sb_attn")
    else:
        c_ckv, c_kr, c_k, c_v = past
        qlat, qrope = _mla_q(cqn, w["wq_cat"], w["wuk_t"], cs, heads=heads, nope=nope, rope=rope,
                             kv_lora=kv_lora, scale=mla_scale, name=f"{tag}_mla_q")
        o_a = _mla_sample(qlat, qrope, c_ckv, jnp.swapaxes(c_kr, 1, 2), ckv_b, krope_b, w["wuv"],
                          name=f"{tag}_mla_attn")
        tri = (jnp.arange(_SB_TILE)[:, None] > jnp.arange(_SB_TILE)[None, :]).astype(BF16)
        o_b = _sb_sample(sbq, sbk_b, sbv_b, c_k, c_v, tri, name=f"{tag}_sb_attn")

    ng = d // tn
    (merged,) = _fused_matmul(
        [o_a, o_b], [(0, w["w_ba"], 0), (1, w["w_bb"], 0)],
        [(gates, "tile", 0), (gates, "tile", ng)], [(BF16, "tile")],
        lambda accs, ex: (ex[0].astype(F32) * accs[0] + ex[1].astype(F32) * accs[1],),
        n=d, tm=tm, tn=tn, name=f"{tag}_merge", vmem_mib=_VMEM_BIG)
    (x,) = _fused_matmul(
        [merged], [(0, w["w_out"], 0)], [(x, "tile", 0)], [(F32, "tile")],
        lambda accs, ex: (ex[0] + accs[0],), n=d, tm=tm, tn=tn, name=f"{tag}_out_proj", vmem_mib=_VMEM_BIG)

    mem_heads = mem_k.shape[2]
    mk = mem_k.reshape(mem_k.shape[0], mem_k.shape[1], -1).astype(BF16)
    mv = mem_v.reshape(mem_v.shape[0], mem_v.shape[1], -1).astype(BF16)
    x, hf = _mem_attn(x, g_xattn, w["w_mq"], mk, mv, w["w_mo"], g_ffn, sub=t, heads=mem_heads,
                      name=f"{tag}_mem_attn")

    d_ff = w["w_gate"].shape[1]
    tn_ff = _tile(d_ff, 256) if d_ff % 512 else 512
    (act,) = _fused_matmul(
        [hf], [(0, w["w_gate"], 0), (0, w["w_up"], 0)], [], [(BF16, "tile")],
        lambda accs, ex: (accs[0] * _sigmoid(accs[0]) * accs[1],), n=d_ff, tm=tm, tn=tn_ff,
        name=f"{tag}_ffn_up", vmem_mib=_VMEM_BIG)
    (x,) = _fused_matmul(
        [act], [(0, w["w_down"], 0)], [(x, "tile", 0)], [(F32, "tile")],
        lambda accs, ex: (ex[0] + accs[0],), n=d, tm=_tile(m, _FFN_DOWN_ROW_TILE), tn=tn,
        name=f"{tag}_ffn_down", vmem_mib=_VMEM_BIG)
    return x, (ckv, krope, sbk, sbv)


def kernel(x_prompt, x_sample, cache_mla_ckv, cache_mla_krope, cache_sb_k, cache_sb_v, cache_mem_k, cache_mem_v, mem_prompt, g_mix, w_in, b_gate, g_q_lat, w_uq, g_kv_lat, w_uk, w_uv, w_branch_a, w_branch_b, w_out, g_xattn, g_mem, w_mq, w_mk, w_mv, w_mo, g_ffn, w_gate, w_up, w_down, g_final):
    depth = w_in.shape[0]
    bp, seq, d = x_prompt.shape
    bs, dec, _ = x_sample.shape
    past_len = cache_mla_ckv.shape[2]
    q_lora, heads, qk = w_uq.shape[1:]
    kv_lora, _, nope = w_uk.shape[1:]
    rope = qk - nope
    dims = (q_lora, kv_lora, rope, heads, nope)
    sb_heads, sb_hd = cache_sb_k.shape[3:]
    n_mem, mem_heads, mem_hd = cache_mem_k.shape[2:]
    pos_p = jnp.arange(seq)
    pos_s = past_len + jnp.arange(dec)

    xp = x_prompt.reshape(bp * seq, d)
    xs = x_sample.reshape(bs * dec, d)
    outs = [[] for _ in range(10)]
    for l in range(depth):
        w = _prepare_weights(w_in[l], w_uq[l], w_uk[l], w_uv[l], w_branch_a[l], w_branch_b[l],
                             w_out[l], w_mq[l], w_mk[l], w_mv[l], w_mo[l], w_gate[l], w_up[l],
                             w_down[l], dims)
        gains = (g_mix[l], g_q_lat[l], g_kv_lat[l], g_xattn[l], g_ffn[l])
        mn = _rmsnorm(mem_prompt.reshape(bp * n_mem, d), g_mem[l], BF16, f"l{l}_norm_mem")
        mem_w = w["w_mk"].shape[1]
        mk, mv = _fused_matmul(
            [mn], [(0, w["w_mk"], 0), (0, w["w_mv"], 0)], [], [(F32, "tile"), (F32, "tile")],
            lambda accs, ex: (accs[0], accs[1]), n=mem_w, tm=_tile(bp * n_mem, 512),
            tn=_tile(mem_w, 512), name=f"l{l}_mem_kv")
        mk = mk.reshape(bp, n_mem, mem_heads, mem_hd)
        mv = mv.reshape(bp, n_mem, mem_heads, mem_hd)
        xp, (ckv, kr, k, v) = _layer(xp, pos_p, None, mk, mv, w, gains, b_gate[l], dims,
                                     batch=bp, tag=f"l{l}p")
        for lst, val in zip(outs[:6], (ckv.reshape(bp, seq, -1), kr.reshape(bp, seq, -1),
                                       k.reshape(bp, seq, sb_heads, sb_hd),
                                       v.reshape(bp, seq, sb_heads, sb_hd), mk, mv)):
            lst.append(val)
        past = (cache_mla_ckv[l], cache_mla_krope[l], cache_sb_k[l], cache_sb_v[l])
        xs, (ckv, kr, k, v) = _layer(xs, pos_s, past, cache_mem_k[l], cache_mem_v[l], w, gains,
                                     b_gate[l], dims, batch=bs, tag=f"l{l}s")
        for lst, val in zip(outs[6:], (ckv.reshape(bs, dec, -1), kr.reshape(bs, dec, -1),
                                       k.reshape(bs, dec, sb_heads, sb_hd),
                                       v.reshape(bs, dec, sb_heads, sb_hd))):
            lst.append(val)
    y_prompt = _rmsnorm(xp, g_final, F32, "final_norm_p").reshape(bp, seq, d)
    y_sample = _rmsnorm(xs, g_final, F32, "final_norm_s").reshape(bs, dec, d)
    return (y_prompt, y_sample) + tuple(jnp.stack(o) for o in outs)
```
